```python
import math
import jax, jax.numpy as jnp
from jax import lax
import numpy as np

D_MODEL = 1024
BATCH = 16
SEQ = 2048
DEPTH = 4

H_A = 8
HD_A = 64
DILATED_PAIRS = ((128, 1), (512, 4), (2048, 16))
H_B = 8
HD_B = 64
H_C = 8
Q_LORA = 256
KV_LORA = 128
NOPE_C = 64
ROPE_C = 32
V_C = 64
ROPE_THETA = 10000.0
N_BRANCHES = 3
BRANCH_WIDTH = 512
N_BUCKETS = 32
MAX_DISTANCE = 2048
Q_BLOCK = 128
N_GROUPS = 4
EXPERTS_PER_GROUP = 4
N_EXPERTS = 16
TOP_K = 2
D_FF_EXPERT = 256
EPS = 1e-6
IN_SPLIT_SIZES = (3 * H_A * HD_A, 3 * H_B * HD_B, H_B, Q_LORA, KV_LORA, ROPE_C, N_BRANCHES * D_MODEL)
D_IN = 3 * H_A * HD_A + 3 * H_B * HD_B + H_B + Q_LORA + KV_LORA + ROPE_C + N_BRANCHES * D_MODEL

kernel_name = 'hybrid_dilated_fox_mla_hiermoe'


def rms_norm(x, g):
    xf = x.astype(jnp.float32)
    y = xf * lax.rsqrt(jnp.mean(xf * xf, axis=-1, keepdims=True) + EPS)
    return (y * g.astype(jnp.float32)).astype(x.dtype)


def t5_bucket(dist):
    max_exact = N_BUCKETS // 2
    log_ratio = np.log(np.maximum(dist, max_exact) / max_exact) / np.log(MAX_DISTANCE / max_exact)
    large = np.minimum(max_exact + (log_ratio * (N_BUCKETS - max_exact)).astype(np.int32), N_BUCKETS - 1)
    return np.where(dist < max_exact, dist, large).astype(np.int32)


def rope_tables(positions, dim):
    half = dim // 2
    inv_freq = ROPE_THETA ** (-jnp.arange(half, dtype=jnp.float32) / half)
    ang = positions.astype(jnp.float32)[..., None] * inv_freq
    return jnp.cos(ang)[:, :, None, :], jnp.sin(ang)[:, :, None, :]


def apply_rope(x, cos, sin):
    half = x.shape[-1] // 2
    x1, x2 = x[..., :half], x[..., half:]
    c, s = cos.astype(x.dtype), sin.astype(x.dtype)
    return jnp.concatenate([x1 * c - x2 * s, x1 * s + x2 * c], axis=-1)


def dilated_branch(q, k, v, rel_bias, window_sub, dil):
    b, s_len, h, hd = q.shape
    n_sub = s_len // dil
    nb = -(-n_sub // Q_BLOCK)
    pad_end = nb * Q_BLOCK - n_sub

    def to_sub(t):
        return t.reshape(b, n_sub, dil, h, hd).transpose(0, 2, 3, 1, 4)

    qs = jnp.pad(to_sub(q), ((0, 0), (0, 0), (0, 0), (0, pad_end), (0, 0))).reshape(b, dil, h, nb, Q_BLOCK, hd)

    def kv_blocks(t):
        t = jnp.pad(to_sub(t), ((0, 0), (0, 0), (0, 0), (Q_BLOCK, pad_end), (0, 0)))
        t = t.reshape(b, dil, h, nb + 1, Q_BLOCK, hd)
        return jnp.concatenate([t[:, :, :, :-1], t[:, :, :, 1:]], axis=4)

    kb, vb = kv_blocks(k), kv_blocks(v)
    qi = np.arange(Q_BLOCK)[:, None]
    km = np.arange(2 * Q_BLOCK)[None, :]
    dist = qi + Q_BLOCK - km
    band = (dist >= 0) & (dist <= window_sub)
    valid = band[None] & ((np.arange(nb)[:, None, None] > 0) | (km[None] >= Q_BLOCK))
    bias = rel_bias[t5_bucket(np.maximum(dist, 0) * dil)].astype(jnp.float32).transpose(2, 0, 1)
    scores = jnp.einsum('bchnqd,bchnkd->bchnqk', qs, kb, preferred_element_type=jnp.float32) / math.sqrt(hd)
    scores = jnp.where(valid, scores + bias[:, None], -jnp.inf)
    m = jnp.max(scores, axis=-1, keepdims=True)
    p = jnp.exp(scores - m)
    den = jnp.sum(p, axis=-1, keepdims=True)
    o = jnp.einsum('bchnqk,bchnkd->bchnqd', p.astype(v.dtype), vb, preferred_element_type=jnp.float32) / den
    lse = (m + jnp.log(den))[..., 0]
    o = o.reshape(b, dil, h, nb * Q_BLOCK, hd)[:, :, :, :n_sub].transpose(0, 3, 1, 2, 4).reshape(b, s_len, h, hd)
    lse = lse.reshape(b, dil, h, nb * Q_BLOCK)[:, :, :, :n_sub].transpose(0, 3, 1, 2).reshape(b, s_len, h)
    return o, lse


def dilated_mixture(q, k, v, rel_bias):
    outs, lses = [], []
    for window, dil in DILATED_PAIRS:
        o, lse = dilated_branch(q, k, v, rel_bias, window // dil, dil)
        outs.append(o)
        lses.append(lse)
    w = jax.nn.softmax(jnp.stack(lses, axis=0), axis=0)
    o = jnp.sum(jnp.stack(outs, axis=0) * w[..., None], axis=0)
    return o.astype(q.dtype)


def causal_block_attention(q, k, v, scale, log_decay_cum=None):
    b, h, s_len, _ = q.shape
    dv = v.shape[-1]
    nb = s_len // Q_BLOCK
    kpos = jnp.arange(s_len)

    def one_block(j):
        start = j * Q_BLOCK
        qb = lax.dynamic_slice_in_dim(q, start, Q_BLOCK, axis=2)
        s = jnp.einsum('bhqd,bhkd->bhqk', qb, k, preferred_element_type=jnp.float32) * scale
        if log_decay_cum is not None:
            cq = lax.dynamic_slice_in_dim(log_decay_cum, start, Q_BLOCK, axis=2)
            s = s + cq[..., :, None] - log_decay_cum[..., None, :]
        qpos = start + jnp.arange(Q_BLOCK)
        s = jnp.where(kpos[None, :] <= qpos[:, None], s, -jnp.inf)
        p = jax.nn.softmax(s, axis=-1)
        return jnp.einsum('bhqk,bhkd->bhqd', p.astype(v.dtype), v)

    out = lax.map(one_block, jnp.arange(nb))
    return jnp.moveaxis(out, 0, 2).reshape(b, h, s_len, dv)


def hierarchical_moe(xn, w_rg, b_rg, w_re, b_re, w_g, w_u, w_d):
    b, s_len, d = xn.shape
    t = xn.reshape(-1, d)
    g_logits = jnp.einsum('td,dg->tg', t, w_rg, preferred_element_type=jnp.float32) + b_rg.astype(jnp.float32)
    p_group = jax.nn.softmax(g_logits, axis=-1)
    pg_top, g_idx = lax.top_k(p_group, 1)
    e_logits = (jnp.einsum('td,de->te', t, w_re, preferred_element_type=jnp.float32)
                + b_re.astype(jnp.float32)).reshape(-1, N_GROUPS, EXPERTS_PER_GROUP)
    e_sel = jnp.einsum('tge,tg->te', e_logits, jax.nn.one_hot(g_idx[:, 0], N_GROUPS, dtype=jnp.float32))
    ev, e_idx = lax.top_k(e_sel, TOP_K)
    w_e = jax.nn.softmax(ev, axis=-1) * pg_top
    gid = g_idx * EXPERTS_PER_GROUP + e_idx
    gates = jnp.sum(jax.nn.one_hot(gid, N_EXPERTS, dtype=jnp.float32) * w_e[..., None], axis=1).astype(t.dtype)
    y = jnp.zeros_like(t)
    for e in range(N_EXPERTS):
        hid = jax.nn.silu(t @ w_g[e]) * (t @ w_u[e])
        y = y + gates[:, e:e + 1] * (hid @ w_d[e])
    return y.reshape(b, s_len, d)


def setup_inputs(seed: int = 0) -> dict:
    key = jax.random.key(seed)
    ks = jax.random.split(key, 26)
    f32 = jnp.float32
    L = DEPTH

    def dense(k, shape, fan_in):
        return jax.random.normal(k, shape, f32) * fan_in ** -0.5

    def gain(k, shape):
        return 1.0 + 0.05 * jax.random.normal(k, shape, f32)

    x = jax.random.normal(ks[0], (BATCH, SEQ, D_MODEL), f32)
    offset = jax.random.randint(ks[1], (BATCH, 1), 0, 4096, dtype=jnp.int32)
    positions = offset + jnp.arange(SEQ, dtype=jnp.int32)[None, :]
    return {
        'x': x,
        'positions': positions,
        'rel_bias': 0.1 * jax.random.normal(ks[2], (N_BUCKETS, H_A), f32),
        'norm_mix': gain(ks[3], (L, D_MODEL)),
        'w_in': dense(ks[4], (L, D_MODEL, D_IN), D_MODEL),
        'b_forget': 1.0 + 2.0 * jax.random.uniform(ks[5], (L, H_B), f32),
        'gq_a': gain(ks[6], (L, HD_A)),
        'gk_a': gain(ks[7], (L, HD_A)),
        'gq_b': gain(ks[8], (L, HD_B)),
        'gk_b': gain(ks[9], (L, HD_B)),
        'gq_c': gain(ks[10], (L, NOPE_C + ROPE_C)),
        'gk_c': gain(ks[11], (L, NOPE_C + ROPE_C)),
        'norm_cq': gain(ks[12], (L, Q_LORA)),
        'norm_ckv': gain(ks[13], (L, KV_LORA)),
        'w_uq': dense(ks[14], (L, Q_LORA, H_C * (NOPE_C + ROPE_C)), Q_LORA),
        'w_ukv': dense(ks[15], (L, KV_LORA, H_C * (NOPE_C + V_C)), KV_LORA),
        'w_branch': dense(ks[16], (L, N_BRANCHES, BRANCH_WIDTH, D_MODEL), BRANCH_WIDTH),
        'w_out': dense(ks[17], (L, D_MODEL, D_MODEL), D_MODEL),
        'norm_ffn': gain(ks[18], (L, D_MODEL)),
        'w_router_group': dense(ks[19], (L, D_MODEL, N_GROUPS), D_MODEL),
        'b_router_group': 0.01 * jax.random.normal(ks[20], (L, N_GROUPS), f32),
        'w_router_expert': dense(ks[21], (L, D_MODEL, N_EXPERTS), D_MODEL),
        'b_router_expert': 0.01 * jax.random.normal(ks[22], (L, N_EXPERTS), f32),
        'w_expert_gate': dense(ks[23], (L, N_EXPERTS, D_MODEL, D_FF_EXPERT), D_MODEL),
        'w_expert_up': dense(ks[24], (L, N_EXPERTS, D_MODEL, D_FF_EXPERT), D_MODEL),
        'w_expert_down': dense(ks[25], (L, N_EXPERTS, D_FF_EXPERT, D_MODEL), D_FF_EXPERT),
    }


def reference(x, positions, rel_bias, norm_mix, w_in, b_forget, gq_a, gk_a, gq_b, gk_b, gq_c, gk_c,
              norm_cq, norm_ckv, w_uq, w_ukv, w_branch, w_out, norm_ffn, w_router_group, b_router_group,
              w_router_expert, b_router_expert, w_expert_gate, w_expert_up, w_expert_down):
    b, s_len, d = x.shape
    cos, sin = rope_tables(positions, ROPE_C)
    split_points = [int(i) for i in np.cumsum(IN_SPLIT_SIZES)[:-1]]
    for l in range(DEPTH):
        xn = rms_norm(x, norm_mix[l])
        h = xn @ w_in[l]
        ha, hb, hf, hcq, hckv, hkr, hg = jnp.split(h, split_points, axis=-1)

        qa, ka, va = [t.reshape(b, s_len, H_A, HD_A) for t in jnp.split(ha, 3, axis=-1)]
        o_a = dilated_mixture(rms_norm(qa, gq_a[l]), rms_norm(ka, gk_a[l]), va, rel_bias)
        o_a = o_a.reshape(b, s_len, BRANCH_WIDTH)

        qb, kb, vb = [t.reshape(b, s_len, H_B, HD_B).transpose(0, 2, 1, 3) for t in jnp.split(hb, 3, axis=-1)]
        log_f = jax.nn.log_sigmoid(hf.astype(jnp.float32) + b_forget[l].astype(jnp.float32))
        cum = jnp.cumsum(log_f, axis=1).transpose(0, 2, 1)
        o_b = causal_block_attention(rms_norm(qb, gq_b[l]), rms_norm(kb, gk_b[l]), vb, 1.0 / math.sqrt(HD_B), cum)
        o_b = o_b.transpose(0, 2, 1, 3).reshape(b, s_len, BRANCH_WIDTH)

        qc = (rms_norm(hcq, norm_cq[l]) @ w_uq[l]).reshape(b, s_len, H_C, NOPE_C + ROPE_C)
        kvc = (rms_norm(hckv, norm_ckv[l]) @ w_ukv[l]).reshape(b, s_len, H_C, NOPE_C + V_C)
        q_nope = rms_norm(qc[..., :NOPE_C], gq_c[l][:NOPE_C])
        q_rope = apply_rope(rms_norm(qc[..., NOPE_C:], gq_c[l][NOPE_C:]), cos, sin)
        k_nope = rms_norm(kvc[..., :NOPE_C], gk_c[l][:NOPE_C])
        vc = kvc[..., NOPE_C:]
        k_rope = apply_rope(rms_norm(hkr, gk_c[l][NOPE_C:])[:, :, None, :], cos, sin)
        q_full = jnp.concatenate([q_nope, q_rope], axis=-1).transpose(0, 2, 1, 3)
        k_full = jnp.concatenate([k_nope, jnp.broadcast_to(k_rope, (b, s_len, H_C, ROPE_C))], axis=-1).transpose(0, 2, 1, 3)
        o_c = causal_block_attention(q_full, k_full, vc.transpose(0, 2, 1, 3), 1.0 / math.sqrt(NOPE_C + ROPE_C))
        o_c = o_c.transpose(0, 2, 1, 3).reshape(b, s_len, BRANCH_WIDTH)

        branches = jnp.stack([o_a, o_b, o_c], axis=2)
        proj = jnp.einsum('bsgc,gcd->bsgd', branches, w_branch[l])
        gates = jax.nn.sigmoid(hg.reshape(b, s_len, N_BRANCHES, d))
        x = x + jnp.sum(gates * proj, axis=2) @ w_out[l]

        x = x + hierarchical_moe(rms_norm(x, norm_ffn[l]), w_router_group[l], b_router_group[l],
                                 w_router_expert[l], b_router_expert[l], w_expert_gate[l],
                                 w_expert_up[l], w_expert_down[l])
    return x
```

```python
import functools
import math

import numpy as np
import jax
import jax.numpy as jnp
from jax import lax
from jax.experimental import pallas as pl
from jax.experimental.pallas import tpu as pltpu

F32 = jnp.float32
BF16 = jnp.bfloat16

D_MODEL = 1024
N_HEADS = 8
HEAD_DIM = 64
N_PAIRS = N_HEADS // 2
PAIR_W = 2 * HEAD_DIM
DILATED_PAIRS = ((128, 1), (512, 4), (2048, 16))
Q_LORA = 256
KV_LORA = 128
NOPE = 64
ROPE = 32
ROPE_HALF = ROPE // 2
ROPE_THETA = 10000.0
N_BUCKETS = 32
MAX_DISTANCE = 2048
Q_BLOCK = 128
N_GROUPS = 4
EXPERTS_PER_GROUP = 4
N_EXPERTS = 16
D_FF = 256
EPS = 1e-6
LANES = 128

VMEM_LIMIT_BYTES = 56 * 1024 * 1024

FRONT_TM = 512
MERGE_TM = 512
MOE_TM = 1024
FLASH_T = 256

C_A = 0
C_B = 1536
C_CQ = 3072
C_CKV = C_CQ + Q_LORA
C_KRA = C_CKV + KV_LORA
C_KRB = C_KRA + LANES
C_HF = C_KRB + LANES
N_FRONT = C_HF + LANES
HF_ROWS = 16


def _nt_dot(a, b):
    return lax.dot_general(a, b, (((1,), (1,)), ((), ())), preferred_element_type=F32)


def _dot(a, b):
    return jnp.dot(a, b, preferred_element_type=F32)


def _log_sigmoid(x):
    return jnp.minimum(x, 0.0) - jnp.log1p(jnp.exp(-jnp.abs(x)))


def _split_hi_lo(x):
    hi = x.astype(BF16)
    lo = (x - hi.astype(F32)).astype(BF16)
    return hi, lo


def _front_kernel(x_ref, gmix_ref, wf_ref, wft_ref, wuq_ref, wukv_ref, g64_ref, g32_ref, tri_ref,
                  g512_ref, g128_ref, ncq_ref, nckv_ref, bfrow_ref, ca_ref, sb_ref,
                  qa_ref, ka_ref, va_ref, qb_ref, kb_ref, vb_ref, qc_ref, kc_ref, vc_ref,
                  ccol_ref, crow_ref, carry_col, carry_row, *, tiles_per_seq):
    i = pl.program_id(0)

    @pl.when(i % tiles_per_seq == 0)
    def _():
        carry_col[...] = jnp.zeros_like(carry_col)
        carry_row[...] = jnp.zeros_like(carry_row)

    x = x_ref[...]
    ms = jnp.mean(x * x, axis=-1, keepdims=True)
    xn = (x * lax.rsqrt(ms + EPS) * gmix_ref[...]).astype(BF16)

    def group_norm(h, gmat, n, gain):
        ss = _dot((h * h).astype(BF16), gmat)
        return h * lax.rsqrt(ss * (1.0 / n) + EPS) * gain

    def store_pairs(ref, val, lo=0, w=PAIR_W):
        for p in range(N_PAIRS):
            ref[p, :, lo:lo + w] = val[:, p * PAIR_W:(p + 1) * PAIR_W].astype(BF16)

    g64 = g64_ref[...]
    for base, q_ref, k_ref, v_ref, row in ((C_A, qa_ref, ka_ref, va_ref, 0), (C_B, qb_ref, kb_ref, vb_ref, 2)):
        hq = _dot(xn, wf_ref[:, base:base + 512])
        store_pairs(q_ref, group_norm(hq, g64, HEAD_DIM, g512_ref[row:row + 1, :]))
        hk = _dot(xn, wf_ref[:, base + 512:base + 1024])
        store_pairs(k_ref, group_norm(hk, g64, HEAD_DIM, g512_ref[row + 1:row + 2, :]))
        hv = _dot(xn, wf_ref[:, base + 1024:base + 1536])
        store_pairs(v_ref, hv)

    hs = _dot(xn, wf_ref[:, C_CQ:N_FRONT])
    ca = ca_ref[...]
    sb = sb_ref[...]

    hcq = hs[:, 0:Q_LORA]
    cq = (hcq * lax.rsqrt(jnp.mean(hcq * hcq, axis=-1, keepdims=True) + EPS) * ncq_ref[...]).astype(BF16)
    qc = _dot(cq, wuq_ref[...])
    qn = group_norm(qc[:, 0:512], g64, NOPE, g512_ref[4:5, :])
    store_pairs(qc_ref, qn, 0)
    qra = qc[:, 512:1024]
    qrb = qc[:, 1024:1536]
    rs = lax.rsqrt(_dot((qra * qra).astype(BF16), g32_ref[...]) * (1.0 / ROPE) + EPS)
    ga = g128_ref[0:1, :]
    gb = g128_ref[1:2, :]
    for p in range(N_PAIRS):
        sl = slice(p * PAIR_W, (p + 1) * PAIR_W)
        qr = (qra[:, sl] * ga * ca + qrb[:, sl] * gb * sb) * rs[:, sl]
        qc_ref[p, :, PAIR_W:2 * PAIR_W] = qr.astype(BF16)

    hckv = hs[:, Q_LORA:Q_LORA + KV_LORA]
    ckv = (hckv * lax.rsqrt(jnp.mean(hckv * hckv, axis=-1, keepdims=True) + EPS) * nckv_ref[...]).astype(BF16)
    kv = _dot(ckv, wukv_ref[...])
    store_pairs(kc_ref, group_norm(kv[:, 0:512], g64, NOPE, g512_ref[5:6, :]), 0)
    store_pairs(vc_ref, kv[:, 512:1024])
    kra = hs[:, C_KRA - C_CQ:C_KRA - C_CQ + LANES]
    krb = hs[:, C_KRB - C_CQ:C_KRB - C_CQ + LANES]
    rsk = lax.rsqrt(_dot((kra * kra).astype(BF16), g32_ref[0:LANES, 0:LANES]) * (1.0 / ROPE) + EPS)
    kr = ((kra * g128_ref[2:3, :] * ca + krb * g128_ref[3:4, :] * sb) * rsk).astype(BF16)
    for p in range(N_PAIRS):
        kc_ref[p, :, PAIR_W:2 * PAIR_W] = kr

    tri = tri_ref[...]
    hf_col = hs[:, C_HF - C_CQ:C_HF - C_CQ + LANES] + g128_ref[4:5, :]
    hi, lo = _split_hi_lo(_log_sigmoid(hf_col))
    ccol = _dot(tri, hi) + _dot(tri, lo) + carry_col[0:1, :]
    ccol_ref[...] = ccol
    tm = ccol.shape[0]
    carry_col[...] = jnp.broadcast_to(ccol[tm - 1:tm, :], carry_col.shape)

    hf_row = _nt_dot(wft_ref[...], xn) + bfrow_ref[:, 0:1]
    hi, lo = _split_hi_lo(_log_sigmoid(hf_row))
    crow = _nt_dot(hi, tri) + _nt_dot(lo, tri) + carry_row[:, 0:1]
    crow_ref[...] = crow
    carry_row[...] = jnp.broadcast_to(crow[:, tm - 1:tm], carry_row.shape)


def _front_call(x2d, cw, lw, seq_len):
    t = x2d.shape[0]
    tm = FRONT_TM
    const = lambda shape: pl.BlockSpec(shape, lambda i: (0,) * len(shape))
    pair_out = lambda w: pl.BlockSpec((N_PAIRS, tm, w), lambda i: (0, i, 0))
    pair_shape = lambda w: jax.ShapeDtypeStruct((N_PAIRS, t, w), BF16)
    in_specs = [
        pl.BlockSpec((tm, D_MODEL), lambda i: (i, 0)),
        const((1, D_MODEL)),
        const((D_MODEL, N_FRONT)),
        const((HF_ROWS, D_MODEL)),
        const((Q_LORA, 1536)),
        const((KV_LORA, 1024)),
        const((512, 512)),
        const((512, 512)),
        const((tm, tm)),
        const((8, 512)),
        const((8, LANES)),
        const((1, Q_LORA)),
        const((1, KV_LORA)),
        const((HF_ROWS, LANES)),
        pl.BlockSpec((tm, LANES), lambda i: (i, 0)),
        pl.BlockSpec((tm, LANES), lambda i: (i, 0)),
    ]
    out_specs = [pair_out(PAIR_W)] * 6 + [pair_out(2 * PAIR_W), pair_out(2 * PAIR_W), pair_out(PAIR_W),
                                           pl.BlockSpec((tm, LANES), lambda i: (i, 0)),
                                           pl.BlockSpec((HF_ROWS, tm), lambda i: (0, i))]
    out_shape = [pair_shape(PAIR_W)] * 6 + [pair_shape(2 * PAIR_W), pair_shape(2 * PAIR_W), pair_shape(PAIR_W),
                                            jax.ShapeDtypeStruct((t, LANES), F32),
                                            jax.ShapeDtypeStruct((HF_ROWS, t), F32)]
    return pl.pallas_call(
        functools.partial(_front_kernel, tiles_per_seq=seq_len // tm),
        grid=(t // tm,),
        in_specs=in_specs,
        out_specs=out_specs,
        out_shape=out_shape,
        scratch_shapes=[pltpu.VMEM((8, LANES), F32), pltpu.VMEM((HF_ROWS, LANES), F32)],
        compiler_params=pltpu.CompilerParams(dimension_semantics=("arbitrary",),
                                             vmem_limit_bytes=VMEM_LIMIT_BYTES),
        name="front",
    )(x2d, lw["gmix"], lw["wf"], lw["wft"], lw["wuq"], lw["wukv"], cw["g64"], cw["g32"], cw["tri"],
      lw["g512"], lw["g128"], lw["ncq"], lw["nckv"], lw["bfrow"], cw["ca"], cw["sb"])


def _dilated_kernel(q1, k1, v1, q4, k4, v4, q16, k16, v16, bias_ref, o_ref, oacc, lacc, *, seq_len):
    lane = lax.broadcasted_iota(jnp.int32, (1, PAIR_W), 1)
    left = lane < HEAD_DIM
    views = ((1, q1, k1, v1), (4, q4, k4, v4), (16, q16, k16, v16))
    for di, (d, qr, kr, vr) in enumerate(views):
        n_blocks = seq_len // d // Q_BLOCK
        for c in range(d):
            cl = slice(c * PAIR_W, (c + 1) * PAIR_W)
            for nb in range(n_blocks):
                q = qr[nb * Q_BLOCK:(nb + 1) * Q_BLOCK, cl]
                k_lo = max(nb - 1, 0) * Q_BLOCK
                kwin = kr[k_lo:(nb + 1) * Q_BLOCK, cl]
                vwin = vr[k_lo:(nb + 1) * Q_BLOCK, cl]
                b_lo = 2 * Q_BLOCK - kwin.shape[0]
                outs, lses = [], []
                for j in range(2):
                    qh = jnp.where(left if j == 0 else jnp.logical_not(left), q, jnp.zeros_like(q))
                    s = _nt_dot(qh, kwin) + bias_ref[di, j, :, b_lo:]
                    m = jnp.max(s, axis=-1, keepdims=True)
                    p = jnp.exp(s - m)
                    den = jnp.sum(p, axis=-1, keepdims=True)
                    outs.append(_dot(p.astype(BF16), vwin) / den)
                    lses.append(m + jnp.log(den))
                o_blk = jnp.where(left, outs[0], outs[1])
                l_blk = jnp.where(left, lses[0], lses[1])
                start = nb * Q_BLOCK * d + c
                rows = pl.ds(start, Q_BLOCK) if d == 1 else pl.ds(start, Q_BLOCK, stride=d)
                oacc[di, rows, :] = o_blk
                lacc[di, rows, :] = l_blk

    l0, l1, l2 = lacc[0], lacc[1], lacc[2]
    m = jnp.maximum(jnp.maximum(l0, l1), l2)
    w0, w1, w2 = jnp.exp(l0 - m), jnp.exp(l1 - m), jnp.exp(l2 - m)
    wsum = w0 + w1 + w2
    o = oacc[0] * (w0 / wsum) + oacc[1] * (w1 / wsum) + oacc[2] * (w2 / wsum)
    o_ref[...] = o.astype(BF16)


def _dilated_call(q, k, v, bias, batch, seq_len):
    args, in_specs = [], []
    for _, d in DILATED_PAIRS:
        for a in (q, k, v):
            args.append(a.reshape(N_PAIRS, batch, seq_len // d, d * PAIR_W))
            in_specs.append(pl.BlockSpec((None, None, seq_len // d, d * PAIR_W), lambda p, b: (p, b, 0, 0)))
    args.append(bias)
    in_specs.append(pl.BlockSpec((len(DILATED_PAIRS), 2, Q_BLOCK, 2 * Q_BLOCK), lambda p, b: (0, p, 0, 0)))
    out = pl.pallas_call(
        functools.partial(_dilated_kernel, seq_len=seq_len),
        grid=(N_PAIRS, batch),
        in_specs=in_specs,
        out_specs=pl.BlockSpec((None, None, seq_len, PAIR_W), lambda p, b: (p, b, 0, 0)),
        out_shape=jax.ShapeDtypeStruct((N_PAIRS, batch, seq_len, PAIR_W), BF16),
        scratch_shapes=[pltpu.VMEM((len(DILATED_PAIRS), seq_len, PAIR_W), F32),
                        pltpu.VMEM((len(DILATED_PAIRS), seq_len, PAIR_W), F32)],
        compiler_params=pltpu.CompilerParams(dimension_semantics=("arbitrary", "arbitrary"),
                                             vmem_limit_bytes=VMEM_LIMIT_BYTES),
        name="dilated",
    )(*args)
    return out.reshape(N_PAIRS, batch * seq_len, PAIR_W)


def _flash_kernel(*refs, kw, decay):
    if decay:
        q_ref, k_ref, v_ref, crow_ref, ccol_ref, o_ref = refs
    else:
        q_ref, k_ref, v_ref, o_ref = refs
    hp = pl.program_id(0)
    qi = pl.program_id(2)
    t = FLASH_T
    q = q_ref[...]
    lane = lax.broadcasted_iota(jnp.int32, (1, kw), 1)
    in_h0 = (lane < HEAD_DIM) | ((lane >= PAIR_W) & (lane < PAIR_W + ROPE))
    in_h1 = ((lane >= HEAD_DIM) & (lane < PAIR_W)) | ((lane >= PAIR_W + ROPE) & (lane < PAIR_W + 2 * ROPE))
    row = lax.broadcasted_iota(jnp.int32, (t, t), 0)
    col = lax.broadcasted_iota(jnp.int32, (t, t), 1)
    causal = col <= row
    out_lane = lax.broadcasted_iota(jnp.int32, (1, PAIR_W), 1)
    outs = []
    for j, sel in enumerate((in_h0, in_h1)):
        qh = jnp.where(sel, q, jnp.zeros_like(q))
        if decay:
            head = (2 * hp + j).astype(F32)
            cc = ccol_ref[...]
            lane_c = lax.broadcasted_iota(jnp.int32, cc.shape, 1).astype(F32)
            cq = jnp.sum(jnp.where(lane_c == head, cc, 0.0), axis=-1, keepdims=True)

        def tile(kj, carry, masked):
            m, l, acc = carry
            ks = pl.multiple_of(kj * t, t)
            kt = k_ref[pl.ds(ks, t), :]
            vt = v_ref[pl.ds(ks, t), :]
            s = _nt_dot(qh, kt)
            if decay:
                s = s - crow_ref[pl.ds(2 * hp + j, 1), pl.ds(ks, t)]
            if masked:
                s = jnp.where(causal, s, -jnp.inf)
            mt = jnp.max(s, axis=-1, keepdims=True)
            if decay:
                m_new = jnp.maximum(m, mt + cq)
                shift = m_new - cq
            else:
                m_new = jnp.maximum(m, mt)
                shift = m_new
            alpha = jnp.exp(m - m_new)
            p = jnp.exp(s - shift)
            l = alpha * l + jnp.sum(p, axis=-1, keepdims=True)
            acc = alpha * acc + _dot(p.astype(BF16), vt)
            return m_new, l, acc

        init = (jnp.full((t, 1), -jnp.inf, F32), jnp.zeros((t, 1), F32), jnp.zeros((t, PAIR_W), F32))
        carry = lax.fori_loop(0, qi, lambda kj, c: tile(kj, c, False), init)
        _, l, acc = tile(qi, carry, True)
        outs.append(acc / l)
    o_ref[...] = jnp.where(out_lane < HEAD_DIM, outs[0], outs[1]).astype(BF16)


def _flash_call(q, k, v, batch, seq_len, crow=None, ccol=None):
    kw = q.shape[-1]
    t = FLASH_T
    decay = crow is not None
    view = lambda a: a.reshape(N_PAIRS, batch, seq_len, a.shape[-1])
    args = [view(q), view(k), view(v)]
    in_specs = [
        pl.BlockSpec((None, None, t, kw), lambda p, b, i: (p, b, i, 0)),
        pl.BlockSpec((None, None, seq_len, kw), lambda p, b, i: (p, b, 0, 0)),
        pl.BlockSpec((None, None, seq_len, PAIR_W), lambda p, b, i: (p, b, 0, 0)),
    ]
    if decay:
        args += [crow, ccol]
        in_specs += [pl.BlockSpec((HF_ROWS, seq_len), lambda p, b, i: (0, b)),
                     pl.BlockSpec((t, LANES), lambda p, b, i: (b * (seq_len // t) + i, 0))]
    out = pl.pallas_call(
        functools.partial(_flash_kernel, kw=kw, decay=decay),
        grid=(N_PAIRS, batch, seq_len // t),
        in_specs=in_specs,
        out_specs=pl.BlockSpec((None, None, t, PAIR_W), lambda p, b, i: (p, b, i, 0)),
        out_shape=jax.ShapeDtypeStruct((N_PAIRS, batch, seq_len, PAIR_W), BF16),
        compiler_params=pltpu.CompilerParams(dimension_semantics=("arbitrary",) * 3,
                                             vmem_limit_bytes=VMEM_LIMIT_BYTES),
        name="flash_fox" if decay else "flash_mla",
    )(*args)
    return out.reshape(N_PAIRS, batch * seq_len, PAIR_W)


def _merge_kernel(x_ref, oa_ref, ob_ref, oc_ref, gmix_ref, wg_ref, wb_ref, wo_ref, gffn_ref,
                  wrh_ref, wrl_ref, br_ref, x1_ref, gates_ref):
    x = x_ref[...]
    ms = jnp.mean(x * x, axis=-1, keepdims=True)
    xn = (x * lax.rsqrt(ms + EPS) * gmix_ref[...]).astype(BF16)
    mixed = None
    for g, o_ref in enumerate((oa_ref, ob_ref, oc_ref)):
        o = jnp.concatenate([o_ref[p] for p in range(N_PAIRS)], axis=-1)
        gate = jax.nn.sigmoid(_dot(xn, wg_ref[:, g * D_MODEL:(g + 1) * D_MODEL]))
        term = gate * _dot(o, wb_ref[g])
        mixed = term if mixed is None else mixed + term
    x1 = x + _dot(mixed.astype(BF16), wo_ref[...])
    x1_ref[...] = x1

    ms1 = jnp.mean(x1 * x1, axis=-1, keepdims=True)
    xf = x1 * lax.rsqrt(ms1 + EPS) * gffn_ref[...]
    hi, lo = _split_hi_lo(xf)
    wrh = wrh_ref[...]
    lg = _dot(hi, wrh) + _dot(lo, wrh) + _dot(hi, wrl_ref[...]) + br_ref[...]
    lane = lax.broadcasted_iota(jnp.int32, lg.shape, 1)
    lane_f = lane.astype(F32)
    neg = -jnp.inf
    far = float(LANES)
    gl = jnp.where((lane >= N_EXPERTS) & (lane < N_EXPERTS + N_GROUPS), lg, neg)
    gmax = jnp.max(gl, axis=-1, keepdims=True)
    pg_top = 1.0 / jnp.sum(jnp.exp(gl - gmax), axis=-1, keepdims=True)
    gidx = jnp.min(jnp.where(gl == gmax, lane_f, far), axis=-1, keepdims=True) - float(N_EXPERTS)
    in_group = (lane < N_EXPERTS) & ((lane // EXPERTS_PER_GROUP).astype(F32) == gidx)
    ev = jnp.where(in_group, lg, neg)
    v1 = jnp.max(ev, axis=-1, keepdims=True)
    i1 = jnp.min(jnp.where(ev == v1, lane_f, far), axis=-1, keepdims=True)
    ev2 = jnp.where(lane_f == i1, neg, ev)
    v2 = jnp.max(ev2, axis=-1, keepdims=True)
    i2 = jnp.min(jnp.where(ev2 == v2, lane_f, far), axis=-1, keepdims=True)
    e2 = jnp.exp(v2 - v1)
    den = 1.0 + e2
    w1 = (1.0 / den) * pg_top
    w2 = (e2 / den) * pg_top
    gates_ref[...] = jnp.where(lane_f == i1, w1, jnp.where(lane_f == i2, w2, 0.0))


def _merge_call(x2d, oa, ob, oc, lw):
    t = x2d.shape[0]
    tm = MERGE_TM
    const = lambda shape: pl.BlockSpec(shape, lambda i: (0,) * len(shape))
    row_tile = lambda w: pl.BlockSpec((tm, w), lambda i: (i, 0))
    pair_in = pl.BlockSpec((N_PAIRS, tm, PAIR_W), lambda i: (0, i, 0))
    return pl.pallas_call(
        _merge_kernel,
        grid=(t // tm,),
        in_specs=[row_tile(D_MODEL), pair_in, pair_in, pair_in, const((1, D_MODEL)),
                  const((D_MODEL, 3 * D_MODEL)), const((3, 512, D_MODEL)), const((D_MODEL, D_MODEL)),
                  const((1, D_MODEL)), const((D_MODEL, LANES)), const((D_MODEL, LANES)), const((1, LANES))],
        out_specs=[row_tile(D_MODEL), row_tile(LANES)],
        out_shape=[jax.ShapeDtypeStruct((t, D_MODEL), F32), jax.ShapeDtypeStruct((t, LANES), F32)],
        compiler_params=pltpu.CompilerParams(dimension_semantics=("arbitrary",),
                                             vmem_limit_bytes=VMEM_LIMIT_BYTES),
        name="merge",
    )(x2d, oa, ob, oc, lw["gmix"], lw["wg"], lw["wb"], lw["wo"], lw["gffn"], lw["wrh"], lw["wrl"], lw["br"])


def _moe_kernel(x_ref, gates_ref, gffn_ref, wgu_ref, wd_ref, o_ref, xn_ref):
    e = pl.program_id(1)

    @pl.when(e == 0)
    def _():
        x = x_ref[...]
        ms = jnp.mean(x * x, axis=-1, keepdims=True)
        xn_ref[...] = (x * lax.rsqrt(ms + EPS) * gffn_ref[...]).astype(BF16)
        o_ref[...] = x

    h = _dot(xn_ref[...], wgu_ref[...])
    hid = jax.nn.silu(h[:, 0:D_FF]) * h[:, D_FF:2 * D_FF]
    gates = gates_ref[...]
    lane = lax.broadcasted_iota(jnp.int32, gates.shape, 1)
    gate = jnp.sum(jnp.where(lane == e, gates, 0.0), axis=-1, keepdims=True)
    o_ref[...] += gate * _dot(hid.astype(BF16), wd_ref[...])


def _moe_call(x1, gates, lw):
    t = x1.shape[0]
    tm = MOE_TM
    return pl.pallas_call(
        _moe_kernel,
        grid=(t // tm, N_EXPERTS),
        in_specs=[pl.BlockSpec((tm, D_MODEL), lambda i, e: (i, 0)),
                  pl.BlockSpec((tm, LANES), lambda i, e: (i, 0)),
                  pl.BlockSpec((1, D_MODEL), lambda i, e: (0, 0)),
                  pl.BlockSpec((None, D_MODEL, 2 * D_FF), lambda i, e: (e, 0, 0)),
                  pl.BlockSpec((None, D_FF, D_MODEL), lambda i, e: (e, 0, 0))],
        out_specs=pl.BlockSpec((tm, D_MODEL), lambda i, e: (i, 0)),
        out_shape=jax.ShapeDtypeStruct((t, D_MODEL), F32),
        scratch_shapes=[pltpu.VMEM((tm, D_MODEL), BF16)],
        compiler_params=pltpu.CompilerParams(dimension_semantics=("arbitrary", "arbitrary"),
                                             vmem_limit_bytes=VMEM_LIMIT_BYTES),
        name="moe",
    )(x1, gates, lw["gffn"], lw["wgu"], lw["wd"])


def _t5_bucket(dist):
    max_exact = N_BUCKETS // 2
    log_ratio = np.log(np.maximum(dist, max_exact) / max_exact) / np.log(MAX_DISTANCE / max_exact)
    large = np.minimum(max_exact + (log_ratio * (N_BUCKETS - max_exact)).astype(np.int32), N_BUCKETS - 1)
    return np.where(dist < max_exact, dist, large).astype(np.int32)


def _dilated_bias(rel_bias):
    qi = np.arange(Q_BLOCK)[:, None]
    km = np.arange(2 * Q_BLOCK)[None, :]
    dist = qi + Q_BLOCK - km
    tables = []
    for window, dil in DILATED_PAIRS:
        band = (dist >= 0) & (dist <= window // dil)
        bias = rel_bias[_t5_bucket(np.maximum(dist, 0) * dil)].astype(F32).transpose(2, 0, 1)
        tables.append(jnp.where(band[None], bias, -jnp.inf))
    return jnp.stack(tables, axis=0)


def _rope_a(v1, v2):
    pad = jnp.zeros(v1.shape[:-1] + (LANES - 4 * ROPE_HALF,), v1.dtype)
    return jnp.concatenate([v1, v2, v1, v2, pad], axis=-1)


def _const_weights(positions, rel_bias):
    b, s = positions.shape
    inv_freq = ROPE_THETA ** (-jnp.arange(ROPE_HALF, dtype=F32) / ROPE_HALF)
    ang = positions.astype(F32).reshape(b * s, 1) * inv_freq
    cos, sin = jnp.cos(ang), jnp.sin(ang)
    idx = np.arange(512)
    return {
        "ca": _rope_a(cos, cos),
        "sb": _rope_a(-sin, sin),
        "g64": jnp.asarray(idx[:, None] // HEAD_DIM == idx[None, :] // HEAD_DIM, BF16),
        "g32": jnp.asarray(idx[:, None] // ROPE == idx[None, :] // ROPE, BF16),
        "tri": jnp.asarray(np.arange(FRONT_TM)[None, :] <= np.arange(FRONT_TM)[:, None], BF16),
        "bias": _dilated_bias(rel_bias),
    }


def _layer_weights(l, p):
    w_in = p["w_in"][l]
    o_hf = 3072
    o_cq = o_hf + N_HEADS
    o_ckv = o_cq + Q_LORA
    o_kr = o_ckv + KV_LORA
    o_g = o_kr + ROPE
    kr1 = w_in[:, o_kr:o_kr + ROPE_HALF]
    kr2 = w_in[:, o_kr + ROPE_HALF:o_kr + ROPE]
    hf_w = w_in[:, o_hf:o_cq]
    wf = jnp.concatenate([
        w_in[:, 0:3072], w_in[:, o_cq:o_ckv], w_in[:, o_ckv:o_kr],
        _rope_a(kr1, kr2), _rope_a(kr2, kr1),
        jnp.pad(hf_w, ((0, 0), (0, LANES - N_HEADS))),
    ], axis=1).astype(BF16)
    wft = jnp.pad(hf_w.T, ((0, HF_ROWS - N_HEADS), (0, 0))).astype(BF16)

    wuq = p["w_uq"][l].reshape(Q_LORA, N_HEADS, NOPE + ROPE)
    q_nope = wuq[:, :, :NOPE].reshape(Q_LORA, N_HEADS * NOPE)
    q1 = wuq[:, :, NOPE:NOPE + ROPE_HALF]
    q2 = wuq[:, :, NOPE + ROPE_HALF:]

    def pair_rope(v1, v2):
        x = jnp.concatenate([v1, v2], axis=-1).reshape(Q_LORA, N_PAIRS, 2 * ROPE)
        return jnp.pad(x, ((0, 0), (0, 0), (0, LANES - 2 * ROPE))).reshape(Q_LORA, N_PAIRS * LANES)

    wuq_p = jnp.concatenate([q_nope, pair_rope(q1, q2), pair_rope(q2, q1)], axis=1).astype(BF16)
    wukv = p["w_ukv"][l].reshape(KV_LORA, N_HEADS, NOPE + HEAD_DIM)
    wukv_p = jnp.concatenate([wukv[:, :, :NOPE].reshape(KV_LORA, 512),
                              wukv[:, :, NOPE:].reshape(KV_LORA, 512)], axis=1).astype(BF16)

    tile8 = lambda g: jnp.tile(g, N_HEADS)
    sc_ab = 1.0 / math.sqrt(HEAD_DIM)
    sc_c = 1.0 / math.sqrt(NOPE + ROPE)
    gq_c, gk_c = p["gq_c"][l], p["gk_c"][l]
    zeros512 = jnp.zeros((512,), F32)
    g512 = jnp.stack([tile8(p["gq_a"][l]) * sc_ab, tile8(p["gk_a"][l]),
                      tile8(p["gq_b"][l]) * sc_ab, tile8(p["gk_b"][l]),
                      tile8(gq_c[:NOPE]) * sc_c, tile8(gk_c[:NOPE]), zeros512, zeros512])
    gq1, gq2 = gq_c[NOPE:NOPE + ROPE_HALF] * sc_c, gq_c[NOPE + ROPE_HALF:] * sc_c
    gk1, gk2 = gk_c[NOPE:NOPE + ROPE_HALF], gk_c[NOPE + ROPE_HALF:]
    zeros128 = jnp.zeros((LANES,), F32)
    bf = p["b_forget"][l].astype(F32)
    g128 = jnp.stack([_rope_a(gq1, gq2), _rope_a(gq2, gq1), _rope_a(gk1, gk2), _rope_a(gk2, gk1),
                      jnp.pad(bf, (0, LANES - N_HEADS)), zeros128, zeros128, zeros128])
    bfrow = jnp.broadcast_to(jnp.pad(bf, (0, HF_ROWS - N_HEADS))[:, None], (HF_ROWS, LANES))

    wr = jnp.concatenate([p["w_router_expert"][l], p["w_router_group"][l]], axis=1)
    wr = jnp.pad(wr, ((0, 0), (0, LANES - N_EXPERTS - N_GROUPS))).astype(F32)
    wrh = wr.astype(BF16)
    wrl = (wr - wrh.astype(F32)).astype(BF16)
    br = jnp.pad(jnp.concatenate([p["b_router_expert"][l], p["b_router_group"][l]]).astype(F32),
                 (0, LANES - N_EXPERTS - N_GROUPS))[None, :]
    return {
        "gmix": p["norm_mix"][l][None, :], "wf": wf, "wft": wft, "wuq": wuq_p, "wukv": wukv_p,
        "g512": g512, "g128": g128, "ncq": p["norm_cq"][l][None, :], "nckv": p["norm_ckv"][l][None, :],
        "bfrow": bfrow,
        "wg": w_in[:, o_g:].astype(BF16), "wb": p["w_branch"][l].astype(BF16), "wo": p["w_out"][l].astype(BF16),
        "gffn": p["norm_ffn"][l][None, :], "wrh": wrh, "wrl": wrl, "br": br,
        "wgu": jnp.concatenate([p["w_expert_gate"][l], p["w_expert_up"][l]], axis=-1).astype(BF16),
        "wd": p["w_expert_down"][l].astype(BF16),
    }


def kernel(x, positions, rel_bias, norm_mix, w_in, b_forget, gq_a, gk_a, gq_b, gk_b, gq_c, gk_c, norm_cq, norm_ckv, w_uq, w_ukv, w_branch, w_out, norm_ffn, w_router_group, b_router_group, w_router_expert, b_router_expert, w_expert_gate, w_expert_up, w_expert_down):
    batch, seq_len, d_model = x.shape
    assert d_model == D_MODEL and seq_len % (Q_BLOCK * DILATED_PAIRS[-1][1]) == 0
    assert seq_len == DILATED_PAIRS[-1][0], "the widest dilated window is assumed to span the sequence"
    p = dict(norm_mix=norm_mix, w_in=w_in, b_forget=b_forget, gq_a=gq_a, gk_a=gk_a, gq_b=gq_b, gk_b=gk_b,
             gq_c=gq_c, gk_c=gk_c, norm_cq=norm_cq, norm_ckv=norm_ckv, w_uq=w_uq, w_ukv=w_ukv,
             w_branch=w_branch, w_out=w_out, norm_ffn=norm_ffn, w_router_group=w_router_group,
             b_router_group=b_router_group, w_router_expert=w_router_expert, b_router_expert=b_router_expert,
             w_expert_gate=w_expert_gate, w_expert_up=w_expert_up, w_expert_down=w_expert_down)
    cw = _const_weights(positions, rel_bias)
    xs = x.reshape(batch * seq_len, d_model)
    for l in range(norm_mix.shape[0]):
        lw = _layer_weights(l, p)
        qa, ka, va, qb, kb, vb, qc, kc, vc, ccol, crow = _front_call(xs, cw, lw, seq_len)
        oa = _dilated_call(qa, ka, va, cw["bias"], batch, seq_len)
        ob = _flash_call(qb, kb, vb, batch, seq_len, crow, ccol)
        oc = _flash_call(qc, kc, vc, batch, seq_len)
        x1, gates = _merge_call(xs, oa, ob, oc, lw)
        xs = _moe_call(x1, gates, lw)
    return xs.reshape(batch, seq_len, d_model)
```

```python
import functools
import math

import numpy as np
import jax
import jax.numpy as jnp
from jax import lax
from jax.experimental import pallas as pl
from jax.experimental.pallas import tpu as pltpu

F32 = jnp.float32
BF16 = jnp.bfloat16

D_MODEL = 1024
N_HEADS = 8
HEAD_DIM = 64
N_PAIRS = N_HEADS // 2
PAIR_W = 2 * HEAD_DIM
DILATED_PAIRS = ((128, 1), (512, 4), (2048, 16))
Q_LORA = 256
KV_LORA = 128
NOPE = 64
ROPE = 32
ROPE_HALF = ROPE // 2
ROPE_THETA = 10000.0
N_BUCKETS = 32
MAX_DISTANCE = 2048
Q_BLOCK = 128
N_GROUPS = 4
EXPERTS_PER_GROUP = 4
N_EXPERTS = 16
D_FF = 256
EPS = 1e-6
LANES = 128

VMEM_LIMIT_BYTES = 56 * 1024 * 1024

FRONT_TM = 512
MERGE_TM = 512
MOE_TM = 1024
FLASH_T = 256

C_A = 0
C_B = 1536
C_CQ = 3072
C_CKV = C_CQ + Q_LORA
C_KRA = C_CKV + KV_LORA
C_KRB = C_KRA + LANES
C_HF = C_KRB + LANES
N_FRONT = C_HF + LANES
HF_ROWS = 16


def _nt_dot(a, b):
    return lax.dot_general(a, b, (((1,), (1,)), ((), ())), preferred_element_type=F32)


def _dot(a, b):
    return jnp.dot(a, b, preferred_element_type=F32)


def _log_sigmoid(x):
    return jnp.minimum(x, 0.0) - jnp.log1p(jnp.exp(-jnp.abs(x)))


def _split_hi_lo(x):
    hi = x.astype(BF16)
    lo = (x - hi.astype(F32)).astype(BF16)
    return hi, lo


def _front_kernel(x_ref, gmix_ref, wf_ref, wft_ref, wuq_ref, wukv_ref, g64_ref, g32_ref, tri_ref,
                  g512_ref, g128_ref, ncq_ref, nckv_ref, bfrow_ref, ca_ref, sb_ref,
                  qa_ref, ka_ref, va_ref, qb_ref, kb_ref, vb_ref, qc_ref, kc_ref, vc_ref,
                  ccol_ref, crow_ref, carry_col, carry_row, *, tiles_per_seq):
    i = pl.program_id(0)

    @pl.when(i % tiles_per_seq == 0)
    def _():
        carry_col[...] = jnp.zeros_like(carry_col)
        carry_row[...] = jnp.zeros_like(carry_row)

    x = x_ref[...]
    ms = jnp.mean(x * x, axis=-1, keepdims=True)
    xn = (x * lax.rsqrt(ms + EPS) * gmix_ref[...]).astype(BF16)

    def group_norm(h, gmat, n, gain):
        ss = _dot((h * h).astype(BF16), gmat)
        return h * lax.rsqrt(ss * (1.0 / n) + EPS) * gain

    def store_pairs(ref, val, lo=0, w=PAIR_W):
        for p in range(N_PAIRS):
            ref[p, :, lo:lo + w] = val[:, p * PAIR_W:(p + 1) * PAIR_W].astype(BF16)

    g64 = g64_ref[...]
    for base, q_ref, k_ref, v_ref, row in ((C_A, qa_ref, ka_ref, va_ref, 0), (C_B, qb_ref, kb_ref, vb_ref, 2)):
        hq = _dot(xn, wf_ref[:, base:base + 512])
        store_pairs(q_ref, group_norm(hq, g64, HEAD_DIM, g512_ref[row:row + 1, :]))
        hk = _dot(xn, wf_ref[:, base + 512:base + 1024])
        store_pairs(k_ref, group_norm(hk, g64, HEAD_DIM, g512_ref[row + 1:row + 2, :]))
        hv = _dot(xn, wf_ref[:, base + 1024:base + 1536])
        store_pairs(v_ref, hv)

    hs = _dot(xn, wf_ref[:, C_CQ:N_FRONT])
    ca = ca_ref[...]
    sb = sb_ref[...]

    hcq = hs[:, 0:Q_LORA]
    cq = (hcq * lax.rsqrt(jnp.mean(hcq * hcq, axis=-1, keepdims=True) + EPS) * ncq_ref[...]).astype(BF16)
    qc = _dot(cq, wuq_ref[...])
    qn = group_norm(qc[:, 0:512], g64, NOPE, g512_ref[4:5, :])
    store_pairs(qc_ref, qn, 0)
    qra = qc[:, 512:1024]
    qrb = qc[:, 1024:1536]
    rs = lax.rsqrt(_dot((qra * qra).astype(BF16), g32_ref[...]) * (1.0 / ROPE) + EPS)
    ga = g128_ref[0:1, :]
    gb = g128_ref[1:2, :]
    for p in range(N_PAIRS):
        sl = slice(p * PAIR_W, (p + 1) * PAIR_W)
        qr = (qra[:, sl] * ga * ca + qrb[:, sl] * gb * sb) * rs[:, sl]
        qc_ref[p, :, PAIR_W:2 * PAIR_W] = qr.astype(BF16)

    hckv = hs[:, Q_LORA:Q_LORA + KV_LORA]
    ckv = (hckv * lax.rsqrt(jnp.mean(hckv * hckv, axis=-1, keepdims=True) + EPS) * nckv_ref[...]).astype(BF16)
    kv = _dot(ckv, wukv_ref[...])
    store_pairs(kc_ref, group_norm(kv[:, 0:512], g64, NOPE, g512_ref[5:6, :]), 0)
    store_pairs(vc_ref, kv[:, 512:1024])
    kra = hs[:, C_KRA - C_CQ:C_KRA - C_CQ + LANES]
    krb = hs[:, C_KRB - C_CQ:C_KRB - C_CQ + LANES]
    rsk = lax.rsqrt(_dot((kra * kra).astype(BF16), g32_ref[0:LANES, 0:LANES]) * (1.0 / ROPE) + EPS)
    kr = ((kra * g128_ref[2:3, :] * ca + krb * g128_ref[3:4, :] * sb) * rsk).astype(BF16)
    for p in range(N_PAIRS):
        kc_ref[p, :, PAIR_W:2 * PAIR_W] = kr

    tri = tri_ref[...]
    hf_col = hs[:, C_HF - C_CQ:C_HF - C_CQ + LANES] + g128_ref[4:5, :]
    hi, lo = _split_hi_lo(_log_sigmoid(hf_col))
    ccol = _dot(tri, hi) + _dot(tri, lo) + carry_col[0:1, :]
    ccol_ref[...] = ccol
    tm = ccol.shape[0]
    carry_col[...] = jnp.broadcast_to(ccol[tm - 1:tm, :], carry_col.shape)

    hf_row = _nt_dot(wft_ref[...], xn) + bfrow_ref[:, 0:1]
    hi, lo = _split_hi_lo(_log_sigmoid(hf_row))
    crow = _nt_dot(hi, tri) + _nt_dot(lo, tri) + carry_row[:, 0:1]
    crow_ref[...] = crow
    carry_row[...] = jnp.broadcast_to(crow[:, tm - 1:tm], carry_row.shape)


def _front_call(x2d, cw, lw, seq_len):
    t = x2d.shape[0]
    tm = FRONT_TM
    const = lambda shape: pl.BlockSpec(shape, lambda i: (0,) * len(shape))
    pair_out = lambda w: pl.BlockSpec((N_PAIRS, tm, w), lambda i: (0, i, 0))
    pair_shape = lambda w: jax.ShapeDtypeStruct((N_PAIRS, t, w), BF16)
    in_specs = [
        pl.BlockSpec((tm, D_MODEL), lambda i: (i, 0)),
        const((1, D_MODEL)),
        const((D_MODEL, N_FRONT)),
        const((HF_ROWS, D_MODEL)),
        const((Q_LORA, 1536)),
        const((KV_LORA, 1024)),
        const((512, 512)),
        const((512, 512)),
        const((tm, tm)),
        const((8, 512)),
        const((8, LANES)),
        const((1, Q_LORA)),
        const((1, KV_LORA)),
        const((HF_ROWS, LANES)),
        pl.BlockSpec((tm, LANES), lambda i: (i, 0)),
        pl.BlockSpec((tm, LANES), lambda i: (i, 0)),
    ]
    out_specs = [pair_out(PAIR_W)] * 6 + [pair_out(2 * PAIR_W), pair_out(2 * PAIR_W), pair_out(PAIR_W),
                                           pl.BlockSpec((tm, LANES), lambda i: (i, 0)),
                                           pl.BlockSpec((HF_ROWS, tm), lambda i: (0, i))]
    out_shape = [pair_shape(PAIR_W)] * 6 + [pair_shape(2 * PAIR_W), pair_shape(2 * PAIR_W), pair_shape(PAIR_W),
                                            jax.ShapeDtypeStruct((t, LANES), F32),
                                            jax.ShapeDtypeStruct((HF_ROWS, t), F32)]
    return pl.pallas_call(
        functools.partial(_front_kernel, tiles_per_seq=seq_len // tm),
        grid=(t // tm,),
        in_specs=in_specs,
        out_specs=out_specs,
        out_shape=out_shape,
        scratch_shapes=[pltpu.VMEM((8, LANES), F32), pltpu.VMEM((HF_ROWS, LANES), F32)],
        compiler_params=pltpu.CompilerParams(dimension_semantics=("arbitrary",),
                                             vmem_limit_bytes=VMEM_LIMIT_BYTES),
        name="front",
    )(x2d, lw["gmix"], lw["wf"], lw["wft"], lw["wuq"], lw["wukv"], cw["g64"], cw["g32"], cw["tri"],
      lw["g512"], lw["g128"], lw["ncq"], lw["nckv"], lw["bfrow"], cw["ca"], cw["sb"])


def _dilated_kernel(q1, k1, v1, q4, k4, v4, q16, k16, v16, bias_ref, o_ref, oacc, lacc, *, seq_len):
    lane = lax.broadcasted_iota(jnp.int32, (1, PAIR_W), 1)
    left = lane < HEAD_DIM
    views = ((1, q1, k1, v1), (4, q4, k4, v4), (16, q16, k16, v16))
    for di, (d, qr, kr, vr) in enumerate(views):
        n_blocks = seq_len // d // Q_BLOCK
        for c in range(d):
            cl = slice(c * PAIR_W, (c + 1) * PAIR_W)
            for nb in range(n_blocks):
                q = qr[nb * Q_BLOCK:(nb + 1) * Q_BLOCK, cl]
                k_lo = max(nb - 1, 0) * Q_BLOCK
                kwin = kr[k_lo:(nb + 1) * Q_BLOCK, cl]
                vwin = vr[k_lo:(nb + 1) * Q_BLOCK, cl]
                b_lo = 2 * Q_BLOCK - kwin.shape[0]
                outs, lses = [], []
                for j in range(2):
                    qh = jnp.where(left if j == 0 else jnp.logical_not(left), q, jnp.zeros_like(q))
                    s = _nt_dot(qh, kwin) + bias_ref[di, j, :, b_lo:]
                    m = jnp.max(s, axis=-1, keepdims=True)
                    p = jnp.exp(s - m)
                    den = jnp.sum(p, axis=-1, keepdims=True)
                    outs.append(_dot(p.astype(BF16), vwin) / den)
                    lses.append(m + jnp.log(den))
                o_blk = jnp.where(left, outs[0], outs[1])
                l_blk = jnp.where(left, lses[0], lses[1])
                start = nb * Q_BLOCK * d + c
                rows = pl.ds(start, Q_BLOCK) if d == 1 else pl.ds(start, Q_BLOCK, stride=d)
                oacc[di, rows, :] = o_blk
                lacc[di, rows, :] = l_blk

    l0, l1, l2 = lacc[0], lacc[1], lacc[2]
    m = jnp.maximum(jnp.maximum(l0, l1), l2)
    w0, w1, w2 = jnp.exp(l0 - m), jnp.exp(l1 - m), jnp.exp(l2 - m)
    wsum = w0 + w1 + w2
    o = oacc[0] * (w0 / wsum) + oacc[1] * (w1 / wsum) + oacc[2] * (w2 / wsum)
    o_ref[...] = o.astype(BF16)


def _dilated_call(q, k, v, bias, batch, seq_len):
    args, in_specs = [], []
    for _, d in DILATED_PAIRS:
        for a in (q, k, v):
            args.append(a.reshape(N_PAIRS, batch, seq_len // d, d * PAIR_W))
            in_specs.append(pl.BlockSpec((None, None, seq_len // d, d * PAIR_W), lambda p, b: (p, b, 0, 0)))
    args.append(bias)
    in_specs.append(pl.BlockSpec((len(DILATED_PAIRS), 2, Q_BLOCK, 2 * Q_BLOCK), lambda p, b: (0, p, 0, 0)))
    out = pl.pallas_call(
        functools.partial(_dilated_kernel, seq_len=seq_len),
        grid=(N_PAIRS, batch),
        in_specs=in_specs,
        out_specs=pl.BlockSpec((None, None, seq_len, PAIR_W), lambda p, b: (p, b, 0, 0)),
        out_shape=jax.ShapeDtypeStruct((N_PAIRS, batch, seq_len, PAIR_W), BF16),
        scratch_shapes=[pltpu.VMEM((len(DILATED_PAIRS), seq_len, PAIR_W), F32),
                        pltpu.VMEM((len(DILATED_PAIRS), seq_len, PAIR_W), F32)],
        compiler_params=pltpu.CompilerParams(dimension_semantics=("arbitrary", "arbitrary"),
                                             vmem_limit_bytes=VMEM_LIMIT_BYTES),
        name="dilated",
    )(*args)
    return out.reshape(N_PAIRS, batch * seq_len, PAIR_W)


def _flash_kernel(*refs, kw, decay, seq_len):
    if decay:
        q_ref, k_ref, v_ref, crow_ref, ccol_ref, o_ref, s_scr = refs
    else:
        q_ref, k_ref, v_ref, o_ref, s_scr = refs
    hp = pl.program_id(0)
    t = FLASH_T
    lane = lax.broadcasted_iota(jnp.int32, (1, kw), 1)
    in_h0 = (lane < HEAD_DIM) | ((lane >= PAIR_W) & (lane < PAIR_W + ROPE))
    in_h1 = ((lane >= HEAD_DIM) & (lane < PAIR_W)) | ((lane >= PAIR_W + ROPE) & (lane < PAIR_W + 2 * ROPE))
    row = lax.broadcasted_iota(jnp.int32, (t, t), 0)
    col = lax.broadcasted_iota(jnp.int32, (t, t), 1)
    causal = col <= row
    out_lane = lax.broadcasted_iota(jnp.int32, (1, PAIR_W), 1)
    for qi in range(seq_len // t):
        rows = slice(qi * t, (qi + 1) * t)
        q = q_ref[rows, :]
        outs = []
        for j, sel in enumerate((in_h0, in_h1)):
            qh = jnp.where(sel, q, jnp.zeros_like(q))
            mx = None
            for c in range(qi + 1):
                cols = slice(c * t, (c + 1) * t)
                s = _nt_dot(qh, k_ref[cols, :])
                if decay:
                    s = s - crow_ref[pl.ds(2 * hp + j, 1), cols]
                if c == qi:
                    s = jnp.where(causal, s, -jnp.inf)
                s_scr[j, :, cols] = s
                mx = s if mx is None else jnp.maximum(mx, s)
            shift = jnp.max(mx, axis=-1, keepdims=True)
            if decay:
                head = (2 * hp + j).astype(F32)
                cc = ccol_ref[rows, :]
                lane_c = lax.broadcasted_iota(jnp.int32, cc.shape, 1).astype(F32)
                cq = jnp.sum(jnp.where(lane_c == head, cc, 0.0), axis=-1, keepdims=True)
                shift = (shift + cq) - cq
            acc = None
            lsum = None
            for c in range(qi + 1):
                cols = slice(c * t, (c + 1) * t)
                p = jnp.exp(s_scr[j, :, cols] - shift)
                pv = _dot(p.astype(BF16), v_ref[cols, :])
                lsum = p if lsum is None else lsum + p
                acc = pv if acc is None else acc + pv
            outs.append(acc / jnp.sum(lsum, axis=-1, keepdims=True))
        o_ref[rows, :] = jnp.where(out_lane < HEAD_DIM, outs[0], outs[1]).astype(BF16)


def _flash_call(q, k, v, batch, seq_len, crow=None, ccol=None):
    kw = q.shape[-1]
    decay = crow is not None
    view = lambda a: a.reshape(N_PAIRS, batch, seq_len, a.shape[-1])
    args = [view(q), view(k), view(v)]
    seq_block = lambda w: pl.BlockSpec((None, None, seq_len, w), lambda p, b: (p, b, 0, 0))
    in_specs = [seq_block(kw), seq_block(kw), seq_block(PAIR_W)]
    if decay:
        args += [crow, ccol]
        in_specs += [pl.BlockSpec((HF_ROWS, seq_len), lambda p, b: (0, b)),
                     pl.BlockSpec((seq_len, LANES), lambda p, b: (b, 0))]
    out = pl.pallas_call(
        functools.partial(_flash_kernel, kw=kw, decay=decay, seq_len=seq_len),
        grid=(N_PAIRS, batch),
        in_specs=in_specs,
        out_specs=seq_block(PAIR_W),
        out_shape=jax.ShapeDtypeStruct((N_PAIRS, batch, seq_len, PAIR_W), BF16),
        scratch_shapes=[pltpu.VMEM((2, FLASH_T, seq_len), F32)],
        compiler_params=pltpu.CompilerParams(dimension_semantics=("arbitrary",) * 2,
                                             vmem_limit_bytes=VMEM_LIMIT_BYTES),
        name="flash_fox" if decay else "flash_mla",
    )(*args)
    return out.reshape(N_PAIRS, batch * seq_len, PAIR_W)


def _merge_kernel(x_ref, oa_ref, ob_ref, oc_ref, gmix_ref, wg_ref, wb_ref, wo_ref, gffn_ref,
                  wrh_ref, wrl_ref, br_ref, x1_ref, gates_ref):
    x = x_ref[...]
    ms = jnp.mean(x * x, axis=-1, keepdims=True)
    xn = (x * lax.rsqrt(ms + EPS) * gmix_ref[...]).astype(BF16)
    mixed = None
    for g, o_ref in enumerate((oa_ref, ob_ref, oc_ref)):
        o = jnp.concatenate([o_ref[p] for p in range(N_PAIRS)], axis=-1)
        gate = jax.nn.sigmoid(_dot(xn, wg_ref[:, g * D_MODEL:(g + 1) * D_MODEL]))
        term = gate * _dot(o, wb_ref[g])
        mixed = term if mixed is None else mixed + term
    x1 = x + _dot(mixed.astype(BF16), wo_ref[...])
    x1_ref[...] = x1

    ms1 = jnp.mean(x1 * x1, axis=-1, keepdims=True)
    xf = x1 * lax.rsqrt(ms1 + EPS) * gffn_ref[...]
    hi, lo = _split_hi_lo(xf)
    wrh = wrh_ref[...]
    lg = _dot(hi, wrh) + _dot(lo, wrh) + _dot(hi, wrl_ref[...]) + br_ref[...]
    lane = lax.broadcasted_iota(jnp.int32, lg.shape, 1)
    lane_f = lane.astype(F32)
    neg = -jnp.inf
    far = float(LANES)
    gl = jnp.where((lane >= N_EXPERTS) & (lane < N_EXPERTS + N_GROUPS), lg, neg)
    gmax = jnp.max(gl, axis=-1, keepdims=True)
    pg_top = 1.0 / jnp.sum(jnp.exp(gl - gmax), axis=-1, keepdims=True)
    gidx = jnp.min(jnp.where(gl == gmax, lane_f, far), axis=-1, keepdims=True) - float(N_EXPERTS)
    in_group = (lane < N_EXPERTS) & ((lane // EXPERTS_PER_GROUP).astype(F32) == gidx)
    ev = jnp.where(in_group, lg, neg)
    v1 = jnp.max(ev, axis=-1, keepdims=True)
    i1 = jnp.min(jnp.where(ev == v1, lane_f, far), axis=-1, keepdims=True)
    ev2 = jnp.where(lane_f == i1, neg, ev)
    v2 = jnp.max(ev2, axis=-1, keepdims=True)
    i2 = jnp.min(jnp.where(ev2 == v2, lane_f, far), axis=-1, keepdims=True)
    e2 = jnp.exp(v2 - v1)
    den = 1.0 + e2
    w1 = (1.0 / den) * pg_top
    w2 = (e2 / den) * pg_top
    gates_ref[...] = jnp.where(lane_f == i1, w1, jnp.where(lane_f == i2, w2, 0.0))


def _merge_call(x2d, oa, ob, oc, lw):
    t = x2d.shape[0]
    tm = MERGE_TM
    const = lambda shape: pl.BlockSpec(shape, lambda i: (0,) * len(shape))
    row_tile = lambda w: pl.BlockSpec((tm, w), lambda i: (i, 0))
    pair_in = pl.BlockSpec((N_PAIRS, tm, PAIR_W), lambda i: (0, i, 0))
    return pl.pallas_call(
        _merge_kernel,
        grid=(t // tm,),
        in_specs=[row_tile(D_MODEL), pair_in, pair_in, pair_in, const((1, D_MODEL)),
                  const((D_MODEL, 3 * D_MODEL)), const((3, 512, D_MODEL)), const((D_MODEL, D_MODEL)),
                  const((1, D_MODEL)), const((D_MODEL, LANES)), const((D_MODEL, LANES)), const((1, LANES))],
        out_specs=[row_tile(D_MODEL), row_tile(LANES)],
        out_shape=[jax.ShapeDtypeStruct((t, D_MODEL), F32), jax.ShapeDtypeStruct((t, LANES), F32)],
        compiler_params=pltpu.CompilerParams(dimension_semantics=("arbitrary",),
                                             vmem_limit_bytes=VMEM_LIMIT_BYTES),
        name="merge",
    )(x2d, oa, ob, oc, lw["gmix"], lw["wg"], lw["wb"], lw["wo"], lw["gffn"], lw["wrh"], lw["wrl"], lw["br"])


def _moe_kernel(x_ref, gates_ref, gffn_ref, wgu_ref, wd_ref, o_ref, xn_ref):
    e = pl.program_id(1)

    @pl.when(e == 0)
    def _():
        x = x_ref[...]
        ms = jnp.mean(x * x, axis=-1, keepdims=True)
        xn_ref[...] = (x * lax.rsqrt(ms + EPS) * gffn_ref[...]).astype(BF16)
        o_ref[...] = x

    h = _dot(xn_ref[...], wgu_ref[...])
    hid = jax.nn.silu(h[:, 0:D_FF]) * h[:, D_FF:2 * D_FF]
    gates = gates_ref[...]
    lane = lax.broadcasted_iota(jnp.int32, gates.shape, 1)
    gate = jnp.sum(jnp.where(lane == e, gates, 0.0), axis=-1, keepdims=True)
    o_ref[...] += gate * _dot(hid.astype(BF16), wd_ref[...])


def _moe_call(x1, gates, lw):
    t = x1.shape[0]
    tm = MOE_TM
    return pl.pallas_call(
        _moe_kernel,
        grid=(t // tm, N_EXPERTS),
        in_specs=[pl.BlockSpec((tm, D_MODEL), lambda i, e: (i, 0)),
                  pl.BlockSpec((tm, LANES), lambda i, e: (i, 0)),
                  pl.BlockSpec((1, D_MODEL), lambda i, e: (0, 0)),
                  pl.BlockSpec((None, D_MODEL, 2 * D_FF), lambda i, e: (e, 0, 0)),
                  pl.BlockSpec((None, D_FF, D_MODEL), lambda i, e: (e, 0, 0))],
        out_specs=pl.BlockSpec((tm, D_MODEL), lambda i, e: (i, 0)),
        out_shape=jax.ShapeDtypeStruct((t, D_MODEL), F32),
        scratch_shapes=[pltpu.VMEM((tm, D_MODEL), BF16)],
        compiler_params=pltpu.CompilerParams(dimension_semantics=("arbitrary", "arbitrary"),
                                             vmem_limit_bytes=VMEM_LIMIT_BYTES),
        name="moe",
    )(x1, gates, lw["gffn"], lw["wgu"], lw["wd"])


def _t5_bucket(dist):
    max_exact = N_BUCKETS // 2
    log_ratio = np.log(np.maximum(dist, max_exact) / max_exact) / np.log(MAX_DISTANCE / max_exact)
    large = np.minimum(max_exact + (log_ratio * (N_BUCKETS - max_exact)).astype(np.int32), N_BUCKETS - 1)
    return np.where(dist < max_exact, dist, large).astype(np.int32)


def _dilated_bias(rel_bias):
    qi = np.arange(Q_BLOCK)[:, None]
    km = np.arange(2 * Q_BLOCK)[None, :]
    dist = qi + Q_BLOCK - km
    tables = []
    for window, dil in DILATED_PAIRS:
        band = (dist >= 0) & (dist <= window // dil)
        bias = rel_bias[_t5_bucket(np.maximum(dist, 0) * dil)].astype(F32).transpose(2, 0, 1)
        tables.append(jnp.where(band[None], bias, -jnp.inf))
    return jnp.stack(tables, axis=0)


def _rope_a(v1, v2):
    pad = jnp.zeros(v1.shape[:-1] + (LANES - 4 * ROPE_HALF,), v1.dtype)
    return jnp.concatenate([v1, v2, v1, v2, pad], axis=-1)


def _const_weights(positions, rel_bias):
    b, s = positions.shape
    inv_freq = ROPE_THETA ** (-jnp.arange(ROPE_HALF, dtype=F32) / ROPE_HALF)
    ang = positions.astype(F32).reshape(b * s, 1) * inv_freq
    cos, sin = jnp.cos(ang), jnp.sin(ang)
    idx = np.arange(512)
    return {
        "ca": _rope_a(cos, cos),
        "sb": _rope_a(-sin, sin),
        "g64": jnp.asarray(idx[:, None] // HEAD_DIM == idx[None, :] // HEAD_DIM, BF16),
        "g32": jnp.asarray(idx[:, None] // ROPE == idx[None, :] // ROPE, BF16),
        "tri": jnp.asarray(np.arange(FRONT_TM)[None, :] <= np.arange(FRONT_TM)[:, None], BF16),
        "bias": _dilated_bias(rel_bias),
    }


def _layer_weights(l, p):
    w_in = p["w_in"][l]
    o_hf = 3072
    o_cq = o_hf + N_HEADS
    o_ckv = o_cq + Q_LORA
    o_kr = o_ckv + KV_LORA
    o_g = o_kr + ROPE
    kr1 = w_in[:, o_kr:o_kr + ROPE_HALF]
    kr2 = w_in[:, o_kr + ROPE_HALF:o_kr + ROPE]
    hf_w = w_in[:, o_hf:o_cq]
    wf = jnp.concatenate([
        w_in[:, 0:3072], w_in[:, o_cq:o_ckv], w_in[:, o_ckv:o_kr],
        _rope_a(kr1, kr2), _rope_a(kr2, kr1),
        jnp.pad(hf_w, ((0, 0), (0, LANES - N_HEADS))),
    ], axis=1).astype(BF16)
    wft = jnp.pad(hf_w.T, ((0, HF_ROWS - N_HEADS), (0, 0))).astype(BF16)

    wuq = p["w_uq"][l].reshape(Q_LORA, N_HEADS, NOPE + ROPE)
    q_nope = wuq[:, :, :NOPE].reshape(Q_LORA, N_HEADS * NOPE)
    q1 = wuq[:, :, NOPE:NOPE + ROPE_HALF]
    q2 = wuq[:, :, NOPE + ROPE_HALF:]

    def pair_rope(v1, v2):
        x = jnp.concatenate([v1, v2], axis=-1).reshape(Q_LORA, N_PAIRS, 2 * ROPE)
        return jnp.pad(x, ((0, 0), (0, 0), (0, LANES - 2 * ROPE))).reshape(Q_LORA, N_PAIRS * LANES)

    wuq_p = jnp.concatenate([q_nope, pair_rope(q1, q2), pair_rope(q2, q1)], axis=1).astype(BF16)
    wukv = p["w_ukv"][l].reshape(KV_LORA, N_HEADS, NOPE + HEAD_DIM)
    wukv_p = jnp.concatenate([wukv[:, :, :NOPE].reshape(KV_LORA, 512),
                              wukv[:, :, NOPE:].reshape(KV_LORA, 512)], axis=1).astype(BF16)

    tile8 = lambda g: jnp.tile(g, N_HEADS)
    sc_ab = 1.0 / math.sqrt(HEAD_DIM)
    sc_c = 1.0 / math.sqrt(NOPE + ROPE)
    gq_c, gk_c = p["gq_c"][l], p["gk_c"][l]
    zeros512 = jnp.zeros((512,), F32)
    g512 = jnp.stack([tile8(p["gq_a"][l]) * sc_ab, tile8(p["gk_a"][l]),
                      tile8(p["gq_b"][l]) * sc_ab, tile8(p["gk_b"][l]),
                      tile8(gq_c[:NOPE]) * sc_c, tile8(gk_c[:NOPE]), zeros512, zeros512])
    gq1, gq2 = gq_c[NOPE:NOPE + ROPE_HALF] * sc_c, gq_c[NOPE + ROPE_HALF:] * sc_c
    gk1, gk2 = gk_c[NOPE:NOPE + ROPE_HALF], gk_c[NOPE + ROPE_HALF:]
    zeros128 = jnp.zeros((LANES,), F32)
    bf = p["b_forget"][l].astype(F32)
    g128 = jnp.stack([_rope_a(gq1, gq2), _rope_a(gq2, gq1), _rope_a(gk1, gk2), _rope_a(gk2, gk1),
                      jnp.pad(bf, (0, LANES - N_HEADS)), zeros128, zeros128, zeros128])
    bfrow = jnp.broadcast_to(jnp.pad(bf, (0, HF_ROWS - N_HEADS))[:, None], (HF_ROWS, LANES))

    wr = jnp.concatenate([p["w_router_expert"][l], p["w_router_group"][l]], axis=1)
    wr = jnp.pad(wr, ((0, 0), (0, LANES - N_EXPERTS - N_GROUPS))).astype(F32)
    wrh = wr.astype(BF16)
    wrl = (wr - wrh.astype(F32)).astype(BF16)
    br = jnp.pad(jnp.concatenate([p["b_router_expert"][l], p["b_router_group"][l]]).astype(F32),
                 (0, LANES - N_EXPERTS - N_GROUPS))[None, :]
    return {
        "gmix": p["norm_mix"][l][None, :], "wf": wf, "wft": wft, "wuq": wuq_p, "wukv": wukv_p,
        "g512": g512, "g128": g128, "ncq": p["norm_cq"][l][None, :], "nckv": p["norm_ckv"][l][None, :],
        "bfrow": bfrow,
        "wg": w_in[:, o_g:].astype(BF16), "wb": p["w_branch"][l].astype(BF16), "wo": p["w_out"][l].astype(BF16),
        "gffn": p["norm_ffn"][l][None, :], "wrh": wrh, "wrl": wrl, "br": br,
        "wgu": jnp.concatenate([p["w_expert_gate"][l], p["w_expert_up"][l]], axis=-1).astype(BF16),
        "wd": p["w_expert_down"][l].astype(BF16),
    }


def kernel(x, positions, rel_bias, norm_mix, w_in, b_forget, gq_a, gk_a, gq_b, gk_b, gq_c, gk_c, norm_cq, norm_ckv, w_uq, w_ukv, w_branch, w_out, norm_ffn, w_router_group, b_router_group, w_router_expert, b_router_expert, w_expert_gate, w_expert_up, w_expert_down):
    batch, seq_len, d_model = x.shape
    assert d_model == D_MODEL and seq_len % (Q_BLOCK * DILATED_PAIRS[-1][1]) == 0
    assert seq_len == DILATED_PAIRS[-1][0], "the widest dilated window is assumed to span the sequence"
    p = dict(norm_mix=norm_mix, w_in=w_in, b_forget=b_forget, gq_a=gq_a, gk_a=gk_a, gq_b=gq_b, gk_b=gk_b,
             gq_c=gq_c, gk_c=gk_c, norm_cq=norm_cq, norm_ckv=norm_ckv, w_uq=w_uq, w_ukv=w_ukv,
             w_branch=w_branch, w_out=w_out, norm_ffn=norm_ffn, w_router_group=w_router_group,
             b_router_group=b_router_group, w_router_expert=w_router_expert, b_router_expert=b_router_expert,
             w_expert_gate=w_expert_gate, w_expert_up=w_expert_up, w_expert_down=w_expert_down)
    cw = _const_weights(positions, rel_bias)
    xs = x.reshape(batch * seq_len, d_model)
    for l in range(norm_mix.shape[0]):
        lw = _layer_weights(l, p)
        qa, ka, va, qb, kb, vb, qc, kc, vc, ccol, crow = _front_call(xs, cw, lw, seq_len)
        oa = _dilated_call(qa, ka, va, cw["bias"], batch, seq_len)
        ob = _flash_call(qb, kb, vb, batch, seq_len, crow, ccol)
        oc = _flash_call(qc, kc, vc, batch, seq_len)
        x1, gates = _merge_call(xs, oa, ob, oc, lw)
        xs = _moe_call(x1, gates, lw)
    return xs.reshape(batch, seq_len, d_model)
```

```python
import functools
import math

import numpy as np
import jax
import jax.numpy as jnp
from jax import lax
from jax.experimental import pallas as pl
from jax.experimental.pallas import tpu as pltpu

F32 = jnp.float32
BF16 = jnp.bfloat16

D_MODEL = 1024
N_HEADS = 8
HEAD_DIM = 64
N_PAIRS = N_HEADS // 2
PAIR_W = 2 * HEAD_DIM
DILATED_PAIRS = ((128, 1), (512, 4), (2048, 16))
Q_LORA = 256
KV_LORA = 128
NOPE = 64
ROPE = 32
ROPE_HALF = ROPE // 2
ROPE_THETA = 10000.0
N_BUCKETS = 32
MAX_DISTANCE = 2048
Q_BLOCK = 128
N_GROUPS = 4
EXPERTS_PER_GROUP = 4
N_EXPERTS = 16
D_FF = 256
EPS = 1e-6
LANES = 128

VMEM_LIMIT_BYTES = 56 * 1024 * 1024

FRONT_TM = 512
MERGE_TM = 512
MOE_TM = 1024
FLASH_T = 256

C_A = 0
C_B = 1536
C_CQ = 3072
C_CKV = C_CQ + Q_LORA
C_KRA = C_CKV + KV_LORA
C_KRB = C_KRA + LANES
C_HF = C_KRB + LANES
N_FRONT = C_HF + LANES
HF_ROWS = 16


def _nt_dot(a, b):
    return lax.dot_general(a, b, (((1,), (1,)), ((), ())), preferred_element_type=F32)


def _dot(a, b):
    return jnp.dot(a, b, preferred_element_type=F32)


def _log_sigmoid(x):
    return jnp.minimum(x, 0.0) - jnp.log1p(jnp.exp(-jnp.abs(x)))


def _split_hi_lo(x):
    hi = x.astype(BF16)
    lo = (x - hi.astype(F32)).astype(BF16)
    return hi, lo


def _front_kernel(x_ref, gmix_ref, wf_ref, wft_ref, wuq_ref, wukv_ref, g64_ref, g32_ref, tri_ref,
                  g512_ref, g128_ref, ncq_ref, nckv_ref, bfrow_ref, ca_ref, sb_ref,
                  qa_ref, ka_ref, va_ref, qb_ref, kb_ref, vb_ref, qc_ref, kc_ref, vc_ref,
                  ccol_ref, crow_ref, carry_col, carry_row, *, tiles_per_seq):
    i = pl.program_id(0)

    @pl.when(i % tiles_per_seq == 0)
    def _():
        carry_col[...] = jnp.zeros_like(carry_col)
        carry_row[...] = jnp.zeros_like(carry_row)

    x = x_ref[...]
    ms = jnp.mean(x * x, axis=-1, keepdims=True)
    xn = (x * lax.rsqrt(ms + EPS) * gmix_ref[...]).astype(BF16)

    def group_norm(h, gmat, n, gain):
        ss = _dot((h * h).astype(BF16), gmat)
        return h * lax.rsqrt(ss * (1.0 / n) + EPS) * gain

    def store_pairs(ref, val, lo=0, w=PAIR_W):
        for p in range(N_PAIRS):
            ref[p, :, lo:lo + w] = val[:, p * PAIR_W:(p + 1) * PAIR_W].astype(BF16)

    g64 = g64_ref[...]
    for base, q_ref, k_ref, v_ref, row in ((C_A, qa_ref, ka_ref, va_ref, 0), (C_B, qb_ref, kb_ref, vb_ref, 2)):
        hq = _dot(xn, wf_ref[:, base:base + 512])
        store_pairs(q_ref, group_norm(hq, g64, HEAD_DIM, g512_ref[row:row + 1, :]))
        hk = _dot(xn, wf_ref[:, base + 512:base + 1024])
        store_pairs(k_ref, group_norm(hk, g64, HEAD_DIM, g512_ref[row + 1:row + 2, :]))
        hv = _dot(xn, wf_ref[:, base + 1024:base + 1536])
        store_pairs(v_ref, hv)

    hs = _dot(xn, wf_ref[:, C_CQ:N_FRONT])
    ca = ca_ref[...]
    sb = sb_ref[...]

    hcq = hs[:, 0:Q_LORA]
    cq = (hcq * lax.rsqrt(jnp.mean(hcq * hcq, axis=-1, keepdims=True) + EPS) * ncq_ref[...]).astype(BF16)
    qc = _dot(cq, wuq_ref[...])
    qn = group_norm(qc[:, 0:512], g64, NOPE, g512_ref[4:5, :])
    store_pairs(qc_ref, qn, 0)
    qra = qc[:, 512:1024]
    qrb = qc[:, 1024:1536]
    rs = lax.rsqrt(_dot((qra * qra).astype(BF16), g32_ref[...]) * (1.0 / ROPE) + EPS)
    ga = g128_ref[0:1, :]
    gb = g128_ref[1:2, :]
    for p in range(N_PAIRS):
        sl = slice(p * PAIR_W, (p + 1) * PAIR_W)
        qr = (qra[:, sl] * ga * ca + qrb[:, sl] * gb * sb) * rs[:, sl]
        qc_ref[p, :, PAIR_W:2 * PAIR_W] = qr.astype(BF16)

    hckv = hs[:, Q_LORA:Q_LORA + KV_LORA]
    ckv = (hckv * lax.rsqrt(jnp.mean(hckv * hckv, axis=-1, keepdims=True) + EPS) * nckv_ref[...]).astype(BF16)
    kv = _dot(ckv, wukv_ref[...])
    store_pairs(kc_ref, group_norm(kv[:, 0:512], g64, NOPE, g512_ref[5:6, :]), 0)
    store_pairs(vc_ref, kv[:, 512:1024])
    kra = hs[:, C_KRA - C_CQ:C_KRA - C_CQ + LANES]
    krb = hs[:, C_KRB - C_CQ:C_KRB - C_CQ + LANES]
    rsk = lax.rsqrt(_dot((kra * kra).astype(BF16), g32_ref[0:LANES, 0:LANES]) * (1.0 / ROPE) + EPS)
    kr = ((kra * g128_ref[2:3, :] * ca + krb * g128_ref[3:4, :] * sb) * rsk).astype(BF16)
    for p in range(N_PAIRS):
        kc_ref[p, :, PAIR_W:2 * PAIR_W] = kr

    tri = tri_ref[...]
    hf_col = hs[:, C_HF - C_CQ:C_HF - C_CQ + LANES] + g128_ref[4:5, :]
    hi, lo = _split_hi_lo(_log_sigmoid(hf_col))
    ccol = _dot(tri, hi) + _dot(tri, lo) + carry_col[0:1, :]
    ccol_ref[...] = ccol
    tm = ccol.shape[0]
    carry_col[...] = jnp.broadcast_to(ccol[tm - 1:tm, :], carry_col.shape)

    hf_row = _nt_dot(wft_ref[...], xn) + bfrow_ref[:, 0:1]
    hi, lo = _split_hi_lo(_log_sigmoid(hf_row))
    crow = _nt_dot(hi, tri) + _nt_dot(lo, tri) + carry_row[:, 0:1]
    crow_ref[...] = crow
    carry_row[...] = jnp.broadcast_to(crow[:, tm - 1:tm], carry_row.shape)


def _front_call(x2d, cw, lw, seq_len):
    t = x2d.shape[0]
    tm = FRONT_TM
    const = lambda shape: pl.BlockSpec(shape, lambda i: (0,) * len(shape))
    pair_out = lambda w: pl.BlockSpec((N_PAIRS, tm, w), lambda i: (0, i, 0))
    pair_shape = lambda w: jax.ShapeDtypeStruct((N_PAIRS, t, w), BF16)
    in_specs = [
        pl.BlockSpec((tm, D_MODEL), lambda i: (i, 0)),
        const((1, D_MODEL)),
        const((D_MODEL, N_FRONT)),
        const((HF_ROWS, D_MODEL)),
        const((Q_LORA, 1536)),
        const((KV_LORA, 1024)),
        const((512, 512)),
        const((512, 512)),
        const((tm, tm)),
        const((8, 512)),
        const((8, LANES)),
        const((1, Q_LORA)),
        const((1, KV_LORA)),
        const((HF_ROWS, LANES)),
        pl.BlockSpec((tm, LANES), lambda i: (i, 0)),
        pl.BlockSpec((tm, LANES), lambda i: (i, 0)),
    ]
    out_specs = [pair_out(PAIR_W)] * 6 + [pair_out(2 * PAIR_W), pair_out(2 * PAIR_W), pair_out(PAIR_W),
                                           pl.BlockSpec((tm, LANES), lambda i: (i, 0)),
                                           pl.BlockSpec((HF_ROWS, tm), lambda i: (0, i))]
    out_shape = [pair_shape(PAIR_W)] * 6 + [pair_shape(2 * PAIR_W), pair_shape(2 * PAIR_W), pair_shape(PAIR_W),
                                            jax.ShapeDtypeStruct((t, LANES), F32),
                                            jax.ShapeDtypeStruct((HF_ROWS, t), F32)]
    return pl.pallas_call(
        functools.partial(_front_kernel, tiles_per_seq=seq_len // tm),
        grid=(t // tm,),
        in_specs=in_specs,
        out_specs=out_specs,
        out_shape=out_shape,
        scratch_shapes=[pltpu.VMEM((8, LANES), F32), pltpu.VMEM((HF_ROWS, LANES), F32)],
        compiler_params=pltpu.CompilerParams(dimension_semantics=("arbitrary",),
                                             vmem_limit_bytes=VMEM_LIMIT_BYTES),
        name="front",
    )(x2d, lw["gmix"], lw["wf"], lw["wft"], lw["wuq"], lw["wukv"], cw["g64"], cw["g32"], cw["tri"],
      lw["g512"], lw["g128"], lw["ncq"], lw["nckv"], lw["bfrow"], cw["ca"], cw["sb"])


def _dilated_kernel(q1, k1, v1, q4, k4, v4, q16, k16, v16, bias_ref, o_ref, oacc, lacc, s_scr, *, seq_len):
    lane = lax.broadcasted_iota(jnp.int32, (1, PAIR_W), 1)
    left = lane < HEAD_DIM
    views = ((1, q1, k1, v1), (4, q4, k4, v4), (16, q16, k16, v16))
    for di, (d, qr, kr, vr) in enumerate(views):
        n_blocks = seq_len // d // Q_BLOCK
        blocks = [(c, nb) for c in range(d) for nb in range(n_blocks)]

        def window(ref, c, nb):
            k_lo = max(nb - 1, 0) * Q_BLOCK
            return ref[k_lo:(nb + 1) * Q_BLOCK, c * PAIR_W:(c + 1) * PAIR_W]

        row_max = {}
        for bi, (c, nb) in enumerate(blocks):
            q = qr[nb * Q_BLOCK:(nb + 1) * Q_BLOCK, c * PAIR_W:(c + 1) * PAIR_W]
            kwin = window(kr, c, nb)
            w = kwin.shape[0]
            for j in range(2):
                qh = jnp.where(left if j == 0 else jnp.logical_not(left), q, jnp.zeros_like(q))
                s = _nt_dot(qh, kwin) + bias_ref[di, j, :, 2 * Q_BLOCK - w:]
                s_scr[j, bi * Q_BLOCK:(bi + 1) * Q_BLOCK, 0:w] = s
                row_max[bi, j] = jnp.max(s, axis=-1, keepdims=True)
        for bi, (c, nb) in enumerate(blocks):
            vwin = window(vr, c, nb)
            w = vwin.shape[0]
            outs, lses = [], []
            for j in range(2):
                m = row_max[bi, j]
                p = jnp.exp(s_scr[j, bi * Q_BLOCK:(bi + 1) * Q_BLOCK, 0:w] - m)
                den = jnp.sum(p, axis=-1, keepdims=True)
                outs.append(_dot(p.astype(BF16), vwin) / den)
                lses.append(m + jnp.log(den))
            start = nb * Q_BLOCK * d + c
            rows = pl.ds(start, Q_BLOCK) if d == 1 else pl.ds(start, Q_BLOCK, stride=d)
            oacc[di, rows, :] = jnp.where(left, outs[0], outs[1])
            lacc[di, rows, :] = jnp.where(left, lses[0], lses[1])

    l0, l1, l2 = lacc[0], lacc[1], lacc[2]
    m = jnp.maximum(jnp.maximum(l0, l1), l2)
    w0, w1, w2 = jnp.exp(l0 - m), jnp.exp(l1 - m), jnp.exp(l2 - m)
    wsum = w0 + w1 + w2
    o = oacc[0] * (w0 / wsum) + oacc[1] * (w1 / wsum) + oacc[2] * (w2 / wsum)
    o_ref[...] = o.astype(BF16)


def _dilated_call(q, k, v, bias, batch, seq_len):
    args, in_specs = [], []
    for _, d in DILATED_PAIRS:
        for a in (q, k, v):
            args.append(a.reshape(N_PAIRS, batch, seq_len // d, d * PAIR_W))
            in_specs.append(pl.BlockSpec((None, None, seq_len // d, d * PAIR_W), lambda p, b: (p, b, 0, 0)))
    args.append(bias)
    in_specs.append(pl.BlockSpec((len(DILATED_PAIRS), 2, Q_BLOCK, 2 * Q_BLOCK), lambda p, b: (0, p, 0, 0)))
    out = pl.pallas_call(
        functools.partial(_dilated_kernel, seq_len=seq_len),
        grid=(N_PAIRS, batch),
        in_specs=in_specs,
        out_specs=pl.BlockSpec((None, None, seq_len, PAIR_W), lambda p, b: (p, b, 0, 0)),
        out_shape=jax.ShapeDtypeStruct((N_PAIRS, batch, seq_len, PAIR_W), BF16),
        scratch_shapes=[pltpu.VMEM((len(DILATED_PAIRS), seq_len, PAIR_W), F32),
                        pltpu.VMEM((len(DILATED_PAIRS), seq_len, PAIR_W), F32),
                        pltpu.VMEM((2, seq_len, 2 * Q_BLOCK), F32)],
        compiler_params=pltpu.CompilerParams(dimension_semantics=("arbitrary", "arbitrary"),
                                             vmem_limit_bytes=VMEM_LIMIT_BYTES),
        name="dilated",
    )(*args)
    return out.reshape(N_PAIRS, batch * seq_len, PAIR_W)


def _flash_kernel(*refs, kw, decay, seq_len):
    if decay:
        q_ref, k_ref, v_ref, crow_ref, ccol_ref, o_ref, s_scr = refs
    else:
        q_ref, k_ref, v_ref, o_ref, s_scr = refs
    hp = pl.program_id(0)
    t = FLASH_T
    lane = lax.broadcasted_iota(jnp.int32, (1, kw), 1)
    in_h0 = (lane < HEAD_DIM) | ((lane >= PAIR_W) & (lane < PAIR_W + ROPE))
    in_h1 = ((lane >= HEAD_DIM) & (lane < PAIR_W)) | ((lane >= PAIR_W + ROPE) & (lane < PAIR_W + 2 * ROPE))
    row = lax.broadcasted_iota(jnp.int32, (t, t), 0)
    col = lax.broadcasted_iota(jnp.int32, (t, t), 1)
    causal = col <= row
    out_lane = lax.broadcasted_iota(jnp.int32, (1, PAIR_W), 1)
    for qi in range(seq_len // t):
        rows = slice(qi * t, (qi + 1) * t)
        q = q_ref[rows, :]
        outs = []
        for j, sel in enumerate((in_h0, in_h1)):
            qh = jnp.where(sel, q, jnp.zeros_like(q))
            mx = None
            for c in range(qi + 1):
                cols = slice(c * t, (c + 1) * t)
                s = _nt_dot(qh, k_ref[cols, :])
                if decay:
                    s = s - crow_ref[pl.ds(2 * hp + j, 1), cols]
                if c == qi:
                    s = jnp.where(causal, s, -jnp.inf)
                s_scr[j, :, cols] = s
                mx = s if mx is None else jnp.maximum(mx, s)
            shift = jnp.max(mx, axis=-1, keepdims=True)
            if decay:
                head = (2 * hp + j).astype(F32)
                cc = ccol_ref[rows, :]
                lane_c = lax.broadcasted_iota(jnp.int32, cc.shape, 1).astype(F32)
                cq = jnp.sum(jnp.where(lane_c == head, cc, 0.0), axis=-1, keepdims=True)
                shift = (shift + cq) - cq
            acc = None
            lsum = None
            for c in range(qi + 1):
                cols = slice(c * t, (c + 1) * t)
                p = jnp.exp(s_scr[j, :, cols] - shift)
                pv = _dot(p.astype(BF16), v_ref[cols, :])
                lsum = p if lsum is None else lsum + p
                acc = pv if acc is None else acc + pv
            outs.append(acc / jnp.sum(lsum, axis=-1, keepdims=True))
        o_ref[rows, :] = jnp.where(out_lane < HEAD_DIM, outs[0], outs[1]).astype(BF16)


def _flash_call(q, k, v, batch, seq_len, crow=None, ccol=None):
    kw = q.shape[-1]
    decay = crow is not None
    view = lambda a: a.reshape(N_PAIRS, batch, seq_len, a.shape[-1])
    args = [view(q), view(k), view(v)]
    seq_block = lambda w: pl.BlockSpec((None, None, seq_len, w), lambda p, b: (p, b, 0, 0))
    in_specs = [seq_block(kw), seq_block(kw), seq_block(PAIR_W)]
    if decay:
        args += [crow, ccol]
        in_specs += [pl.BlockSpec((HF_ROWS, seq_len), lambda p, b: (0, b)),
                     pl.BlockSpec((seq_len, LANES), lambda p, b: (b, 0))]
    out = pl.pallas_call(
        functools.partial(_flash_kernel, kw=kw, decay=decay, seq_len=seq_len),
        grid=(N_PAIRS, batch),
        in_specs=in_specs,
        out_specs=seq_block(PAIR_W),
        out_shape=jax.ShapeDtypeStruct((N_PAIRS, batch, seq_len, PAIR_W), BF16),
        scratch_shapes=[pltpu.VMEM((2, FLASH_T, seq_len), F32)],
        compiler_params=pltpu.CompilerParams(dimension_semantics=("arbitrary",) * 2,
                                             vmem_limit_bytes=VMEM_LIMIT_BYTES),
        name="flash_fox" if decay else "flash_mla",
    )(*args)
    return out.reshape(N_PAIRS, batch * seq_len, PAIR_W)


def _merge_kernel(x_ref, oa_ref, ob_ref, oc_ref, gmix_ref, wg_ref, wb_ref, wo_ref, gffn_ref,
                  wrh_ref, wrl_ref, br_ref, x1_ref, gates_ref):
    x = x_ref[...]
    ms = jnp.mean(x * x, axis=-1, keepdims=True)
    xn = (x * lax.rsqrt(ms + EPS) * gmix_ref[...]).astype(BF16)
    mixed = None
    for g, o_ref in enumerate((oa_ref, ob_ref, oc_ref)):
        o = jnp.concatenate([o_ref[p] for p in range(N_PAIRS)], axis=-1)
        gate = jax.nn.sigmoid(_dot(xn, wg_ref[:, g * D_MODEL:(g + 1) * D_MODEL]))
        term = gate * _dot(o, wb_ref[g])
        mixed = term if mixed is None else mixed + term
    x1 = x + _dot(mixed.astype(BF16), wo_ref[...])
    x1_ref[...] = x1

    ms1 = jnp.mean(x1 * x1, axis=-1, keepdims=True)
    xf = x1 * lax.rsqrt(ms1 + EPS) * gffn_ref[...]
    hi, lo = _split_hi_lo(xf)
    wrh = wrh_ref[...]
    lg = _dot(hi, wrh) + _dot(lo, wrh) + _dot(hi, wrl_ref[...]) + br_ref[...]
    lane = lax.broadcasted_iota(jnp.int32, lg.shape, 1)
    lane_f = lane.astype(F32)
    neg = -jnp.inf
    far = float(LANES)
    gl = jnp.where((lane >= N_EXPERTS) & (lane < N_EXPERTS + N_GROUPS), lg, neg)
    gmax = jnp.max(gl, axis=-1, keepdims=True)
    pg_top = 1.0 / jnp.sum(jnp.exp(gl - gmax), axis=-1, keepdims=True)
    gidx = jnp.min(jnp.where(gl == gmax, lane_f, far), axis=-1, keepdims=True) - float(N_EXPERTS)
    in_group = (lane < N_EXPERTS) & ((lane // EXPERTS_PER_GROUP).astype(F32) == gidx)
    ev = jnp.where(in_group, lg, neg)
    v1 = jnp.max(ev, axis=-1, keepdims=True)
    i1 = jnp.min(jnp.where(ev == v1, lane_f, far), axis=-1, keepdims=True)
    ev2 = jnp.where(lane_f == i1, neg, ev)
    v2 = jnp.max(ev2, axis=-1, keepdims=True)
    i2 = jnp.min(jnp.where(ev2 == v2, lane_f, far), axis=-1, keepdims=True)
    e2 = jnp.exp(v2 - v1)
    den = 1.0 + e2
    w1 = (1.0 / den) * pg_top
    w2 = (e2 / den) * pg_top
    gates_ref[...] = jnp.where(lane_f == i1, w1, jnp.where(lane_f == i2, w2, 0.0))


def _merge_call(x2d, oa, ob, oc, lw):
    t = x2d.shape[0]
    tm = MERGE_TM
    const = lambda shape: pl.BlockSpec(shape, lambda i: (0,) * len(shape))
    row_tile = lambda w: pl.BlockSpec((tm, w), lambda i: (i, 0))
    pair_in = pl.BlockSpec((N_PAIRS, tm, PAIR_W), lambda i: (0, i, 0))
    return pl.pallas_call(
        _merge_kernel,
        grid=(t // tm,),
        in_specs=[row_tile(D_MODEL), pair_in, pair_in, pair_in, const((1, D_MODEL)),
                  const((D_MODEL, 3 * D_MODEL)), const((3, 512, D_MODEL)), const((D_MODEL, D_MODEL)),
                  const((1, D_MODEL)), const((D_MODEL, LANES)), const((D_MODEL, LANES)), const((1, LANES))],
        out_specs=[row_tile(D_MODEL), row_tile(LANES)],
        out_shape=[jax.ShapeDtypeStruct((t, D_MODEL), F32), jax.ShapeDtypeStruct((t, LANES), F32)],
        compiler_params=pltpu.CompilerParams(dimension_semantics=("arbitrary",),
                                             vmem_limit_bytes=VMEM_LIMIT_BYTES),
        name="merge",
    )(x2d, oa, ob, oc, lw["gmix"], lw["wg"], lw["wb"], lw["wo"], lw["gffn"], lw["wrh"], lw["wrl"], lw["br"])


def _moe_kernel(x_ref, gates_ref, gffn_ref, wgu_ref, wd_ref, o_ref, xn_ref):
    e = pl.program_id(1)

    @pl.when(e == 0)
    def _():
        x = x_ref[...]
        ms = jnp.mean(x * x, axis=-1, keepdims=True)
        xn_ref[...] = (x * lax.rsqrt(ms + EPS) * gffn_ref[...]).astype(BF16)
        o_ref[...] = x

    h = _dot(xn_ref[...], wgu_ref[...])
    hid = jax.nn.silu(h[:, 0:D_FF]) * h[:, D_FF:2 * D_FF]
    gates = gates_ref[...]
    lane = lax.broadcasted_iota(jnp.int32, gates.shape, 1)
    gate = jnp.sum(jnp.where(lane == e, gates, 0.0), axis=-1, keepdims=True)
    o_ref[...] += gate * _dot(hid.astype(BF16), wd_ref[...])


def _moe_call(x1, gates, lw):
    t = x1.shape[0]
    tm = MOE_TM
    return pl.pallas_call(
        _moe_kernel,
        grid=(t // tm, N_EXPERTS),
        in_specs=[pl.BlockSpec((tm, D_MODEL), lambda i, e: (i, 0)),
                  pl.BlockSpec((tm, LANES), lambda i, e: (i, 0)),
                  pl.BlockSpec((1, D_MODEL), lambda i, e: (0, 0)),
                  pl.BlockSpec((None, D_MODEL, 2 * D_FF), lambda i, e: (e, 0, 0)),
                  pl.BlockSpec((None, D_FF, D_MODEL), lambda i, e: (e, 0, 0))],
        out_specs=pl.BlockSpec((tm, D_MODEL), lambda i, e: (i, 0)),
        out_shape=jax.ShapeDtypeStruct((t, D_MODEL), F32),
        scratch_shapes=[pltpu.VMEM((tm, D_MODEL), BF16)],
        compiler_params=pltpu.CompilerParams(dimension_semantics=("arbitrary", "arbitrary"),
                                             vmem_limit_bytes=VMEM_LIMIT_BYTES),
        name="moe",
    )(x1, gates, lw["gffn"], lw["wgu"], lw["wd"])


def _t5_bucket(dist):
    max_exact = N_BUCKETS // 2
    log_ratio = np.log(np.maximum(dist, max_exact) / max_exact) / np.log(MAX_DISTANCE / max_exact)
    large = np.minimum(max_exact + (log_ratio * (N_BUCKETS - max_exact)).astype(np.int32), N_BUCKETS - 1)
    return np.where(dist < max_exact, dist, large).astype(np.int32)


def _dilated_bias(rel_bias):
    tables = []
    span = 3 * Q_BLOCK
    for window, dil in DILATED_PAIRS:
        assert window // dil == Q_BLOCK
        per_dist = rel_bias[_t5_bucket(np.arange(Q_BLOCK + 1) * dil)].astype(F32).T
        diag = jnp.concatenate([jnp.full((N_HEADS, Q_BLOCK), -jnp.inf, F32), per_dist[:, ::-1],
                                jnp.full((N_HEADS, span - 2 * Q_BLOCK), -jnp.inf, F32)], axis=1)
        skew = jnp.tile(diag, (1, Q_BLOCK))[:, :Q_BLOCK * span].reshape(N_HEADS, Q_BLOCK, span)
        tables.append(skew[:, :, Q_BLOCK:])
    return jnp.stack(tables, axis=0)


def _rope_a(v1, v2):
    pad = jnp.zeros(v1.shape[:-1] + (LANES - 4 * ROPE_HALF,), v1.dtype)
    return jnp.concatenate([v1, v2, v1, v2, pad], axis=-1)


def _const_weights(positions, rel_bias):
    b, s = positions.shape
    inv_freq = ROPE_THETA ** (-jnp.arange(ROPE_HALF, dtype=F32) / ROPE_HALF)
    ang = positions.astype(F32).reshape(b * s, 1) * inv_freq
    cos, sin = jnp.cos(ang), jnp.sin(ang)
    idx = np.arange(512)
    return {
        "ca": _rope_a(cos, cos),
        "sb": _rope_a(-sin, sin),
        "g64": jnp.asarray(idx[:, None] // HEAD_DIM == idx[None, :] // HEAD_DIM, BF16),
        "g32": jnp.asarray(idx[:, None] // ROPE == idx[None, :] // ROPE, BF16),
        "tri": jnp.asarray(np.arange(FRONT_TM)[None, :] <= np.arange(FRONT_TM)[:, None], BF16),
        "bias": _dilated_bias(rel_bias),
    }


def _layer_weights(l, p):
    w_in = p["w_in"][l]
    o_hf = 3072
    o_cq = o_hf + N_HEADS
    o_ckv = o_cq + Q_LORA
    o_kr = o_ckv + KV_LORA
    o_g = o_kr + ROPE
    kr1 = w_in[:, o_kr:o_kr + ROPE_HALF]
    kr2 = w_in[:, o_kr + ROPE_HALF:o_kr + ROPE]
    hf_w = w_in[:, o_hf:o_cq]
    wf = jnp.concatenate([
        w_in[:, 0:3072], w_in[:, o_cq:o_ckv], w_in[:, o_ckv:o_kr],
        _rope_a(kr1, kr2), _rope_a(kr2, kr1),
        jnp.pad(hf_w, ((0, 0), (0, LANES - N_HEADS))),
    ], axis=1).astype(BF16)
    wft = jnp.pad(hf_w.T, ((0, HF_ROWS - N_HEADS), (0, 0))).astype(BF16)

    wuq = p["w_uq"][l].reshape(Q_LORA, N_HEADS, NOPE + ROPE)
    q_nope = wuq[:, :, :NOPE].reshape(Q_LORA, N_HEADS * NOPE)
    q1 = wuq[:, :, NOPE:NOPE + ROPE_HALF]
    q2 = wuq[:, :, NOPE + ROPE_HALF:]

    def pair_rope(v1, v2):
        x = jnp.concatenate([v1, v2], axis=-1).reshape(Q_LORA, N_PAIRS, 2 * ROPE)
        return jnp.pad(x, ((0, 0), (0, 0), (0, LANES - 2 * ROPE))).reshape(Q_LORA, N_PAIRS * LANES)

    wuq_p = jnp.concatenate([q_nope, pair_rope(q1, q2), pair_rope(q2, q1)], axis=1).astype(BF16)
    wukv = p["w_ukv"][l].reshape(KV_LORA, N_HEADS, NOPE + HEAD_DIM)
    wukv_p = jnp.concatenate([wukv[:, :, :NOPE].reshape(KV_LORA, 512),
                              wukv[:, :, NOPE:].reshape(KV_LORA, 512)], axis=1).astype(BF16)

    tile8 = lambda g: jnp.tile(g, N_HEADS)
    sc_ab = 1.0 / math.sqrt(HEAD_DIM)
    sc_c = 1.0 / math.sqrt(NOPE + ROPE)
    gq_c, gk_c = p["gq_c"][l], p["gk_c"][l]
    zeros512 = jnp.zeros((512,), F32)
    g512 = jnp.stack([tile8(p["gq_a"][l]) * sc_ab, tile8(p["gk_a"][l]),
                      tile8(p["gq_b"][l]) * sc_ab, tile8(p["gk_b"][l]),
                      tile8(gq_c[:NOPE]) * sc_c, tile8(gk_c[:NOPE]), zeros512, zeros512])
    gq1, gq2 = gq_c[NOPE:NOPE + ROPE_HALF] * sc_c, gq_c[NOPE + ROPE_HALF:] * sc_c
    gk1, gk2 = gk_c[NOPE:NOPE + ROPE_HALF], gk_c[NOPE + ROPE_HALF:]
    zeros128 = jnp.zeros((LANES,), F32)
    bf = p["b_forget"][l].astype(F32)
    g128 = jnp.stack([_rope_a(gq1, gq2), _rope_a(gq2, gq1), _rope_a(gk1, gk2), _rope_a(gk2, gk1),
                      jnp.pad(bf, (0, LANES - N_HEADS)), zeros128, zeros128, zeros128])
    bfrow = jnp.broadcast_to(jnp.pad(bf, (0, HF_ROWS - N_HEADS))[:, None], (HF_ROWS, LANES))

    wr = jnp.concatenate([p["w_router_expert"][l], p["w_router_group"][l]], axis=1)
    wr = jnp.pad(wr, ((0, 0), (0, LANES - N_EXPERTS - N_GROUPS))).astype(F32)
    wrh = wr.astype(BF16)
    wrl = (wr - wrh.astype(F32)).astype(BF16)
    br = jnp.pad(jnp.concatenate([p["b_router_expert"][l], p["b_router_group"][l]]).astype(F32),
                 (0, LANES - N_EXPERTS - N_GROUPS))[None, :]
    return {
        "gmix": p["norm_mix"][l][None, :], "wf": wf, "wft": wft, "wuq": wuq_p, "wukv": wukv_p,
        "g512": g512, "g128": g128, "ncq": p["norm_cq"][l][None, :], "nckv": p["norm_ckv"][l][None, :],
        "bfrow": bfrow,
        "wg": w_in[:, o_g:].astype(BF16), "wb": p["w_branch"][l].astype(BF16), "wo": p["w_out"][l].astype(BF16),
        "gffn": p["norm_ffn"][l][None, :], "wrh": wrh, "wrl": wrl, "br": br,
        "wgu": jnp.concatenate([p["w_expert_gate"][l], p["w_expert_up"][l]], axis=-1).astype(BF16),
        "wd": p["w_expert_down"][l].astype(BF16),
    }


def kernel(x, positions, rel_bias, norm_mix, w_in, b_forget, gq_a, gk_a, gq_b, gk_b, gq_c, gk_c, norm_cq, norm_ckv, w_uq, w_ukv, w_branch, w_out, norm_ffn, w_router_group, b_router_group, w_router_expert, b_router_expert, w_expert_gate, w_expert_up, w_expert_down):
    batch, seq_len, d_model = x.shape
    assert d_model == D_MODEL and seq_len % (Q_BLOCK * DILATED_PAIRS[-1][1]) == 0
    assert seq_len == DILATED_PAIRS[-1][0], "the widest dilated window is assumed to span the sequence"
    p = dict(norm_mix=norm_mix, w_in=w_in, b_forget=b_forget, gq_a=gq_a, gk_a=gk_a, gq_b=gq_b, gk_b=gk_b,
             gq_c=gq_c, gk_c=gk_c, norm_cq=norm_cq, norm_ckv=norm_ckv, w_uq=w_uq, w_ukv=w_ukv,
             w_branch=w_branch, w_out=w_out, norm_ffn=norm_ffn, w_router_group=w_router_group,
             b_router_group=b_router_group, w_router_expert=w_router_expert, b_router_expert=b_router_expert,
             w_expert_gate=w_expert_gate, w_expert_up=w_expert_up, w_expert_down=w_expert_down)
    cw = _const_weights(positions, rel_bias)
    xs = x.reshape(batch * seq_len, d_model)
    for l in range(norm_mix.shape[0]):
        lw = _layer_weights(l, p)
        qa, ka, va, qb, kb, vb, qc, kc, vc, ccol, crow = _front_call(xs, cw, lw, seq_len)
        oa = _dilated_call(qa, ka, va, cw["bias"], batch, seq_len)
        ob = _flash_call(qb, kb, vb, batch, seq_len, crow, ccol)
        oc = _flash_call(qc, kc, vc, batch, seq_len)
        x1, gates = _merge_call(xs, oa, ob, oc, lw)
        xs = _moe_call(x1, gates, lw)
    return xs.reshape(batch, seq_len, d_model)
```

```python
import functools
import math

import numpy as np
import jax
import jax.numpy as jnp
from jax import lax
from jax.experimental import pallas as pl
from jax.experimental.pallas import tpu as pltpu

F32 = jnp.float32
BF16 = jnp.bfloat16

D_MODEL = 1024
N_HEADS = 8
HEAD_DIM = 64
N_PAIRS = N_HEADS // 2
PAIR_W = 2 * HEAD_DIM
DILATED_PAIRS = ((128, 1), (512, 4), (2048, 16))
Q_LORA = 256
KV_LORA = 128
NOPE = 64
ROPE = 32
ROPE_HALF = ROPE // 2
ROPE_THETA = 10000.0
N_BUCKETS = 32
MAX_DISTANCE = 2048
Q_BLOCK = 128
N_GROUPS = 4
EXPERTS_PER_GROUP = 4
N_EXPERTS = 16
D_FF = 256
EPS = 1e-6
LANES = 128

VMEM_LIMIT_BYTES = 56 * 1024 * 1024

FRONT_TM = 512
MERGE_TM = 512
MOE_TM = 1024
FLASH_T = 256

C_A = 0
C_B = 1536
C_CQ = 3072
C_CKV = C_CQ + Q_LORA
C_KRA = C_CKV + KV_LORA
C_KRB = C_KRA + LANES
C_HF = C_KRB + LANES
N_FRONT = C_HF + LANES
HF_ROWS = 16


def _nt_dot(a, b):
    return lax.dot_general(a, b, (((1,), (1,)), ((), ())), preferred_element_type=F32)


def _dot(a, b):
    return jnp.dot(a, b, preferred_element_type=F32)


def _log_sigmoid(x):
    return jnp.minimum(x, 0.0) - jnp.log1p(jnp.exp(-jnp.abs(x)))


def _split_hi_lo(x):
    hi = x.astype(BF16)
    lo = (x - hi.astype(F32)).astype(BF16)
    return hi, lo


def _front_kernel(x_ref, gmix_ref, wf_ref, wft_ref, wuq_ref, wukv_ref, g64_ref, g32_ref, tri_ref,
                  g512_ref, g128_ref, ncq_ref, nckv_ref, bfrow_ref, ca_ref, sb_ref,
                  qa_ref, ka_ref, va_ref, qb_ref, kb_ref, vb_ref, qc_ref, kc_ref, vc_ref,
                  ccol_ref, crow_ref, carry_col, carry_row, *, tiles_per_seq):
    i = pl.program_id(0)

    @pl.when(i % tiles_per_seq == 0)
    def _():
        carry_col[...] = jnp.zeros_like(carry_col)
        carry_row[...] = jnp.zeros_like(carry_row)

    x = x_ref[...]
    ms = jnp.mean(x * x, axis=-1, keepdims=True)
    xn = (x * lax.rsqrt(ms + EPS) * gmix_ref[...]).astype(BF16)

    def group_norm(h, gmat, n, gain):
        ss = _dot((h * h).astype(BF16), gmat)
        return h * lax.rsqrt(ss * (1.0 / n) + EPS) * gain

    def store_pairs(ref, val, lo=0, w=PAIR_W):
        for p in range(N_PAIRS):
            ref[p, :, lo:lo + w] = val[:, p * PAIR_W:(p + 1) * PAIR_W].astype(BF16)

    g64 = g64_ref[...]
    for base, q_ref, k_ref, v_ref, row in ((C_A, qa_ref, ka_ref, va_ref, 0), (C_B, qb_ref, kb_ref, vb_ref, 2)):
        hq = _dot(xn, wf_ref[:, base:base + 512])
        store_pairs(q_ref, group_norm(hq, g64, HEAD_DIM, g512_ref[row:row + 1, :]))
        hk = _dot(xn, wf_ref[:, base + 512:base + 1024])
        store_pairs(k_ref, group_norm(hk, g64, HEAD_DIM, g512_ref[row + 1:row + 2, :]))
        hv = _dot(xn, wf_ref[:, base + 1024:base + 1536])
        store_pairs(v_ref, hv)

    hs = _dot(xn, wf_ref[:, C_CQ:N_FRONT])
    ca = ca_ref[...]
    sb = sb_ref[...]

    hcq = hs[:, 0:Q_LORA]
    cq = (hcq * lax.rsqrt(jnp.mean(hcq * hcq, axis=-1, keepdims=True) + EPS) * ncq_ref[...]).astype(BF16)
    qc = _dot(cq, wuq_ref[...])
    qn = group_norm(qc[:, 0:512], g64, NOPE, g512_ref[4:5, :])
    store_pairs(qc_ref, qn, 0)
    qra = qc[:, 512:1024]
    qrb = qc[:, 1024:1536]
    rs = lax.rsqrt(_dot((qra * qra).astype(BF16), g32_ref[...]) * (1.0 / ROPE) + EPS)
    ga = g128_ref[0:1, :]
    gb = g128_ref[1:2, :]
    for p in range(N_PAIRS):
        sl = slice(p * PAIR_W, (p + 1) * PAIR_W)
        qr = (qra[:, sl] * ga * ca + qrb[:, sl] * gb * sb) * rs[:, sl]
        qc_ref[p, :, PAIR_W:2 * PAIR_W] = qr.astype(BF16)

    hckv = hs[:, Q_LORA:Q_LORA + KV_LORA]
    ckv = (hckv * lax.rsqrt(jnp.mean(hckv * hckv, axis=-1, keepdims=True) + EPS) * nckv_ref[...]).astype(BF16)
    kv = _dot(ckv, wukv_ref[...])
    store_pairs(kc_ref, group_norm(kv[:, 0:512], g64, NOPE, g512_ref[5:6, :]), 0)
    store_pairs(vc_ref, kv[:, 512:1024])
    kra = hs[:, C_KRA - C_CQ:C_KRA - C_CQ + LANES]
    krb = hs[:, C_KRB - C_CQ:C_KRB - C_CQ + LANES]
    rsk = lax.rsqrt(_dot((kra * kra).astype(BF16), g32_ref[0:LANES, 0:LANES]) * (1.0 / ROPE) + EPS)
    kr = ((kra * g128_ref[2:3, :] * ca + krb * g128_ref[3:4, :] * sb) * rsk).astype(BF16)
    for p in range(N_PAIRS):
        kc_ref[p, :, PAIR_W:2 * PAIR_W] = kr

    tri = tri_ref[...]
    hf_col = hs[:, C_HF - C_CQ:C_HF - C_CQ + LANES] + g128_ref[4:5, :]
    hi, lo = _split_hi_lo(_log_sigmoid(hf_col))
    ccol = _dot(tri, hi) + _dot(tri, lo) + carry_col[0:1, :]
    ccol_ref[...] = ccol
    tm = ccol.shape[0]
    carry_col[...] = jnp.broadcast_to(ccol[tm - 1:tm, :], carry_col.shape)

    hf_row = _nt_dot(wft_ref[...], xn) + bfrow_ref[:, 0:1]
    hi, lo = _split_hi_lo(_log_sigmoid(hf_row))
    crow = _nt_dot(hi, tri) + _nt_dot(lo, tri) + carry_row[:, 0:1]
    crow_ref[...] = crow
    carry_row[...] = jnp.broadcast_to(crow[:, tm - 1:tm], carry_row.shape)


def _front_call(x2d, cw, lw, seq_len):
    t = x2d.shape[0]
    tm = FRONT_TM
    const = lambda shape: pl.BlockSpec(shape, lambda i: (0,) * len(shape))
    pair_out = lambda w: pl.BlockSpec((N_PAIRS, tm, w), lambda i: (0, i, 0))
    pair_shape = lambda w: jax.ShapeDtypeStruct((N_PAIRS, t, w), BF16)
    in_specs = [
        pl.BlockSpec((tm, D_MODEL), lambda i: (i, 0)),
        const((1, D_MODEL)),
        const((D_MODEL, N_FRONT)),
        const((HF_ROWS, D_MODEL)),
        const((Q_LORA, 1536)),
        const((KV_LORA, 1024)),
        const((512, 512)),
        const((512, 512)),
        const((tm, tm)),
        const((8, 512)),
        const((8, LANES)),
        const((1, Q_LORA)),
        const((1, KV_LORA)),
        const((HF_ROWS, LANES)),
        pl.BlockSpec((tm, LANES), lambda i: (i, 0)),
        pl.BlockSpec((tm, LANES), lambda i: (i, 0)),
    ]
    out_specs = [pair_out(PAIR_W)] * 6 + [pair_out(2 * PAIR_W), pair_out(2 * PAIR_W), pair_out(PAIR_W),
                                           pl.BlockSpec((tm, LANES), lambda i: (i, 0)),
                                           pl.BlockSpec((HF_ROWS, tm), lambda i: (0, i))]
    out_shape = [pair_shape(PAIR_W)] * 6 + [pair_shape(2 * PAIR_W), pair_shape(2 * PAIR_W), pair_shape(PAIR_W),
                                            jax.ShapeDtypeStruct((t, LANES), F32),
                                            jax.ShapeDtypeStruct((HF_ROWS, t), F32)]
    return pl.pallas_call(
        functools.partial(_front_kernel, tiles_per_seq=seq_len // tm),
        grid=(t // tm,),
        in_specs=in_specs,
        out_specs=out_specs,
        out_shape=out_shape,
        scratch_shapes=[pltpu.VMEM((8, LANES), F32), pltpu.VMEM((HF_ROWS, LANES), F32)],
        compiler_params=pltpu.CompilerParams(dimension_semantics=("arbitrary",),
                                             vmem_limit_bytes=VMEM_LIMIT_BYTES),
        name="front",
    )(x2d, lw["gmix"], lw["wf"], lw["wft"], lw["wuq"], lw["wukv"], cw["g64"], cw["g32"], cw["tri"],
      lw["g512"], lw["g128"], lw["ncq"], lw["nckv"], lw["bfrow"], cw["ca"], cw["sb"])


def _dilated_kernel(q1, k1, v1, q4, k4, v4, q16, k16, v16, bias_ref, o_ref, oacc, lacc, s_scr, *, seq_len):
    lane = lax.broadcasted_iota(jnp.int32, (1, PAIR_W), 1)
    left = lane < HEAD_DIM
    views = ((1, q1, k1, v1), (4, q4, k4, v4), (16, q16, k16, v16))
    for di, (d, qr, kr, vr) in enumerate(views):
        n_blocks = seq_len // d // Q_BLOCK
        blocks = [(c, nb) for c in range(d) for nb in range(n_blocks)]

        def window(ref, c, nb):
            k_lo = max(nb - 1, 0) * Q_BLOCK
            return ref[k_lo:(nb + 1) * Q_BLOCK, c * PAIR_W:(c + 1) * PAIR_W]

        row_max = {}
        for bi, (c, nb) in enumerate(blocks):
            q = qr[nb * Q_BLOCK:(nb + 1) * Q_BLOCK, c * PAIR_W:(c + 1) * PAIR_W]
            kwin = window(kr, c, nb)
            w = kwin.shape[0]
            for j in range(2):
                qh = jnp.where(left if j == 0 else jnp.logical_not(left), q, jnp.zeros_like(q))
                s = _nt_dot(qh, kwin) + bias_ref[di, j, :, 2 * Q_BLOCK - w:]
                s_scr[j, bi * Q_BLOCK:(bi + 1) * Q_BLOCK, 0:w] = s
                row_max[bi, j] = jnp.max(s, axis=-1, keepdims=True)
        for bi, (c, nb) in enumerate(blocks):
            vwin = window(vr, c, nb)
            w = vwin.shape[0]
            outs, lses = [], []
            for j in range(2):
                m = row_max[bi, j]
                p = jnp.exp(s_scr[j, bi * Q_BLOCK:(bi + 1) * Q_BLOCK, 0:w] - m)
                den = jnp.sum(p, axis=-1, keepdims=True)
                outs.append(_dot(p.astype(BF16), vwin) / den)
                lses.append(m + jnp.log(den))
            start = nb * Q_BLOCK * d + c
            rows = pl.ds(start, Q_BLOCK) if d == 1 else pl.ds(start, Q_BLOCK, stride=d)
            oacc[di, rows, :] = jnp.where(left, outs[0], outs[1])
            lacc[di, rows, :] = jnp.where(left, lses[0], lses[1])

    l0, l1, l2 = lacc[0], lacc[1], lacc[2]
    m = jnp.maximum(jnp.maximum(l0, l1), l2)
    w0, w1, w2 = jnp.exp(l0 - m), jnp.exp(l1 - m), jnp.exp(l2 - m)
    wsum = w0 + w1 + w2
    o = oacc[0] * (w0 / wsum) + oacc[1] * (w1 / wsum) + oacc[2] * (w2 / wsum)
    o_ref[...] = o.astype(BF16)


def _dilated_call(q, k, v, bias, batch, seq_len):
    args, in_specs = [], []
    for _, d in DILATED_PAIRS:
        for a in (q, k, v):
            args.append(a.reshape(N_PAIRS, batch, seq_len // d, d * PAIR_W))
            in_specs.append(pl.BlockSpec((None, None, seq_len // d, d * PAIR_W), lambda p, b: (p, b, 0, 0)))
    args.append(bias)
    in_specs.append(pl.BlockSpec((len(DILATED_PAIRS), 2, Q_BLOCK, 2 * Q_BLOCK), lambda p, b: (0, p, 0, 0)))
    out = pl.pallas_call(
        functools.partial(_dilated_kernel, seq_len=seq_len),
        grid=(N_PAIRS, batch),
        in_specs=in_specs,
        out_specs=pl.BlockSpec((None, None, seq_len, PAIR_W), lambda p, b: (p, b, 0, 0)),
        out_shape=jax.ShapeDtypeStruct((N_PAIRS, batch, seq_len, PAIR_W), BF16),
        scratch_shapes=[pltpu.VMEM((len(DILATED_PAIRS), seq_len, PAIR_W), F32),
                        pltpu.VMEM((len(DILATED_PAIRS), seq_len, PAIR_W), F32),
                        pltpu.VMEM((2, seq_len, 2 * Q_BLOCK), F32)],
        compiler_params=pltpu.CompilerParams(dimension_semantics=("arbitrary", "arbitrary"),
                                             vmem_limit_bytes=VMEM_LIMIT_BYTES),
        name="dilated",
    )(*args)
    return out.reshape(N_PAIRS, batch * seq_len, PAIR_W)


def _flash_kernel(*refs, kw, decay, seq_len):
    if decay:
        q_ref, k_ref, v_ref, crow_ref, ccol_ref, o_ref, s_scr = refs
    else:
        q_ref, k_ref, v_ref, o_ref, s_scr = refs
    hp = pl.program_id(0)
    t = FLASH_T
    lane = lax.broadcasted_iota(jnp.int32, (1, kw), 1)
    in_h0 = (lane < HEAD_DIM) | ((lane >= PAIR_W) & (lane < PAIR_W + ROPE))
    in_h1 = ((lane >= HEAD_DIM) & (lane < PAIR_W)) | ((lane >= PAIR_W + ROPE) & (lane < PAIR_W + 2 * ROPE))
    row = lax.broadcasted_iota(jnp.int32, (t, t), 0)
    col = lax.broadcasted_iota(jnp.int32, (t, t), 1)
    causal = col <= row
    out_lane = lax.broadcasted_iota(jnp.int32, (1, PAIR_W), 1)
    for qi in range(seq_len // t):
        rows = slice(qi * t, (qi + 1) * t)
        q = q_ref[rows, :]
        outs = []
        for j, sel in enumerate((in_h0, in_h1)):
            qh = jnp.where(sel, q, jnp.zeros_like(q))
            mx = None
            for c in range(qi + 1):
                cols = slice(c * t, (c + 1) * t)
                s = _nt_dot(qh, k_ref[cols, :])
                if decay:
                    s = s - crow_ref[pl.ds(2 * hp + j, 1), cols]
                if c == qi:
                    s = jnp.where(causal, s, -jnp.inf)
                s_scr[j, :, cols] = s
                mx = s if mx is None else jnp.maximum(mx, s)
            shift = jnp.max(mx, axis=-1, keepdims=True)
            if decay:
                head = (2 * hp + j).astype(F32)
                cc = ccol_ref[rows, :]
                lane_c = lax.broadcasted_iota(jnp.int32, cc.shape, 1).astype(F32)
                cq = jnp.sum(jnp.where(lane_c == head, cc, 0.0), axis=-1, keepdims=True)
                shift = (shift + cq) - cq
            acc = None
            lsum = None
            for c in range(qi + 1):
                cols = slice(c * t, (c + 1) * t)
                p = jnp.exp(s_scr[j, :, cols] - shift)
                pv = _dot(p.astype(BF16), v_ref[cols, :])
                lsum = p if lsum is None else lsum + p
                acc = pv if acc is None else acc + pv
            outs.append(acc / jnp.sum(lsum, axis=-1, keepdims=True))
        o_ref[rows, :] = jnp.where(out_lane < HEAD_DIM, outs[0], outs[1]).astype(BF16)


def _flash_call(q, k, v, batch, seq_len, crow=None, ccol=None):
    kw = q.shape[-1]
    decay = crow is not None
    view = lambda a: a.reshape(N_PAIRS, batch, seq_len, a.shape[-1])
    args = [view(q), view(k), view(v)]
    seq_block = lambda w: pl.BlockSpec((None, None, seq_len, w), lambda p, b: (p, b, 0, 0))
    in_specs = [seq_block(kw), seq_block(kw), seq_block(PAIR_W)]
    if decay:
        args += [crow, ccol]
        in_specs += [pl.BlockSpec((HF_ROWS, seq_len), lambda p, b: (0, b)),
                     pl.BlockSpec((seq_len, LANES), lambda p, b: (b, 0))]
    out = pl.pallas_call(
        functools.partial(_flash_kernel, kw=kw, decay=decay, seq_len=seq_len),
        grid=(N_PAIRS, batch),
        in_specs=in_specs,
        out_specs=seq_block(PAIR_W),
        out_shape=jax.ShapeDtypeStruct((N_PAIRS, batch, seq_len, PAIR_W), BF16),
        scratch_shapes=[pltpu.VMEM((2, FLASH_T, seq_len), F32)],
        compiler_params=pltpu.CompilerParams(dimension_semantics=("arbitrary",) * 2,
                                             vmem_limit_bytes=VMEM_LIMIT_BYTES),
        name="flash_fox" if decay else "flash_mla",
    )(*args)
    return out.reshape(N_PAIRS, batch * seq_len, PAIR_W)


def _merge_kernel(x_ref, oa_ref, ob_ref, oc_ref, gmix_ref, wg_ref, wb_ref, wo_ref, gffn_ref,
                  wr_ref, br_ref, x1_ref, gates_ref):
    x = x_ref[...]
    ms = jnp.mean(x * x, axis=-1, keepdims=True)
    xn = (x * lax.rsqrt(ms + EPS) * gmix_ref[...]).astype(BF16)
    mixed = None
    for g, o_ref in enumerate((oa_ref, ob_ref, oc_ref)):
        o = jnp.concatenate([o_ref[p] for p in range(N_PAIRS)], axis=-1)
        gate = jax.nn.sigmoid(_dot(xn, wg_ref[:, g * D_MODEL:(g + 1) * D_MODEL]))
        term = gate * _dot(o, wb_ref[g])
        mixed = term if mixed is None else mixed + term
    x1 = x + _dot(mixed.astype(BF16), wo_ref[...])
    x1_ref[...] = x1

    ms1 = jnp.mean(x1 * x1, axis=-1, keepdims=True)
    xf = x1 * lax.rsqrt(ms1 + EPS) * gffn_ref[...]
    hi, lo = _split_hi_lo(xf)
    tm = xf.shape[0]
    parts = _dot(jnp.concatenate([hi, lo], axis=0), wr_ref[...])
    lg = (parts[0:tm, 0:LANES] + parts[0:tm, LANES:2 * LANES]
          + parts[tm:2 * tm, 0:LANES] + parts[tm:2 * tm, LANES:2 * LANES]) + br_ref[...]
    lane = lax.broadcasted_iota(jnp.int32, lg.shape, 1)
    lane_f = lane.astype(F32)
    neg = -jnp.inf
    far = float(LANES)
    gl = jnp.where((lane >= N_EXPERTS) & (lane < N_EXPERTS + N_GROUPS), lg, neg)
    gmax = jnp.max(gl, axis=-1, keepdims=True)
    pg_top = 1.0 / jnp.sum(jnp.exp(gl - gmax), axis=-1, keepdims=True)
    gidx = jnp.min(jnp.where(gl == gmax, lane_f, far), axis=-1, keepdims=True) - float(N_EXPERTS)
    in_group = (lane < N_EXPERTS) & ((lane // EXPERTS_PER_GROUP).astype(F32) == gidx)
    ev = jnp.where(in_group, lg, neg)
    v1 = jnp.max(ev, axis=-1, keepdims=True)
    i1 = jnp.min(jnp.where(ev == v1, lane_f, far), axis=-1, keepdims=True)
    ev2 = jnp.where(lane_f == i1, neg, ev)
    v2 = jnp.max(ev2, axis=-1, keepdims=True)
    i2 = jnp.min(jnp.where(ev2 == v2, lane_f, far), axis=-1, keepdims=True)
    e2 = jnp.exp(v2 - v1)
    den = 1.0 + e2
    w1 = (1.0 / den) * pg_top
    w2 = (e2 / den) * pg_top
    gates_ref[...] = jnp.where(lane_f == i1, w1, jnp.where(lane_f == i2, w2, 0.0))


def _merge_call(x2d, oa, ob, oc, lw):
    t = x2d.shape[0]
    tm = MERGE_TM
    const = lambda shape: pl.BlockSpec(shape, lambda i: (0,) * len(shape))
    row_tile = lambda w: pl.BlockSpec((tm, w), lambda i: (i, 0))
    pair_in = pl.BlockSpec((N_PAIRS, tm, PAIR_W), lambda i: (0, i, 0))
    return pl.pallas_call(
        _merge_kernel,
        grid=(t // tm,),
        in_specs=[row_tile(D_MODEL), pair_in, pair_in, pair_in, const((1, D_MODEL)),
                  const((D_MODEL, 3 * D_MODEL)), const((3, 512, D_MODEL)), const((D_MODEL, D_MODEL)),
                  const((1, D_MODEL)), const((D_MODEL, 2 * LANES)), const((1, LANES))],
        out_specs=[row_tile(D_MODEL), row_tile(LANES)],
        out_shape=[jax.ShapeDtypeStruct((t, D_MODEL), F32), jax.ShapeDtypeStruct((t, LANES), F32)],
        compiler_params=pltpu.CompilerParams(dimension_semantics=("arbitrary",),
                                             vmem_limit_bytes=VMEM_LIMIT_BYTES),
        name="merge",
    )(x2d, oa, ob, oc, lw["gmix"], lw["wg"], lw["wb"], lw["wo"], lw["gffn"], lw["wr"], lw["br"])


def _moe_kernel(x_ref, gates_ref, gffn_ref, wgu_ref, wd_ref, o_ref, xn_ref):
    g = pl.program_id(1)

    @pl.when(g == 0)
    def _():
        x = x_ref[...]
        ms = jnp.mean(x * x, axis=-1, keepdims=True)
        xn_ref[...] = (x * lax.rsqrt(ms + EPS) * gffn_ref[...]).astype(BF16)
        o_ref[...] = x

    xn = xn_ref[...]
    gates = gates_ref[...]
    lane = lax.broadcasted_iota(jnp.int32, gates.shape, 1)
    ff_all = EXPERTS_PER_GROUP * D_FF
    parts = []
    for e in range(EXPERTS_PER_GROUP):
        h_gate = _dot(xn, wgu_ref[:, e * D_FF:(e + 1) * D_FF])
        h_up = _dot(xn, wgu_ref[:, ff_all + e * D_FF:ff_all + (e + 1) * D_FF])
        gate = jnp.sum(jnp.where(lane == g * EXPERTS_PER_GROUP + e, gates, 0.0), axis=-1, keepdims=True)
        parts.append((jax.nn.silu(h_gate) * h_up * gate).astype(BF16))
    o_ref[...] += _dot(jnp.concatenate(parts, axis=-1), wd_ref[...])


def _moe_call(x1, gates, lw):
    t = x1.shape[0]
    tm = MOE_TM
    ff_all = EXPERTS_PER_GROUP * D_FF
    return pl.pallas_call(
        _moe_kernel,
        grid=(t // tm, N_GROUPS),
        in_specs=[pl.BlockSpec((tm, D_MODEL), lambda i, g: (i, 0)),
                  pl.BlockSpec((tm, LANES), lambda i, g: (i, 0)),
                  pl.BlockSpec((1, D_MODEL), lambda i, g: (0, 0)),
                  pl.BlockSpec((None, D_MODEL, 2 * ff_all), lambda i, g: (g, 0, 0)),
                  pl.BlockSpec((None, ff_all, D_MODEL), lambda i, g: (g, 0, 0))],
        out_specs=pl.BlockSpec((tm, D_MODEL), lambda i, g: (i, 0)),
        out_shape=jax.ShapeDtypeStruct((t, D_MODEL), F32),
        scratch_shapes=[pltpu.VMEM((tm, D_MODEL), BF16)],
        compiler_params=pltpu.CompilerParams(dimension_semantics=("arbitrary", "arbitrary"),
                                             vmem_limit_bytes=VMEM_LIMIT_BYTES),
        name="moe",
    )(x1, gates, lw["gffn"], lw["wgu"], lw["wd"])


def _t5_bucket(dist):
    max_exact = N_BUCKETS // 2
    log_ratio = np.log(np.maximum(dist, max_exact) / max_exact) / np.log(MAX_DISTANCE / max_exact)
    large = np.minimum(max_exact + (log_ratio * (N_BUCKETS - max_exact)).astype(np.int32), N_BUCKETS - 1)
    return np.where(dist < max_exact, dist, large).astype(np.int32)


def _dilated_bias(rel_bias):
    tables = []
    span = 3 * Q_BLOCK
    for window, dil in DILATED_PAIRS:
        assert window // dil == Q_BLOCK
        per_dist = rel_bias[_t5_bucket(np.arange(Q_BLOCK + 1) * dil)].astype(F32).T
        diag = jnp.concatenate([jnp.full((N_HEADS, Q_BLOCK), -jnp.inf, F32), per_dist[:, ::-1],
                                jnp.full((N_HEADS, span - 2 * Q_BLOCK), -jnp.inf, F32)], axis=1)
        skew = jnp.tile(diag, (1, Q_BLOCK))[:, :Q_BLOCK * span].reshape(N_HEADS, Q_BLOCK, span)
        tables.append(skew[:, :, Q_BLOCK:])
    return jnp.stack(tables, axis=0)


def _rope_a(v1, v2):
    pad = jnp.zeros(v1.shape[:-1] + (LANES - 4 * ROPE_HALF,), v1.dtype)
    return jnp.concatenate([v1, v2, v1, v2, pad], axis=-1)


def _const_weights(positions, rel_bias):
    b, s = positions.shape
    inv_freq = ROPE_THETA ** (-jnp.arange(ROPE_HALF, dtype=F32) / ROPE_HALF)
    ang = positions.astype(F32).reshape(b * s, 1) * inv_freq
    cos, sin = jnp.cos(ang), jnp.sin(ang)
    idx = np.arange(512)
    return {
        "ca": _rope_a(cos, cos),
        "sb": _rope_a(-sin, sin),
        "g64": jnp.asarray(idx[:, None] // HEAD_DIM == idx[None, :] // HEAD_DIM, BF16),
        "g32": jnp.asarray(idx[:, None] // ROPE == idx[None, :] // ROPE, BF16),
        "tri": jnp.asarray(np.arange(FRONT_TM)[None, :] <= np.arange(FRONT_TM)[:, None], BF16),
        "bias": _dilated_bias(rel_bias),
    }


def _group_cols(w):
    w = w.reshape(N_GROUPS, EXPERTS_PER_GROUP, D_MODEL, D_FF)
    return w.transpose(0, 2, 1, 3).reshape(N_GROUPS, D_MODEL, EXPERTS_PER_GROUP * D_FF)


def _layer_weights(l, p):
    w_in = p["w_in"][l]
    o_hf = 3072
    o_cq = o_hf + N_HEADS
    o_ckv = o_cq + Q_LORA
    o_kr = o_ckv + KV_LORA
    o_g = o_kr + ROPE
    kr1 = w_in[:, o_kr:o_kr + ROPE_HALF]
    kr2 = w_in[:, o_kr + ROPE_HALF:o_kr + ROPE]
    hf_w = w_in[:, o_hf:o_cq]
    wf = jnp.concatenate([
        w_in[:, 0:3072], w_in[:, o_cq:o_ckv], w_in[:, o_ckv:o_kr],
        _rope_a(kr1, kr2), _rope_a(kr2, kr1),
        jnp.pad(hf_w, ((0, 0), (0, LANES - N_HEADS))),
    ], axis=1).astype(BF16)
    wft = jnp.pad(hf_w.T, ((0, HF_ROWS - N_HEADS), (0, 0))).astype(BF16)

    wuq = p["w_uq"][l].reshape(Q_LORA, N_HEADS, NOPE + ROPE)
    q_nope = wuq[:, :, :NOPE].reshape(Q_LORA, N_HEADS * NOPE)
    q1 = wuq[:, :, NOPE:NOPE + ROPE_HALF]
    q2 = wuq[:, :, NOPE + ROPE_HALF:]

    def pair_rope(v1, v2):
        x = jnp.concatenate([v1, v2], axis=-1).reshape(Q_LORA, N_PAIRS, 2 * ROPE)
        return jnp.pad(x, ((0, 0), (0, 0), (0, LANES - 2 * ROPE))).reshape(Q_LORA, N_PAIRS * LANES)

    wuq_p = jnp.concatenate([q_nope, pair_rope(q1, q2), pair_rope(q2, q1)], axis=1).astype(BF16)
    wukv = p["w_ukv"][l].reshape(KV_LORA, N_HEADS, NOPE + HEAD_DIM)
    wukv_p = jnp.concatenate([wukv[:, :, :NOPE].reshape(KV_LORA, 512),
                              wukv[:, :, NOPE:].reshape(KV_LORA, 512)], axis=1).astype(BF16)

    tile8 = lambda g: jnp.tile(g, N_HEADS)
    sc_ab = 1.0 / math.sqrt(HEAD_DIM)
    sc_c = 1.0 / math.sqrt(NOPE + ROPE)
    gq_c, gk_c = p["gq_c"][l], p["gk_c"][l]
    zeros512 = jnp.zeros((512,), F32)
    g512 = jnp.stack([tile8(p["gq_a"][l]) * sc_ab, tile8(p["gk_a"][l]),
                      tile8(p["gq_b"][l]) * sc_ab, tile8(p["gk_b"][l]),
                      tile8(gq_c[:NOPE]) * sc_c, tile8(gk_c[:NOPE]), zeros512, zeros512])
    gq1, gq2 = gq_c[NOPE:NOPE + ROPE_HALF] * sc_c, gq_c[NOPE + ROPE_HALF:] * sc_c
    gk1, gk2 = gk_c[NOPE:NOPE + ROPE_HALF], gk_c[NOPE + ROPE_HALF:]
    zeros128 = jnp.zeros((LANES,), F32)
    bf = p["b_forget"][l].astype(F32)
    g128 = jnp.stack([_rope_a(gq1, gq2), _rope_a(gq2, gq1), _rope_a(gk1, gk2), _rope_a(gk2, gk1),
                      jnp.pad(bf, (0, LANES - N_HEADS)), zeros128, zeros128, zeros128])
    bfrow = jnp.broadcast_to(jnp.pad(bf, (0, HF_ROWS - N_HEADS))[:, None], (HF_ROWS, LANES))

    wr = jnp.concatenate([p["w_router_expert"][l], p["w_router_group"][l]], axis=1)
    wr = jnp.pad(wr, ((0, 0), (0, LANES - N_EXPERTS - N_GROUPS))).astype(F32)
    wrh = wr.astype(BF16)
    wrl = (wr - wrh.astype(F32)).astype(BF16)
    br = jnp.pad(jnp.concatenate([p["b_router_expert"][l], p["b_router_group"][l]]).astype(F32),
                 (0, LANES - N_EXPERTS - N_GROUPS))[None, :]
    return {
        "gmix": p["norm_mix"][l][None, :], "wf": wf, "wft": wft, "wuq": wuq_p, "wukv": wukv_p,
        "g512": g512, "g128": g128, "ncq": p["norm_cq"][l][None, :], "nckv": p["norm_ckv"][l][None, :],
        "bfrow": bfrow,
        "wg": w_in[:, o_g:].astype(BF16), "wb": p["w_branch"][l].astype(BF16), "wo": p["w_out"][l].astype(BF16),
        "gffn": p["norm_ffn"][l][None, :], "wr": jnp.concatenate([wrh, wrl], axis=1), "br": br,
        "wgu": jnp.concatenate([_group_cols(p["w_expert_gate"][l]), _group_cols(p["w_expert_up"][l])],
                               axis=-1).astype(BF16),
        "wd": p["w_expert_down"][l].reshape(N_GROUPS, EXPERTS_PER_GROUP * D_FF, D_MODEL).astype(BF16),
    }


def kernel(x, positions, rel_bias, norm_mix, w_in, b_forget, gq_a, gk_a, gq_b, gk_b, gq_c, gk_c, norm_cq, norm_ckv, w_uq, w_ukv, w_branch, w_out, norm_ffn, w_router_group, b_router_group, w_router_expert, b_router_expert, w_expert_gate, w_expert_up, w_expert_down):
    batch, seq_len, d_model = x.shape
    assert d_model == D_MODEL and seq_len % (Q_BLOCK * DILATED_PAIRS[-1][1]) == 0
    assert seq_len == DILATED_PAIRS[-1][0], "the widest dilated window is assumed to span the sequence"
    p = dict(norm_mix=norm_mix, w_in=w_in, b_forget=b_forget, gq_a=gq_a, gk_a=gk_a, gq_b=gq_b, gk_b=gk_b,
             gq_c=gq_c, gk_c=gk_c, norm_cq=norm_cq, norm_ckv=norm_ckv, w_uq=w_uq, w_ukv=w_ukv,
             w_branch=w_branch, w_out=w_out, norm_ffn=norm_ffn, w_router_group=w_router_group,
             b_router_group=b_router_group, w_router_expert=w_router_expert, b_router_expert=b_router_expert,
             w_expert_gate=w_expert_gate, w_expert_up=w_expert_up, w_expert_down=w_expert_down)
    cw = _const_weights(positions, rel_bias)
    xs = x.reshape(batch * seq_len, d_model)
    for l in range(norm_mix.shape[0]):
        lw = _layer_weights(l, p)
        qa, ka, va, qb, kb, vb, qc, kc, vc, ccol, crow = _front_call(xs, cw, lw, seq_len)
        oa = _dilated_call(qa, ka, va, cw["bias"], batch, seq_len)
        ob = _flash_call(qb, kb, vb, batch, seq_len, crow, ccol)
        oc = _flash_call(qc, kc, vc, batch, seq_len)
        x1, gates = _merge_call(xs, oa, ob, oc, lw)
        xs = _moe_call(x1, gates, lw)
    return xs.reshape(batch, seq_len, d_model)
```

```python
import functools
import math

import numpy as np
import jax
import jax.numpy as jnp
from jax import lax
from jax.experimental import pallas as pl
from jax.experimental.pallas import tpu as pltpu

F32 = jnp.float32
BF16 = jnp.bfloat16

D_MODEL = 1024
N_HEADS = 8
HEAD_DIM = 64
N_PAIRS = N_HEADS // 2
PAIR_W = 2 * HEAD_DIM
DILATED_PAIRS = ((128, 1), (512, 4), (2048, 16))
Q_LORA = 256
KV_LORA = 128
NOPE = 64
ROPE = 32
ROPE_HALF = ROPE // 2
ROPE_THETA = 10000.0
N_BUCKETS = 32
MAX_DISTANCE = 2048
Q_BLOCK = 128
N_GROUPS = 4
EXPERTS_PER_GROUP = 4
N_EXPERTS = 16
D_FF = 256
EPS = 1e-6
LANES = 128

VMEM_LIMIT_BYTES = 56 * 1024 * 1024

FRONT_TM = 512
MERGE_TM = 512
MOE_TM = 1024
FLASH_T = 256

C_A = 0
C_B = 1536
C_CQ = 3072
C_CKV = C_CQ + Q_LORA
C_KRA = C_CKV + KV_LORA
C_KRB = C_KRA + LANES
C_HF = C_KRB + LANES
N_FRONT = C_HF + LANES
HF_ROWS = 16


def _nt_dot(a, b):
    return lax.dot_general(a, b, (((1,), (1,)), ((), ())), preferred_element_type=F32)


def _dot(a, b):
    return jnp.dot(a, b, preferred_element_type=F32)


def _log_sigmoid(x):
    return jnp.minimum(x, 0.0) - jnp.log1p(jnp.exp(-jnp.abs(x)))


def _split_hi_lo(x):
    hi = x.astype(BF16)
    lo = (x - hi.astype(F32)).astype(BF16)
    return hi, lo


def _front_kernel(x_ref, gmix_ref, wf_ref, wft_ref, wuq_ref, wukv_ref, g64_ref, g32_ref, tri_ref,
                  g512_ref, g128_ref, ncq_ref, nckv_ref, bfrow_ref, ca_ref, sb_ref,
                  qa_ref, ka_ref, va_ref, qb_ref, kb_ref, vb_ref, qc_ref, kc_ref, vc_ref,
                  ccol_ref, crow_ref, carry_col, carry_row, *, tiles_per_seq):
    i = pl.program_id(0)

    @pl.when(i % tiles_per_seq == 0)
    def _():
        carry_col[...] = jnp.zeros_like(carry_col)
        carry_row[...] = jnp.zeros_like(carry_row)

    x = x_ref[...]
    ms = jnp.mean(x * x, axis=-1, keepdims=True)
    xn = (x * lax.rsqrt(ms + EPS) * gmix_ref[...]).astype(BF16)

    def group_norm(h, gmat, n, gain):
        ss = _dot((h * h).astype(BF16), gmat)
        return h * lax.rsqrt(ss * (1.0 / n) + EPS) * gain

    def store_pairs(ref, val, lo=0, w=PAIR_W):
        for p in range(N_PAIRS):
            ref[p, :, lo:lo + w] = val[:, p * PAIR_W:(p + 1) * PAIR_W].astype(BF16)

    g64 = g64_ref[...]
    for base, q_ref, k_ref, v_ref, row in ((C_A, qa_ref, ka_ref, va_ref, 0), (C_B, qb_ref, kb_ref, vb_ref, 2)):
        hq = _dot(xn, wf_ref[:, base:base + 512])
        store_pairs(q_ref, group_norm(hq, g64, HEAD_DIM, g512_ref[row:row + 1, :]))
        hk = _dot(xn, wf_ref[:, base + 512:base + 1024])
        store_pairs(k_ref, group_norm(hk, g64, HEAD_DIM, g512_ref[row + 1:row + 2, :]))
        hv = _dot(xn, wf_ref[:, base + 1024:base + 1536])
        store_pairs(v_ref, hv)

    hs = _dot(xn, wf_ref[:, C_CQ:N_FRONT])
    ca = ca_ref[...]
    sb = sb_ref[...]

    hcq = hs[:, 0:Q_LORA]
    cq = (hcq * lax.rsqrt(jnp.mean(hcq * hcq, axis=-1, keepdims=True) + EPS) * ncq_ref[...]).astype(BF16)
    qc = _dot(cq, wuq_ref[...])
    qn = group_norm(qc[:, 0:512], g64, NOPE, g512_ref[4:5, :])
    store_pairs(qc_ref, qn, 0)
    qra = qc[:, 512:1024]
    qrb = qc[:, 1024:1536]
    rs = lax.rsqrt(_dot((qra * qra).astype(BF16), g32_ref[...]) * (1.0 / ROPE) + EPS)
    ga = g128_ref[0:1, :]
    gb = g128_ref[1:2, :]
    for p in range(N_PAIRS):
        sl = slice(p * PAIR_W, (p + 1) * PAIR_W)
        qr = (qra[:, sl] * ga * ca + qrb[:, sl] * gb * sb) * rs[:, sl]
        qc_ref[p, :, PAIR_W:2 * PAIR_W] = qr.astype(BF16)

    hckv = hs[:, Q_LORA:Q_LORA + KV_LORA]
    ckv = (hckv * lax.rsqrt(jnp.mean(hckv * hckv, axis=-1, keepdims=True) + EPS) * nckv_ref[...]).astype(BF16)
    kv = _dot(ckv, wukv_ref[...])
    store_pairs(kc_ref, group_norm(kv[:, 0:512], g64, NOPE, g512_ref[5:6, :]), 0)
    store_pairs(vc_ref, kv[:, 512:1024])
    kra = hs[:, C_KRA - C_CQ:C_KRA - C_CQ + LANES]
    krb = hs[:, C_KRB - C_CQ:C_KRB - C_CQ + LANES]
    rsk = lax.rsqrt(_dot((kra * kra).astype(BF16), g32_ref[0:LANES, 0:LANES]) * (1.0 / ROPE) + EPS)
    kr = ((kra * g128_ref[2:3, :] * ca + krb * g128_ref[3:4, :] * sb) * rsk).astype(BF16)
    for p in range(N_PAIRS):
        kc_ref[p, :, PAIR_W:2 * PAIR_W] = kr

    tri = tri_ref[...]
    hf_col = hs[:, C_HF - C_CQ:C_HF - C_CQ + LANES] + g128_ref[4:5, :]
    hi, lo = _split_hi_lo(_log_sigmoid(hf_col))
    ccol = _dot(tri, hi) + _dot(tri, lo) + carry_col[0:1, :]
    ccol_ref[...] = ccol
    tm = ccol.shape[0]
    carry_col[...] = jnp.broadcast_to(ccol[tm - 1:tm, :], carry_col.shape)

    hf_row = _nt_dot(wft_ref[...], xn) + bfrow_ref[:, 0:1]
    hi, lo = _split_hi_lo(_log_sigmoid(hf_row))
    crow = _nt_dot(hi, tri) + _nt_dot(lo, tri) + carry_row[:, 0:1]
    crow_ref[...] = crow
    carry_row[...] = jnp.broadcast_to(crow[:, tm - 1:tm], carry_row.shape)


def _front_call(x2d, cw, lw, seq_len):
    t = x2d.shape[0]
    tm = FRONT_TM
    const = lambda shape: pl.BlockSpec(shape, lambda i: (0,) * len(shape))
    pair_out = lambda w: pl.BlockSpec((N_PAIRS, tm, w), lambda i: (0, i, 0))
    pair_shape = lambda w: jax.ShapeDtypeStruct((N_PAIRS, t, w), BF16)
    in_specs = [
        pl.BlockSpec((tm, D_MODEL), lambda i: (i, 0)),
        const((1, D_MODEL)),
        const((D_MODEL, N_FRONT)),
        const((HF_ROWS, D_MODEL)),
        const((Q_LORA, 1536)),
        const((KV_LORA, 1024)),
        const((512, 512)),
        const((512, 512)),
        const((tm, tm)),
        const((8, 512)),
        const((8, LANES)),
        const((1, Q_LORA)),
        const((1, KV_LORA)),
        const((HF_ROWS, LANES)),
        pl.BlockSpec((tm, LANES), lambda i: (i, 0)),
        pl.BlockSpec((tm, LANES), lambda i: (i, 0)),
    ]
    out_specs = [pair_out(PAIR_W)] * 6 + [pair_out(2 * PAIR_W), pair_out(2 * PAIR_W), pair_out(PAIR_W),
                                           pl.BlockSpec((tm, LANES), lambda i: (i, 0)),
                                           pl.BlockSpec((HF_ROWS, tm), lambda i: (0, i))]
    out_shape = [pair_shape(PAIR_W)] * 6 + [pair_shape(2 * PAIR_W), pair_shape(2 * PAIR_W), pair_shape(PAIR_W),
                                            jax.ShapeDtypeStruct((t, LANES), F32),
                                            jax.ShapeDtypeStruct((HF_ROWS, t), F32)]
    return pl.pallas_call(
        functools.partial(_front_kernel, tiles_per_seq=seq_len // tm),
        grid=(t // tm,),
        in_specs=in_specs,
        out_specs=out_specs,
        out_shape=out_shape,
        scratch_shapes=[pltpu.VMEM((8, LANES), F32), pltpu.VMEM((HF_ROWS, LANES), F32)],
        compiler_params=pltpu.CompilerParams(dimension_semantics=("arbitrary",),
                                             vmem_limit_bytes=VMEM_LIMIT_BYTES),
        name="front",
    )(x2d, lw["gmix"], lw["wf"], lw["wft"], lw["wuq"], lw["wukv"], cw["g64"], cw["g32"], cw["tri"],
      lw["g512"], lw["g128"], lw["ncq"], lw["nckv"], lw["bfrow"], cw["ca"], cw["sb"])


def _dilated_kernel(q_ref, k_ref, v_ref, bias_ref, o_ref, oacc, lacc, s_scr, nat, res4, g4, g16, *, seq_len):
    lane = lax.broadcasted_iota(jnp.int32, (1, PAIR_W), 1)
    left = lane < HEAD_DIM
    d_mid, d_far = DILATED_PAIRS[1][1], DILATED_PAIRS[2][1]
    assert DILATED_PAIRS[0][1] == 1 and d_far == d_mid * d_mid
    n_mid, n_far = seq_len // d_mid, seq_len // d_far

    for ti, src in enumerate((q_ref, k_ref, v_ref)):
        nat[ti] = src[...].astype(F32)
    for ti in range(3):
        for c in range(d_mid):
            rows = nat[ti, pl.ds(c, n_mid, stride=d_mid), :]
            res4[ti, c * n_mid:(c + 1) * n_mid, :] = rows
            g4[ti, c * n_mid:(c + 1) * n_mid, :] = rows.astype(BF16)
    for ti in range(3):
        for c in range(d_far):
            rows = res4[ti, pl.ds((c % d_mid) * n_mid + c // d_mid, n_far, stride=d_mid), :]
            g16[ti, c * n_far:(c + 1) * n_far, :] = rows.astype(BF16)

    readers = (
        (1, lambda ti, lo, hi: (q_ref, k_ref, v_ref)[ti][lo:hi, :]),
        (d_mid, lambda ti, lo, hi: g4[ti, lo:hi, :]),
        (d_far, lambda ti, lo, hi: g16[ti, lo:hi, :]),
    )
    for di, (d, read) in enumerate(readers):
        n_sub = seq_len // d
        n_blocks = n_sub // Q_BLOCK
        blocks = [(c, nb) for c in range(d) for nb in range(n_blocks)]

        def window(ti, c, nb):
            k_lo = max(nb - 1, 0) * Q_BLOCK
            return read(ti, c * n_sub + k_lo, c * n_sub + (nb + 1) * Q_BLOCK)

        row_max = {}
        for bi, (c, nb) in enumerate(blocks):
            q = read(0, c * n_sub + nb * Q_BLOCK, c * n_sub + (nb + 1) * Q_BLOCK)
            kwin = window(1, c, nb)
            w = kwin.shape[0]
            for j in range(2):
                qh = jnp.where(left if j == 0 else jnp.logical_not(left), q, jnp.zeros_like(q))
                s = _nt_dot(qh, kwin) + bias_ref[di, j, :, 2 * Q_BLOCK - w:]
                s_scr[j, bi * Q_BLOCK:(bi + 1) * Q_BLOCK, 0:w] = s
                row_max[bi, j] = jnp.max(s, axis=-1, keepdims=True)
        for bi, (c, nb) in enumerate(blocks):
            vwin = window(2, c, nb)
            w = vwin.shape[0]
            outs, lses = [], []
            for j in range(2):
                m = row_max[bi, j]
                p = jnp.exp(s_scr[j, bi * Q_BLOCK:(bi + 1) * Q_BLOCK, 0:w] - m)
                den = jnp.sum(p, axis=-1, keepdims=True)
                outs.append(_dot(p.astype(BF16), vwin) / den)
                lses.append(m + jnp.log(den))
            start = nb * Q_BLOCK * d + c
            rows = pl.ds(start, Q_BLOCK) if d == 1 else pl.ds(start, Q_BLOCK, stride=d)
            oacc[di, rows, :] = jnp.where(left, outs[0], outs[1])
            lacc[di, rows, :] = jnp.where(left, lses[0], lses[1])

    l0, l1, l2 = lacc[0], lacc[1], lacc[2]
    m = jnp.maximum(jnp.maximum(l0, l1), l2)
    w0, w1, w2 = jnp.exp(l0 - m), jnp.exp(l1 - m), jnp.exp(l2 - m)
    wsum = w0 + w1 + w2
    o = oacc[0] * (w0 / wsum) + oacc[1] * (w1 / wsum) + oacc[2] * (w2 / wsum)
    o_ref[...] = o.astype(BF16)


def _dilated_call(q, k, v, bias, batch, seq_len):
    n_pat = len(DILATED_PAIRS)
    seq_block = pl.BlockSpec((None, None, seq_len, PAIR_W), lambda p, b: (p, b, 0, 0))
    view = lambda a: a.reshape(N_PAIRS, batch, seq_len, PAIR_W)
    seq_scratch = lambda n, dt: pltpu.VMEM((n, seq_len, PAIR_W), dt)
    out = pl.pallas_call(
        functools.partial(_dilated_kernel, seq_len=seq_len),
        grid=(N_PAIRS, batch),
        in_specs=[seq_block, seq_block, seq_block,
                  pl.BlockSpec((n_pat, 2, Q_BLOCK, 2 * Q_BLOCK), lambda p, b: (0, p, 0, 0))],
        out_specs=seq_block,
        out_shape=jax.ShapeDtypeStruct((N_PAIRS, batch, seq_len, PAIR_W), BF16),
        scratch_shapes=[seq_scratch(n_pat, F32), seq_scratch(n_pat, F32),
                        pltpu.VMEM((2, seq_len, 2 * Q_BLOCK), F32),
                        seq_scratch(3, F32), seq_scratch(3, F32), seq_scratch(3, BF16), seq_scratch(3, BF16)],
        compiler_params=pltpu.CompilerParams(dimension_semantics=("arbitrary", "arbitrary"),
                                             vmem_limit_bytes=VMEM_LIMIT_BYTES),
        name="dilated",
    )(view(q), view(k), view(v), bias)
    return out.reshape(N_PAIRS, batch * seq_len, PAIR_W)


def _flash_kernel(*refs, kw, decay, seq_len):
    if decay:
        q_ref, k_ref, v_ref, crow_ref, ccol_ref, o_ref, s_scr = refs
    else:
        q_ref, k_ref, v_ref, o_ref, s_scr = refs
    hp = pl.program_id(0)
    t = FLASH_T
    lane = lax.broadcasted_iota(jnp.int32, (1, kw), 1)
    in_h0 = (lane < HEAD_DIM) | ((lane >= PAIR_W) & (lane < PAIR_W + ROPE))
    in_h1 = ((lane >= HEAD_DIM) & (lane < PAIR_W)) | ((lane >= PAIR_W + ROPE) & (lane < PAIR_W + 2 * ROPE))
    row = lax.broadcasted_iota(jnp.int32, (t, t), 0)
    col = lax.broadcasted_iota(jnp.int32, (t, t), 1)
    causal = col <= row
    out_lane = lax.broadcasted_iota(jnp.int32, (1, PAIR_W), 1)
    for qi in range(seq_len // t):
        rows = slice(qi * t, (qi + 1) * t)
        q = q_ref[rows, :]
        outs = []
        for j, sel in enumerate((in_h0, in_h1)):
            qh = jnp.where(sel, q, jnp.zeros_like(q))
            mx = None
            for c in range(qi + 1):
                cols = slice(c * t, (c + 1) * t)
                s = _nt_dot(qh, k_ref[cols, :])
                if decay:
                    s = s - crow_ref[pl.ds(2 * hp + j, 1), cols]
                if c == qi:
                    s = jnp.where(causal, s, -jnp.inf)
                s_scr[j, :, cols] = s
                mx = s if mx is None else jnp.maximum(mx, s)
            shift = jnp.max(mx, axis=-1, keepdims=True)
            if decay:
                head = (2 * hp + j).astype(F32)
                cc = ccol_ref[rows, :]
                lane_c = lax.broadcasted_iota(jnp.int32, cc.shape, 1).astype(F32)
                cq = jnp.sum(jnp.where(lane_c == head, cc, 0.0), axis=-1, keepdims=True)
                shift = (shift + cq) - cq
            acc = None
            lsum = None
            for c in range(qi + 1):
                cols = slice(c * t, (c + 1) * t)
                p = jnp.exp(s_scr[j, :, cols] - shift)
                pv = _dot(p.astype(BF16), v_ref[cols, :])
                lsum = p if lsum is None else lsum + p
                acc = pv if acc is None else acc + pv
            outs.append(acc / jnp.sum(lsum, axis=-1, keepdims=True))
        o_ref[rows, :] = jnp.where(out_lane < HEAD_DIM, outs[0], outs[1]).astype(BF16)


def _flash_call(q, k, v, batch, seq_len, crow=None, ccol=None):
    kw = q.shape[-1]
    decay = crow is not None
    view = lambda a: a.reshape(N_PAIRS, batch, seq_len, a.shape[-1])
    args = [view(q), view(k), view(v)]
    seq_block = lambda w: pl.BlockSpec((None, None, seq_len, w), lambda p, b: (p, b, 0, 0))
    in_specs = [seq_block(kw), seq_block(kw), seq_block(PAIR_W)]
    if decay:
        args += [crow, ccol]
        in_specs += [pl.BlockSpec((HF_ROWS, seq_len), lambda p, b: (0, b)),
                     pl.BlockSpec((seq_len, LANES), lambda p, b: (b, 0))]
    out = pl.pallas_call(
        functools.partial(_flash_kernel, kw=kw, decay=decay, seq_len=seq_len),
        grid=(N_PAIRS, batch),
        in_specs=in_specs,
        out_specs=seq_block(PAIR_W),
        out_shape=jax.ShapeDtypeStruct((N_PAIRS, batch, seq_len, PAIR_W), BF16),
        scratch_shapes=[pltpu.VMEM((2, FLASH_T, seq_len), F32)],
        compiler_params=pltpu.CompilerParams(dimension_semantics=("arbitrary",) * 2,
                                             vmem_limit_bytes=VMEM_LIMIT_BYTES),
        name="flash_fox" if decay else "flash_mla",
    )(*args)
    return out.reshape(N_PAIRS, batch * seq_len, PAIR_W)


def _merge_kernel(x_ref, oa_ref, ob_ref, oc_ref, gmix_ref, wg_ref, wb_ref, wo_ref, gffn_ref,
                  wr_ref, br_ref, x1_ref, gates_ref):
    x = x_ref[...]
    ms = jnp.mean(x * x, axis=-1, keepdims=True)
    xn = (x * lax.rsqrt(ms + EPS) * gmix_ref[...]).astype(BF16)
    mixed = None
    for g, o_ref in enumerate((oa_ref, ob_ref, oc_ref)):
        o = jnp.concatenate([o_ref[p] for p in range(N_PAIRS)], axis=-1)
        gate = jax.nn.sigmoid(_dot(xn, wg_ref[:, g * D_MODEL:(g + 1) * D_MODEL]))
        term = gate * _dot(o, wb_ref[g])
        mixed = term if mixed is None else mixed + term
    x1 = x + _dot(mixed.astype(BF16), wo_ref[...])
    x1_ref[...] = x1

    ms1 = jnp.mean(x1 * x1, axis=-1, keepdims=True)
    xf = x1 * lax.rsqrt(ms1 + EPS) * gffn_ref[...]
    hi, lo = _split_hi_lo(xf)
    tm = xf.shape[0]
    parts = _dot(jnp.concatenate([hi, lo], axis=0), wr_ref[...])
    lg = (parts[0:tm, 0:LANES] + parts[0:tm, LANES:2 * LANES]
          + parts[tm:2 * tm, 0:LANES] + parts[tm:2 * tm, LANES:2 * LANES]) + br_ref[...]
    lane = lax.broadcasted_iota(jnp.int32, lg.shape, 1)
    lane_f = lane.astype(F32)
    neg = -jnp.inf
    far = float(LANES)
    gl = jnp.where((lane >= N_EXPERTS) & (lane < N_EXPERTS + N_GROUPS), lg, neg)
    gmax = jnp.max(gl, axis=-1, keepdims=True)
    pg_top = 1.0 / jnp.sum(jnp.exp(gl - gmax), axis=-1, keepdims=True)
    gidx = jnp.min(jnp.where(gl == gmax, lane_f, far), axis=-1, keepdims=True) - float(N_EXPERTS)
    in_group = (lane < N_EXPERTS) & ((lane // EXPERTS_PER_GROUP).astype(F32) == gidx)
    ev = jnp.where(in_group, lg, neg)
    v1 = jnp.max(ev, axis=-1, keepdims=True)
    i1 = jnp.min(jnp.where(ev == v1, lane_f, far), axis=-1, keepdims=True)
    ev2 = jnp.where(lane_f == i1, neg, ev)
    v2 = jnp.max(ev2, axis=-1, keepdims=True)
    i2 = jnp.min(jnp.where(ev2 == v2, lane_f, far), axis=-1, keepdims=True)
    e2 = jnp.exp(v2 - v1)
    den = 1.0 + e2
    w1 = (1.0 / den) * pg_top
    w2 = (e2 / den) * pg_top
    gates_ref[...] = jnp.where(lane_f == i1, w1, jnp.where(lane_f == i2, w2, 0.0))


def _merge_call(x2d, oa, ob, oc, lw):
    t = x2d.shape[0]
    tm = MERGE_TM
    const = lambda shape: pl.BlockSpec(shape, lambda i: (0,) * len(shape))
    row_tile = lambda w: pl.BlockSpec((tm, w), lambda i: (i, 0))
    pair_in = pl.BlockSpec((N_PAIRS, tm, PAIR_W), lambda i: (0, i, 0))
    return pl.pallas_call(
        _merge_kernel,
        grid=(t // tm,),
        in_specs=[row_tile(D_MODEL), pair_in, pair_in, pair_in, const((1, D_MODEL)),
                  const((D_MODEL, 3 * D_MODEL)), const((3, 512, D_MODEL)), const((D_MODEL, D_MODEL)),
                  const((1, D_MODEL)), const((D_MODEL, 2 * LANES)), const((1, LANES))],
        out_specs=[row_tile(D_MODEL), row_tile(LANES)],
        out_shape=[jax.ShapeDtypeStruct((t, D_MODEL), F32), jax.ShapeDtypeStruct((t, LANES), F32)],
        compiler_params=pltpu.CompilerParams(dimension_semantics=("arbitrary",),
                                             vmem_limit_bytes=VMEM_LIMIT_BYTES),
        name="merge",
    )(x2d, oa, ob, oc, lw["gmix"], lw["wg"], lw["wb"], lw["wo"], lw["gffn"], lw["wr"], lw["br"])


def _moe_kernel(x_ref, gates_ref, gffn_ref, wgu_ref, wd_ref, o_ref, xn_ref):
    g = pl.program_id(1)

    @pl.when(g == 0)
    def _():
        x = x_ref[...]
        ms = jnp.mean(x * x, axis=-1, keepdims=True)
        xn_ref[...] = (x * lax.rsqrt(ms + EPS) * gffn_ref[...]).astype(BF16)
        o_ref[...] = x

    xn = xn_ref[...]
    gates = gates_ref[...]
    lane = lax.broadcasted_iota(jnp.int32, gates.shape, 1)
    ff_all = EXPERTS_PER_GROUP * D_FF
    parts = []
    for e in range(EXPERTS_PER_GROUP):
        h_gate = _dot(xn, wgu_ref[:, e * D_FF:(e + 1) * D_FF])
        h_up = _dot(xn, wgu_ref[:, ff_all + e * D_FF:ff_all + (e + 1) * D_FF])
        gate = jnp.sum(jnp.where(lane == g * EXPERTS_PER_GROUP + e, gates, 0.0), axis=-1, keepdims=True)
        parts.append((jax.nn.silu(h_gate) * h_up * gate).astype(BF16))
    o_ref[...] += _dot(jnp.concatenate(parts, axis=-1), wd_ref[...])


def _moe_call(x1, gates, lw):
    t = x1.shape[0]
    tm = MOE_TM
    ff_all = EXPERTS_PER_GROUP * D_FF
    return pl.pallas_call(
        _moe_kernel,
        grid=(t // tm, N_GROUPS),
        in_specs=[pl.BlockSpec((tm, D_MODEL), lambda i, g: (i, 0)),
                  pl.BlockSpec((tm, LANES), lambda i, g: (i, 0)),
                  pl.BlockSpec((1, D_MODEL), lambda i, g: (0, 0)),
                  pl.BlockSpec((None, D_MODEL, 2 * ff_all), lambda i, g: (g, 0, 0)),
                  pl.BlockSpec((None, ff_all, D_MODEL), lambda i, g: (g, 0, 0))],
        out_specs=pl.BlockSpec((tm, D_MODEL), lambda i, g: (i, 0)),
        out_shape=jax.ShapeDtypeStruct((t, D_MODEL), F32),
        scratch_shapes=[pltpu.VMEM((tm, D_MODEL), BF16)],
        compiler_params=pltpu.CompilerParams(dimension_semantics=("arbitrary", "arbitrary"),
                                             vmem_limit_bytes=VMEM_LIMIT_BYTES),
        name="moe",
    )(x1, gates, lw["gffn"], lw["wgu"], lw["wd"])


def _t5_bucket(dist):
    max_exact = N_BUCKETS // 2
    log_ratio = np.log(np.maximum(dist, max_exact) / max_exact) / np.log(MAX_DISTANCE / max_exact)
    large = np.minimum(max_exact + (log_ratio * (N_BUCKETS - max_exact)).astype(np.int32), N_BUCKETS - 1)
    return np.where(dist < max_exact, dist, large).astype(np.int32)


def _dilated_bias(rel_bias):
    tables = []
    span = 3 * Q_BLOCK
    for window, dil in DILATED_PAIRS:
        assert window // dil == Q_BLOCK
        per_dist = rel_bias[_t5_bucket(np.arange(Q_BLOCK + 1) * dil)].astype(F32).T
        diag = jnp.concatenate([jnp.full((N_HEADS, Q_BLOCK), -jnp.inf, F32), per_dist[:, ::-1],
                                jnp.full((N_HEADS, span - 2 * Q_BLOCK), -jnp.inf, F32)], axis=1)
        skew = jnp.tile(diag, (1, Q_BLOCK))[:, :Q_BLOCK * span].reshape(N_HEADS, Q_BLOCK, span)
        tables.append(skew[:, :, Q_BLOCK:])
    return jnp.stack(tables, axis=0)


def _rope_a(v1, v2):
    pad = jnp.zeros(v1.shape[:-1] + (LANES - 4 * ROPE_HALF,), v1.dtype)
    return jnp.concatenate([v1, v2, v1, v2, pad], axis=-1)


def _const_weights(positions, rel_bias):
    b, s = positions.shape
    inv_freq = ROPE_THETA ** (-jnp.arange(ROPE_HALF, dtype=F32) / ROPE_HALF)
    ang = positions.astype(F32).reshape(b * s, 1) * inv_freq
    cos, sin = jnp.cos(ang), jnp.sin(ang)
    idx = np.arange(512)
    return {
        "ca": _rope_a(cos, cos),
        "sb": _rope_a(-sin, sin),
        "g64": jnp.asarray(idx[:, None] // HEAD_DIM == idx[None, :] // HEAD_DIM, BF16),
        "g32": jnp.asarray(idx[:, None] // ROPE == idx[None, :] // ROPE, BF16),
        "tri": jnp.asarray(np.arange(FRONT_TM)[None, :] <= np.arange(FRONT_TM)[:, None], BF16),
        "bias": _dilated_bias(rel_bias),
    }


def _group_cols(w):
    w = w.reshape(N_GROUPS, EXPERTS_PER_GROUP, D_MODEL, D_FF)
    return w.transpose(0, 2, 1, 3).reshape(N_GROUPS, D_MODEL, EXPERTS_PER_GROUP * D_FF)


def _layer_weights(l, p):
    w_in = p["w_in"][l]
    o_hf = 3072
    o_cq = o_hf + N_HEADS
    o_ckv = o_cq + Q_LORA
    o_kr = o_ckv + KV_LORA
    o_g = o_kr + ROPE
    kr1 = w_in[:, o_kr:o_kr + ROPE_HALF]
    kr2 = w_in[:, o_kr + ROPE_HALF:o_kr + ROPE]
    hf_w = w_in[:, o_hf:o_cq]
    wf = jnp.concatenate([
        w_in[:, 0:3072], w_in[:, o_cq:o_ckv], w_in[:, o_ckv:o_kr],
        _rope_a(kr1, kr2), _rope_a(kr2, kr1),
        jnp.pad(hf_w, ((0, 0), (0, LANES - N_HEADS))),
    ], axis=1).astype(BF16)
    wft = jnp.pad(hf_w.T, ((0, HF_ROWS - N_HEADS), (0, 0))).astype(BF16)

    wuq = p["w_uq"][l].reshape(Q_LORA, N_HEADS, NOPE + ROPE)
    q_nope = wuq[:, :, :NOPE].reshape(Q_LORA, N_HEADS * NOPE)
    q1 = wuq[:, :, NOPE:NOPE + ROPE_HALF]
    q2 = wuq[:, :, NOPE + ROPE_HALF:]

    def pair_rope(v1, v2):
        x = jnp.concatenate([v1, v2], axis=-1).reshape(Q_LORA, N_PAIRS, 2 * ROPE)
        return jnp.pad(x, ((0, 0), (0, 0), (0, LANES - 2 * ROPE))).reshape(Q_LORA, N_PAIRS * LANES)

    wuq_p = jnp.concatenate([q_nope, pair_rope(q1, q2), pair_rope(q2, q1)], axis=1).astype(BF16)
    wukv = p["w_ukv"][l].reshape(KV_LORA, N_HEADS, NOPE + HEAD_DIM)
    wukv_p = jnp.concatenate([wukv[:, :, :NOPE].reshape(KV_LORA, 512),
                              wukv[:, :, NOPE:].reshape(KV_LORA, 512)], axis=1).astype(BF16)

    tile8 = lambda g: jnp.tile(g, N_HEADS)
    sc_ab = 1.0 / math.sqrt(HEAD_DIM)
    sc_c = 1.0 / math.sqrt(NOPE + ROPE)
    gq_c, gk_c = p["gq_c"][l], p["gk_c"][l]
    zeros512 = jnp.zeros((512,), F32)
    g512 = jnp.stack([tile8(p["gq_a"][l]) * sc_ab, tile8(p["gk_a"][l]),
                      tile8(p["gq_b"][l]) * sc_ab, tile8(p["gk_b"][l]),
                      tile8(gq_c[:NOPE]) * sc_c, tile8(gk_c[:NOPE]), zeros512, zeros512])
    gq1, gq2 = gq_c[NOPE:NOPE + ROPE_HALF] * sc_c, gq_c[NOPE + ROPE_HALF:] * sc_c
    gk1, gk2 = gk_c[NOPE:NOPE + ROPE_HALF], gk_c[NOPE + ROPE_HALF:]
    zeros128 = jnp.zeros((LANES,), F32)
    bf = p["b_forget"][l].astype(F32)
    g128 = jnp.stack([_rope_a(gq1, gq2), _rope_a(gq2, gq1), _rope_a(gk1, gk2), _rope_a(gk2, gk1),
                      jnp.pad(bf, (0, LANES - N_HEADS)), zeros128, zeros128, zeros128])
    bfrow = jnp.broadcast_to(jnp.pad(bf, (0, HF_ROWS - N_HEADS))[:, None], (HF_ROWS, LANES))

    wr = jnp.concatenate([p["w_router_expert"][l], p["w_router_group"][l]], axis=1)
    wr = jnp.pad(wr, ((0, 0), (0, LANES - N_EXPERTS - N_GROUPS))).astype(F32)
    wrh = wr.astype(BF16)
    wrl = (wr - wrh.astype(F32)).astype(BF16)
    br = jnp.pad(jnp.concatenate([p["b_router_expert"][l], p["b_router_group"][l]]).astype(F32),
                 (0, LANES - N_EXPERTS - N_GROUPS))[None, :]
    return {
        "gmix": p["norm_mix"][l][None, :], "wf": wf, "wft": wft, "wuq": wuq_p, "wukv": wukv_p,
        "g512": g512, "g128": g128, "ncq": p["norm_cq"][l][None, :], "nckv": p["norm_ckv"][l][None, :],
        "bfrow": bfrow,
        "wg": w_in[:, o_g:].astype(BF16), "wb": p["w_branch"][l].astype(BF16), "wo": p["w_out"][l].astype(BF16),
        "gffn": p["norm_ffn"][l][None, :], "wr": jnp.concatenate([wrh, wrl], axis=1), "br": br,
        "wgu": jnp.concatenate([_group_cols(p["w_expert_gate"][l]), _group_cols(p["w_expert_up"][l])],
                               axis=-1).astype(BF16),
        "wd": p["w_expert_down"][l].reshape(N_GROUPS, EXPERTS_PER_GROUP * D_FF, D_MODEL).astype(BF16),
    }


def kernel(x, positions, rel_bias, norm_mix, w_in, b_forget, gq_a, gk_a, gq_b, gk_b, gq_c, gk_c, norm_cq, norm_ckv, w_uq, w_ukv, w_branch, w_out, norm_ffn, w_router_group, b_router_group, w_router_expert, b_router_expert, w_expert_gate, w_expert_up, w_expert_down):
    batch, seq_len, d_model = x.shape
    assert d_model == D_MODEL and seq_len % (Q_BLOCK * DILATED_PAIRS[-1][1]) == 0
    assert seq_len == DILATED_PAIRS[-1][0], "the widest dilated window is assumed to span the sequence"
    p = dict(norm_mix=norm_mix, w_in=w_in, b_forget=b_forget, gq_a=gq_a, gk_a=gk_a, gq_b=gq_b, gk_b=gk_b,
             gq_c=gq_c, gk_c=gk_c, norm_cq=norm_cq, norm_ckv=norm_ckv, w_uq=w_uq, w_ukv=w_ukv,
             w_branch=w_branch, w_out=w_out, norm_ffn=norm_ffn, w_router_group=w_router_group,
             b_router_group=b_router_group, w_router_expert=w_router_expert, b_router_expert=b_router_expert,
             w_expert_gate=w_expert_gate, w_expert_up=w_expert_up, w_expert_down=w_expert_down)
    cw = _const_weights(positions, rel_bias)
    xs = x.reshape(batch * seq_len, d_model)
    for l in range(norm_mix.shape[0]):
        lw = _layer_weights(l, p)
        qa, ka, va, qb, kb, vb, qc, kc, vc, ccol, crow = _front_call(xs, cw, lw, seq_len)
        oa = _dilated_call(qa, ka, va, cw["bias"], batch, seq_len)
        ob = _flash_call(qb, kb, vb, batch, seq_len, crow, ccol)
        oc = _flash_call(qc, kc, vc, batch, seq_len)
        x1, gates = _merge_call(xs, oa, ob, oc, lw)
        xs = _moe_call(x1, gates, lw)
    return xs.reshape(batch, seq_len, d_model)
```

```python
import functools
import math

import numpy as np
import jax
import jax.numpy as jnp
from jax import lax
from jax.experimental import pallas as pl
from jax.experimental.pallas import tpu as pltpu

F32 = jnp.float32
BF16 = jnp.bfloat16

D_MODEL = 1024
N_HEADS = 8
HEAD_DIM = 64
N_PAIRS = N_HEADS // 2
PAIR_W = 2 * HEAD_DIM
DILATED_PAIRS = ((128, 1), (512, 4), (2048, 16))
Q_LORA = 256
KV_LORA = 128
NOPE = 64
ROPE = 32
ROPE_HALF = ROPE // 2
ROPE_THETA = 10000.0
N_BUCKETS = 32
MAX_DISTANCE = 2048
Q_BLOCK = 128
N_GROUPS = 4
EXPERTS_PER_GROUP = 4
N_EXPERTS = 16
D_FF = 256
EPS = 1e-6
LANES = 128

VMEM_LIMIT_BYTES = 56 * 1024 * 1024

FRONT_TM = 512
MERGE_TM = 512
MOE_TM = 512
MOE_CHUNK = 128
MOE_ROWS = MOE_TM + (N_GROUPS - 1) * MOE_CHUNK
GROUP_LANE = N_EXPERTS
FLASH_T = 256

C_A = 0
C_B = 1536
C_CQ = 3072
C_CKV = C_CQ + Q_LORA
C_KRA = C_CKV + KV_LORA
C_KRB = C_KRA + LANES
C_HF = C_KRB + LANES
N_FRONT = C_HF + LANES
HF_ROWS = 16


def _nt_dot(a, b):
    return lax.dot_general(a, b, (((1,), (1,)), ((), ())), preferred_element_type=F32)


def _dot(a, b):
    return jnp.dot(a, b, preferred_element_type=F32)


def _log_sigmoid(x):
    return jnp.minimum(x, 0.0) - jnp.log1p(jnp.exp(-jnp.abs(x)))


def _split_hi_lo(x):
    hi = x.astype(BF16)
    lo = (x - hi.astype(F32)).astype(BF16)
    return hi, lo


def _front_kernel(x_ref, gmix_ref, wf_ref, wft_ref, wuq_ref, wukv_ref, g64_ref, g32_ref, tri_ref,
                  g512_ref, g128_ref, ncq_ref, nckv_ref, bfrow_ref, ca_ref, sb_ref,
                  qa_ref, ka_ref, va_ref, qb_ref, kb_ref, vb_ref, qc_ref, kc_ref, vc_ref,
                  ccol_ref, crow_ref, carry_col, carry_row, *, tiles_per_seq):
    i = pl.program_id(0)

    @pl.when(i % tiles_per_seq == 0)
    def _():
        carry_col[...] = jnp.zeros_like(carry_col)
        carry_row[...] = jnp.zeros_like(carry_row)

    x = x_ref[...]
    ms = jnp.mean(x * x, axis=-1, keepdims=True)
    xn = (x * lax.rsqrt(ms + EPS) * gmix_ref[...]).astype(BF16)

    def group_norm(h, gmat, n, gain):
        ss = _dot((h * h).astype(BF16), gmat)
        return h * lax.rsqrt(ss * (1.0 / n) + EPS) * gain

    def store_pairs(ref, val, lo=0, w=PAIR_W):
        for p in range(N_PAIRS):
            ref[p, :, lo:lo + w] = val[:, p * PAIR_W:(p + 1) * PAIR_W].astype(BF16)

    g64 = g64_ref[...]
    for base, q_ref, k_ref, v_ref, row in ((C_A, qa_ref, ka_ref, va_ref, 0), (C_B, qb_ref, kb_ref, vb_ref, 2)):
        hq = _dot(xn, wf_ref[:, base:base + 512])
        store_pairs(q_ref, group_norm(hq, g64, HEAD_DIM, g512_ref[row:row + 1, :]))
        hk = _dot(xn, wf_ref[:, base + 512:base + 1024])
        store_pairs(k_ref, group_norm(hk, g64, HEAD_DIM, g512_ref[row + 1:row + 2, :]))
        hv = _dot(xn, wf_ref[:, base + 1024:base + 1536])
        store_pairs(v_ref, hv)

    hs = _dot(xn, wf_ref[:, C_CQ:N_FRONT])
    ca = ca_ref[...]
    sb = sb_ref[...]

    hcq = hs[:, 0:Q_LORA]
    cq = (hcq * lax.rsqrt(jnp.mean(hcq * hcq, axis=-1, keepdims=True) + EPS) * ncq_ref[...]).astype(BF16)
    qc = _dot(cq, wuq_ref[...])
    qn = group_norm(qc[:, 0:512], g64, NOPE, g512_ref[4:5, :])
    store_pairs(qc_ref, qn, 0)
    qra = qc[:, 512:1024]
    qrb = qc[:, 1024:1536]
    rs = lax.rsqrt(_dot((qra * qra).astype(BF16), g32_ref[...]) * (1.0 / ROPE) + EPS)
    ga = g128_ref[0:1, :]
    gb = g128_ref[1:2, :]
    for p in range(N_PAIRS):
        sl = slice(p * PAIR_W, (p + 1) * PAIR_W)
        qr = (qra[:, sl] * ga * ca + qrb[:, sl] * gb * sb) * rs[:, sl]
        qc_ref[p, :, PAIR_W:2 * PAIR_W] = qr.astype(BF16)

    hckv = hs[:, Q_LORA:Q_LORA + KV_LORA]
    ckv = (hckv * lax.rsqrt(jnp.mean(hckv * hckv, axis=-1, keepdims=True) + EPS) * nckv_ref[...]).astype(BF16)
    kv = _dot(ckv, wukv_ref[...])
    store_pairs(kc_ref, group_norm(kv[:, 0:512], g64, NOPE, g512_ref[5:6, :]), 0)
    store_pairs(vc_ref, kv[:, 512:1024])
    kra = hs[:, C_KRA - C_CQ:C_KRA - C_CQ + LANES]
    krb = hs[:, C_KRB - C_CQ:C_KRB - C_CQ + LANES]
    rsk = lax.rsqrt(_dot((kra * kra).astype(BF16), g32_ref[0:LANES, 0:LANES]) * (1.0 / ROPE) + EPS)
    kr = ((kra * g128_ref[2:3, :] * ca + krb * g128_ref[3:4, :] * sb) * rsk).astype(BF16)
    for p in range(N_PAIRS):
        kc_ref[p, :, PAIR_W:2 * PAIR_W] = kr

    tri = tri_ref[...]
    hf_col = hs[:, C_HF - C_CQ:C_HF - C_CQ + LANES] + g128_ref[4:5, :]
    hi, lo = _split_hi_lo(_log_sigmoid(hf_col))
    ccol = _dot(tri, hi) + _dot(tri, lo) + carry_col[0:1, :]
    ccol_ref[...] = ccol
    tm = ccol.shape[0]
    carry_col[...] = jnp.broadcast_to(ccol[tm - 1:tm, :], carry_col.shape)

    hf_row = _nt_dot(wft_ref[...], xn) + bfrow_ref[:, 0:1]
    hi, lo = _split_hi_lo(_log_sigmoid(hf_row))
    crow = _nt_dot(hi, tri) + _nt_dot(lo, tri) + carry_row[:, 0:1]
    crow_ref[...] = crow
    carry_row[...] = jnp.broadcast_to(crow[:, tm - 1:tm], carry_row.shape)


def _front_call(x2d, cw, lw, seq_len):
    t = x2d.shape[0]
    tm = FRONT_TM
    const = lambda shape: pl.BlockSpec(shape, lambda i: (0,) * len(shape))
    pair_out = lambda w: pl.BlockSpec((N_PAIRS, tm, w), lambda i: (0, i, 0))
    pair_shape = lambda w: jax.ShapeDtypeStruct((N_PAIRS, t, w), BF16)
    in_specs = [
        pl.BlockSpec((tm, D_MODEL), lambda i: (i, 0)),
        const((1, D_MODEL)),
        const((D_MODEL, N_FRONT)),
        const((HF_ROWS, D_MODEL)),
        const((Q_LORA, 1536)),
        const((KV_LORA, 1024)),
        const((512, 512)),
        const((512, 512)),
        const((tm, tm)),
        const((8, 512)),
        const((8, LANES)),
        const((1, Q_LORA)),
        const((1, KV_LORA)),
        const((HF_ROWS, LANES)),
        pl.BlockSpec((tm, LANES), lambda i: (i, 0)),
        pl.BlockSpec((tm, LANES), lambda i: (i, 0)),
    ]
    out_specs = [pair_out(PAIR_W)] * 6 + [pair_out(2 * PAIR_W), pair_out(2 * PAIR_W), pair_out(PAIR_W),
                                           pl.BlockSpec((tm, LANES), lambda i: (i, 0)),
                                           pl.BlockSpec((HF_ROWS, tm), lambda i: (0, i))]
    out_shape = [pair_shape(PAIR_W)] * 6 + [pair_shape(2 * PAIR_W), pair_shape(2 * PAIR_W), pair_shape(PAIR_W),
                                            jax.ShapeDtypeStruct((t, LANES), F32),
                                            jax.ShapeDtypeStruct((HF_ROWS, t), F32)]
    return pl.pallas_call(
        functools.partial(_front_kernel, tiles_per_seq=seq_len // tm),
        grid=(t // tm,),
        in_specs=in_specs,
        out_specs=out_specs,
        out_shape=out_shape,
        scratch_shapes=[pltpu.VMEM((8, LANES), F32), pltpu.VMEM((HF_ROWS, LANES), F32)],
        compiler_params=pltpu.CompilerParams(dimension_semantics=("arbitrary",),
                                             vmem_limit_bytes=VMEM_LIMIT_BYTES),
        name="front",
    )(x2d, lw["gmix"], lw["wf"], lw["wft"], lw["wuq"], lw["wukv"], cw["g64"], cw["g32"], cw["tri"],
      lw["g512"], lw["g128"], lw["ncq"], lw["nckv"], lw["bfrow"], cw["ca"], cw["sb"])


def _dilated_kernel(q_ref, k_ref, v_ref, bias_ref, o_ref, oacc, lacc, s_scr, nat, res4, g4, g16, *, seq_len):
    lane = lax.broadcasted_iota(jnp.int32, (1, PAIR_W), 1)
    left = lane < HEAD_DIM
    d_mid, d_far = DILATED_PAIRS[1][1], DILATED_PAIRS[2][1]
    assert DILATED_PAIRS[0][1] == 1 and d_far == d_mid * d_mid
    n_mid, n_far = seq_len // d_mid, seq_len // d_far

    for ti, src in enumerate((q_ref, k_ref, v_ref)):
        nat[ti] = src[...].astype(F32)
    for ti in range(3):
        for c in range(d_mid):
            rows = nat[ti, pl.ds(c, n_mid, stride=d_mid), :]
            res4[ti, c * n_mid:(c + 1) * n_mid, :] = rows
            g4[ti, c * n_mid:(c + 1) * n_mid, :] = rows.astype(BF16)
    for ti in range(3):
        for c in range(d_far):
            rows = res4[ti, pl.ds((c % d_mid) * n_mid + c // d_mid, n_far, stride=d_mid), :]
            g16[ti, c * n_far:(c + 1) * n_far, :] = rows.astype(BF16)

    readers = (
        (1, lambda ti, lo, hi: (q_ref, k_ref, v_ref)[ti][lo:hi, :]),
        (d_mid, lambda ti, lo, hi: g4[ti, lo:hi, :]),
        (d_far, lambda ti, lo, hi: g16[ti, lo:hi, :]),
    )
    for di, (d, read) in enumerate(readers):
        n_sub = seq_len // d
        n_blocks = n_sub // Q_BLOCK
        blocks = [(c, nb) for c in range(d) for nb in range(n_blocks)]

        def window(ti, c, nb):
            k_lo = max(nb - 1, 0) * Q_BLOCK
            return read(ti, c * n_sub + k_lo, c * n_sub + (nb + 1) * Q_BLOCK)

        row_max = {}
        for bi, (c, nb) in enumerate(blocks):
            q = read(0, c * n_sub + nb * Q_BLOCK, c * n_sub + (nb + 1) * Q_BLOCK)
            kwin = window(1, c, nb)
            w = kwin.shape[0]
            for j in range(2):
                qh = jnp.where(left if j == 0 else jnp.logical_not(left), q, jnp.zeros_like(q))
                s = _nt_dot(qh, kwin) + bias_ref[di, j, :, 2 * Q_BLOCK - w:]
                s_scr[j, bi * Q_BLOCK:(bi + 1) * Q_BLOCK, 0:w] = s
                row_max[bi, j] = jnp.max(s, axis=-1, keepdims=True)
        for bi, (c, nb) in enumerate(blocks):
            vwin = window(2, c, nb)
            w = vwin.shape[0]
            outs, lses = [], []
            for j in range(2):
                m = row_max[bi, j]
                p = jnp.exp(s_scr[j, bi * Q_BLOCK:(bi + 1) * Q_BLOCK, 0:w] - m)
                den = jnp.sum(p, axis=-1, keepdims=True)
                outs.append(_dot(p.astype(BF16), vwin) / den)
                lses.append(m + jnp.log(den))
            start = nb * Q_BLOCK * d + c
            rows = pl.ds(start, Q_BLOCK) if d == 1 else pl.ds(start, Q_BLOCK, stride=d)
            oacc[di, rows, :] = jnp.where(left, outs[0], outs[1])
            lacc[di, rows, :] = jnp.where(left, lses[0], lses[1])

    l0, l1, l2 = lacc[0], lacc[1], lacc[2]
    m = jnp.maximum(jnp.maximum(l0, l1), l2)
    w0, w1, w2 = jnp.exp(l0 - m), jnp.exp(l1 - m), jnp.exp(l2 - m)
    wsum = w0 + w1 + w2
    o = oacc[0] * (w0 / wsum) + oacc[1] * (w1 / wsum) + oacc[2] * (w2 / wsum)
    o_ref[...] = o.astype(BF16)


def _dilated_call(q, k, v, bias, batch, seq_len):
    n_pat = len(DILATED_PAIRS)
    seq_block = pl.BlockSpec((None, None, seq_len, PAIR_W), lambda p, b: (p, b, 0, 0))
    view = lambda a: a.reshape(N_PAIRS, batch, seq_len, PAIR_W)
    seq_scratch = lambda n, dt: pltpu.VMEM((n, seq_len, PAIR_W), dt)
    out = pl.pallas_call(
        functools.partial(_dilated_kernel, seq_len=seq_len),
        grid=(N_PAIRS, batch),
        in_specs=[seq_block, seq_block, seq_block,
                  pl.BlockSpec((n_pat, 2, Q_BLOCK, 2 * Q_BLOCK), lambda p, b: (0, p, 0, 0))],
        out_specs=seq_block,
        out_shape=jax.ShapeDtypeStruct((N_PAIRS, batch, seq_len, PAIR_W), BF16),
        scratch_shapes=[seq_scratch(n_pat, F32), seq_scratch(n_pat, F32),
                        pltpu.VMEM((2, seq_len, 2 * Q_BLOCK), F32),
                        seq_scratch(3, F32), seq_scratch(3, F32), seq_scratch(3, BF16), seq_scratch(3, BF16)],
        compiler_params=pltpu.CompilerParams(dimension_semantics=("arbitrary", "arbitrary"),
                                             vmem_limit_bytes=VMEM_LIMIT_BYTES),
        name="dilated",
    )(view(q), view(k), view(v), bias)
    return out.reshape(N_PAIRS, batch * seq_len, PAIR_W)


def _flash_kernel(*refs, kw, decay, seq_len):
    if decay:
        q_ref, k_ref, v_ref, crow_ref, ccol_ref, o_ref, s_scr = refs
    else:
        q_ref, k_ref, v_ref, o_ref, s_scr = refs
    hp = pl.program_id(0)
    t = FLASH_T
    lane = lax.broadcasted_iota(jnp.int32, (1, kw), 1)
    in_h0 = (lane < HEAD_DIM) | ((lane >= PAIR_W) & (lane < PAIR_W + ROPE))
    in_h1 = ((lane >= HEAD_DIM) & (lane < PAIR_W)) | ((lane >= PAIR_W + ROPE) & (lane < PAIR_W + 2 * ROPE))
    row = lax.broadcasted_iota(jnp.int32, (t, t), 0)
    col = lax.broadcasted_iota(jnp.int32, (t, t), 1)
    causal = col <= row
    out_lane = lax.broadcasted_iota(jnp.int32, (1, PAIR_W), 1)
    for qi in range(seq_len // t):
        rows = slice(qi * t, (qi + 1) * t)
        q = q_ref[rows, :]
        outs = []
        for j, sel in enumerate((in_h0, in_h1)):
            qh = jnp.where(sel, q, jnp.zeros_like(q))
            mx = None
            for c in range(qi + 1):
                cols = slice(c * t, (c + 1) * t)
                s = _nt_dot(qh, k_ref[cols, :])
                if decay:
                    s = s - crow_ref[pl.ds(2 * hp + j, 1), cols]
                if c == qi:
                    s = jnp.where(causal, s, -jnp.inf)
                s_scr[j, :, cols] = s
                mx = s if mx is None else jnp.maximum(mx, s)
            shift = jnp.max(mx, axis=-1, keepdims=True)
            if decay:
                head = (2 * hp + j).astype(F32)
                cc = ccol_ref[rows, :]
                lane_c = lax.broadcasted_iota(jnp.int32, cc.shape, 1).astype(F32)
                cq = jnp.sum(jnp.where(lane_c == head, cc, 0.0), axis=-1, keepdims=True)
                shift = (shift + cq) - cq
            acc = None
            lsum = None
            for c in range(qi + 1):
                cols = slice(c * t, (c + 1) * t)
                p = jnp.exp(s_scr[j, :, cols] - shift)
                pv = _dot(p.astype(BF16), v_ref[cols, :])
                lsum = p if lsum is None else lsum + p
                acc = pv if acc is None else acc + pv
            outs.append(acc / jnp.sum(lsum, axis=-1, keepdims=True))
        o_ref[rows, :] = jnp.where(out_lane < HEAD_DIM, outs[0], outs[1]).astype(BF16)


def _flash_call(q, k, v, batch, seq_len, crow=None, ccol=None):
    kw = q.shape[-1]
    decay = crow is not None
    view = lambda a: a.reshape(N_PAIRS, batch, seq_len, a.shape[-1])
    args = [view(q), view(k), view(v)]
    seq_block = lambda w: pl.BlockSpec((None, None, seq_len, w), lambda p, b: (p, b, 0, 0))
    in_specs = [seq_block(kw), seq_block(kw), seq_block(PAIR_W)]
    if decay:
        args += [crow, ccol]
        in_specs += [pl.BlockSpec((HF_ROWS, seq_len), lambda p, b: (0, b)),
                     pl.BlockSpec((seq_len, LANES), lambda p, b: (b, 0))]
    out = pl.pallas_call(
        functools.partial(_flash_kernel, kw=kw, decay=decay, seq_len=seq_len),
        grid=(N_PAIRS, batch),
        in_specs=in_specs,
        out_specs=seq_block(PAIR_W),
        out_shape=jax.ShapeDtypeStruct((N_PAIRS, batch, seq_len, PAIR_W), BF16),
        scratch_shapes=[pltpu.VMEM((2, FLASH_T, seq_len), F32)],
        compiler_params=pltpu.CompilerParams(dimension_semantics=("arbitrary",) * 2,
                                             vmem_limit_bytes=VMEM_LIMIT_BYTES),
        name="flash_fox" if decay else "flash_mla",
    )(*args)
    return out.reshape(N_PAIRS, batch * seq_len, PAIR_W)


def _merge_kernel(x_ref, oa_ref, ob_ref, oc_ref, gmix_ref, wg_ref, wb_ref, wo_ref, gffn_ref,
                  wr_ref, br_ref, x1_ref, gates_ref):
    x = x_ref[...]
    ms = jnp.mean(x * x, axis=-1, keepdims=True)
    xn = (x * lax.rsqrt(ms + EPS) * gmix_ref[...]).astype(BF16)
    mixed = None
    for g, o_ref in enumerate((oa_ref, ob_ref, oc_ref)):
        o = jnp.concatenate([o_ref[p] for p in range(N_PAIRS)], axis=-1)
        gate = jax.nn.sigmoid(_dot(xn, wg_ref[:, g * D_MODEL:(g + 1) * D_MODEL]))
        term = gate * _dot(o, wb_ref[g])
        mixed = term if mixed is None else mixed + term
    x1 = x + _dot(mixed.astype(BF16), wo_ref[...])
    x1_ref[...] = x1

    ms1 = jnp.mean(x1 * x1, axis=-1, keepdims=True)
    xf = x1 * lax.rsqrt(ms1 + EPS) * gffn_ref[...]
    hi, lo = _split_hi_lo(xf)
    tm = xf.shape[0]
    parts = _dot(jnp.concatenate([hi, lo], axis=0), wr_ref[...])
    lg = (parts[0:tm, 0:LANES] + parts[0:tm, LANES:2 * LANES]
          + parts[tm:2 * tm, 0:LANES] + parts[tm:2 * tm, LANES:2 * LANES]) + br_ref[...]
    lane = lax.broadcasted_iota(jnp.int32, lg.shape, 1)
    lane_f = lane.astype(F32)
    neg = -jnp.inf
    far = float(LANES)
    gl = jnp.where((lane >= N_EXPERTS) & (lane < N_EXPERTS + N_GROUPS), lg, neg)
    gmax = jnp.max(gl, axis=-1, keepdims=True)
    pg_top = 1.0 / jnp.sum(jnp.exp(gl - gmax), axis=-1, keepdims=True)
    gidx = jnp.min(jnp.where(gl == gmax, lane_f, far), axis=-1, keepdims=True) - float(N_EXPERTS)
    in_group = (lane < N_EXPERTS) & ((lane // EXPERTS_PER_GROUP).astype(F32) == gidx)
    ev = jnp.where(in_group, lg, neg)
    v1 = jnp.max(ev, axis=-1, keepdims=True)
    i1 = jnp.min(jnp.where(ev == v1, lane_f, far), axis=-1, keepdims=True)
    ev2 = jnp.where(lane_f == i1, neg, ev)
    v2 = jnp.max(ev2, axis=-1, keepdims=True)
    i2 = jnp.min(jnp.where(ev2 == v2, lane_f, far), axis=-1, keepdims=True)
    e2 = jnp.exp(v2 - v1)
    den = 1.0 + e2
    w1 = (1.0 / den) * pg_top
    w2 = (e2 / den) * pg_top
    gates_ref[...] = jnp.where(lane_f == i1, w1, jnp.where(lane_f == i2, w2,
                                                           jnp.where(lane == GROUP_LANE, gidx, 0.0)))


def _merge_call(x2d, oa, ob, oc, lw):
    t = x2d.shape[0]
    tm = MERGE_TM
    const = lambda shape: pl.BlockSpec(shape, lambda i: (0,) * len(shape))
    row_tile = lambda w: pl.BlockSpec((tm, w), lambda i: (i, 0))
    pair_in = pl.BlockSpec((N_PAIRS, tm, PAIR_W), lambda i: (0, i, 0))
    return pl.pallas_call(
        _merge_kernel,
        grid=(t // tm,),
        in_specs=[row_tile(D_MODEL), pair_in, pair_in, pair_in, const((1, D_MODEL)),
                  const((D_MODEL, 3 * D_MODEL)), const((3, 512, D_MODEL)), const((D_MODEL, D_MODEL)),
                  const((1, D_MODEL)), const((D_MODEL, 2 * LANES)), const((1, LANES))],
        out_specs=[row_tile(D_MODEL), row_tile(LANES)],
        out_shape=[jax.ShapeDtypeStruct((t, D_MODEL), F32), jax.ShapeDtypeStruct((t, LANES), F32)],
        compiler_params=pltpu.CompilerParams(dimension_semantics=("arbitrary",),
                                             vmem_limit_bytes=VMEM_LIMIT_BYTES),
        name="merge",
    )(x2d, oa, ob, oc, lw["gmix"], lw["wg"], lw["wb"], lw["wo"], lw["gffn"], lw["wr"], lw["br"])


def _moe_kernel(x_ref, gates_ref, gffn_ref, lstrict_ref, ustrict_ref, wgu_ref, wd_ref, o_ref,
                xs_ref, gs_ref, ys_ref):
    x = x_ref[...]
    tm = x.shape[0]
    ms = jnp.mean(x * x, axis=-1, keepdims=True)
    xn = (x * lax.rsqrt(ms + EPS) * gffn_ref[...]).astype(BF16)

    gates = gates_ref[...]
    lane = lax.broadcasted_iota(jnp.int32, gates.shape, 1)
    gid = jnp.sum(jnp.where(lane == GROUP_LANE, gates, 0.0), axis=-1, keepdims=True)
    onehot = jnp.where((lane.astype(F32) == gid) & (lane < N_GROUPS), 1.0, 0.0)
    before = _dot(lstrict_ref[...], onehot.astype(BF16))
    counts = before[tm - 1:tm, :] + onehot[tm - 1:tm, :]
    n_chunks = jnp.floor((counts + (MOE_CHUNK - 1.0)) * (1.0 / MOE_CHUNK))
    seg_start = _dot(jnp.broadcast_to(n_chunks, (8, LANES)).astype(BF16), ustrict_ref[...])[0:1, :] * MOE_CHUNK
    rank = jnp.sum(onehot * (before + seg_start), axis=-1, keepdims=True)
    rank_row = jnp.transpose(jnp.broadcast_to(rank, (tm, LANES)))[0:1, :]
    slot_col = lax.broadcasted_iota(jnp.int32, (MOE_ROWS, 1), 0).astype(F32)
    slot_row = lax.broadcasted_iota(jnp.int32, (1, MOE_ROWS), 1).astype(F32)
    perm = jnp.where(slot_col == rank_row, 1.0, 0.0).astype(BF16)
    perm_t = jnp.where(rank == slot_row, 1.0, 0.0).astype(BF16)

    xs_ref[...] = _dot(perm, xn).astype(BF16)
    g_hi, g_lo = _split_hi_lo(gates)
    g_sorted = _dot(perm, jnp.concatenate([g_hi, g_lo], axis=-1))
    gs_ref[...] = g_sorted[:, 0:LANES] + g_sorted[:, LANES:2 * LANES]
    ys_ref[...] = jnp.zeros_like(ys_ref)

    nc = [n_chunks[0, g].astype(jnp.int32) for g in range(N_GROUPS)]
    ends = [nc[0], nc[0] + nc[1], nc[0] + nc[1] + nc[2]]
    total = ends[2] + nc[3]
    ff_all = EXPERTS_PER_GROUP * D_FF

    def chunk(k, carry):
        g = sum((k >= e).astype(jnp.int32) for e in ends)
        r0 = pl.multiple_of(k * MOE_CHUNK, MOE_CHUNK)
        xc = xs_ref[pl.ds(r0, MOE_CHUNK), :]
        gc = gs_ref[pl.ds(r0, MOE_CHUNK), :]
        lane_c = lax.broadcasted_iota(jnp.int32, gc.shape, 1)
        parts = []
        for e in range(EXPERTS_PER_GROUP):
            h_gate = _dot(xc, wgu_ref[g, :, e * D_FF:(e + 1) * D_FF])
            h_up = _dot(xc, wgu_ref[g, :, ff_all + e * D_FF:ff_all + (e + 1) * D_FF])
            gate = jnp.sum(jnp.where(lane_c == g * EXPERTS_PER_GROUP + e, gc, 0.0), axis=-1, keepdims=True)
            parts.append((jax.nn.silu(h_gate) * h_up * gate).astype(BF16))
        y = _dot(jnp.concatenate(parts, axis=-1), wd_ref[g])
        ys_ref[pl.ds(r0, MOE_CHUNK), :] = y.astype(BF16)
        return carry

    lax.fori_loop(0, total, chunk, 0)
    o_ref[...] = x + _dot(perm_t, ys_ref[...])


def _moe_call(x1, gates, cw, lw):
    t = x1.shape[0]
    tm = MOE_TM
    ff_all = EXPERTS_PER_GROUP * D_FF
    const = lambda shape: pl.BlockSpec(shape, lambda i: (0,) * len(shape))
    resident = lambda shape: pl.BlockSpec(shape, lambda i: (0,) * len(shape), pipeline_mode=pl.Buffered(1))
    return pl.pallas_call(
        _moe_kernel,
        grid=(t // tm,),
        in_specs=[pl.BlockSpec((tm, D_MODEL), lambda i: (i, 0)),
                  pl.BlockSpec((tm, LANES), lambda i: (i, 0)),
                  const((1, D_MODEL)), const((tm, tm)), const((LANES, LANES)),
                  resident((N_GROUPS, D_MODEL, 2 * ff_all)),
                  resident((N_GROUPS, ff_all, D_MODEL))],
        out_specs=pl.BlockSpec((tm, D_MODEL), lambda i: (i, 0)),
        out_shape=jax.ShapeDtypeStruct((t, D_MODEL), F32),
        scratch_shapes=[pltpu.VMEM((MOE_ROWS, D_MODEL), BF16), pltpu.VMEM((MOE_ROWS, LANES), F32),
                        pltpu.VMEM((MOE_ROWS, D_MODEL), BF16)],
        compiler_params=pltpu.CompilerParams(dimension_semantics=("arbitrary",),
                                             vmem_limit_bytes=VMEM_LIMIT_BYTES),
        name="moe",
    )(x1, gates, lw["gffn"], cw["lstrict"], cw["ustrict"], lw["wgu"], lw["wd"])


def _t5_bucket(dist):
    max_exact = N_BUCKETS // 2
    log_ratio = np.log(np.maximum(dist, max_exact) / max_exact) / np.log(MAX_DISTANCE / max_exact)
    large = np.minimum(max_exact + (log_ratio * (N_BUCKETS - max_exact)).astype(np.int32), N_BUCKETS - 1)
    return np.where(dist < max_exact, dist, large).astype(np.int32)


def _dilated_bias(rel_bias):
    tables = []
    span = 3 * Q_BLOCK
    for window, dil in DILATED_PAIRS:
        assert window // dil == Q_BLOCK
        per_dist = rel_bias[_t5_bucket(np.arange(Q_BLOCK + 1) * dil)].astype(F32).T
        diag = jnp.concatenate([jnp.full((N_HEADS, Q_BLOCK), -jnp.inf, F32), per_dist[:, ::-1],
                                jnp.full((N_HEADS, span - 2 * Q_BLOCK), -jnp.inf, F32)], axis=1)
        skew = jnp.tile(diag, (1, Q_BLOCK))[:, :Q_BLOCK * span].reshape(N_HEADS, Q_BLOCK, span)
        tables.append(skew[:, :, Q_BLOCK:])
    return jnp.stack(tables, axis=0)


def _rope_a(v1, v2):
    pad = jnp.zeros(v1.shape[:-1] + (LANES - 4 * ROPE_HALF,), v1.dtype)
    return jnp.concatenate([v1, v2, v1, v2, pad], axis=-1)


def _const_weights(positions, rel_bias):
    b, s = positions.shape
    inv_freq = ROPE_THETA ** (-jnp.arange(ROPE_HALF, dtype=F32) / ROPE_HALF)
    ang = positions.astype(F32).reshape(b * s, 1) * inv_freq
    cos, sin = jnp.cos(ang), jnp.sin(ang)
    idx = np.arange(512)
    return {
        "ca": _rope_a(cos, cos),
        "sb": _rope_a(-sin, sin),
        "g64": jnp.asarray(idx[:, None] // HEAD_DIM == idx[None, :] // HEAD_DIM, BF16),
        "g32": jnp.asarray(idx[:, None] // ROPE == idx[None, :] // ROPE, BF16),
        "tri": jnp.asarray(np.arange(FRONT_TM)[None, :] <= np.arange(FRONT_TM)[:, None], BF16),
        "lstrict": jnp.asarray(np.arange(MOE_TM)[None, :] < np.arange(MOE_TM)[:, None], BF16),
        "ustrict": jnp.asarray(np.arange(LANES)[:, None] < np.arange(LANES)[None, :], BF16),
        "bias": _dilated_bias(rel_bias),
    }


def _group_cols(w):
    w = w.reshape(N_GROUPS, EXPERTS_PER_GROUP, D_MODEL, D_FF)
    return w.transpose(0, 2, 1, 3).reshape(N_GROUPS, D_MODEL, EXPERTS_PER_GROUP * D_FF)


def _layer_weights(l, p):
    w_in = p["w_in"][l]
    o_hf = 3072
    o_cq = o_hf + N_HEADS
    o_ckv = o_cq + Q_LORA
    o_kr = o_ckv + KV_LORA
    o_g = o_kr + ROPE
    kr1 = w_in[:, o_kr:o_kr + ROPE_HALF]
    kr2 = w_in[:, o_kr + ROPE_HALF:o_kr + ROPE]
    hf_w = w_in[:, o_hf:o_cq]
    wf = jnp.concatenate([
        w_in[:, 0:3072], w_in[:, o_cq:o_ckv], w_in[:, o_ckv:o_kr],
        _rope_a(kr1, kr2), _rope_a(kr2, kr1),
        jnp.pad(hf_w, ((0, 0), (0, LANES - N_HEADS))),
    ], axis=1).astype(BF16)
    wft = jnp.pad(hf_w.T, ((0, HF_ROWS - N_HEADS), (0, 0))).astype(BF16)

    wuq = p["w_uq"][l].reshape(Q_LORA, N_HEADS, NOPE + ROPE)
    q_nope = wuq[:, :, :NOPE].reshape(Q_LORA, N_HEADS * NOPE)
    q1 = wuq[:, :, NOPE:NOPE + ROPE_HALF]
    q2 = wuq[:, :, NOPE + ROPE_HALF:]

    def pair_rope(v1, v2):
        x = jnp.concatenate([v1, v2], axis=-1).reshape(Q_LORA, N_PAIRS, 2 * ROPE)
        return jnp.pad(x, ((0, 0), (0, 0), (0, LANES - 2 * ROPE))).reshape(Q_LORA, N_PAIRS * LANES)

    wuq_p = jnp.concatenate([q_nope, pair_rope(q1, q2), pair_rope(q2, q1)], axis=1).astype(BF16)
    wukv = p["w_ukv"][l].reshape(KV_LORA, N_HEADS, NOPE + HEAD_DIM)
    wukv_p = jnp.concatenate([wukv[:, :, :NOPE].reshape(KV_LORA, 512),
                              wukv[:, :, NOPE:].reshape(KV_LORA, 512)], axis=1).astype(BF16)

    tile8 = lambda g: jnp.tile(g, N_HEADS)
    sc_ab = 1.0 / math.sqrt(HEAD_DIM)
    sc_c = 1.0 / math.sqrt(NOPE + ROPE)
    gq_c, gk_c = p["gq_c"][l], p["gk_c"][l]
    zeros512 = jnp.zeros((512,), F32)
    g512 = jnp.stack([tile8(p["gq_a"][l]) * sc_ab, tile8(p["gk_a"][l]),
                      tile8(p["gq_b"][l]) * sc_ab, tile8(p["gk_b"][l]),
                      tile8(gq_c[:NOPE]) * sc_c, tile8(gk_c[:NOPE]), zeros512, zeros512])
    gq1, gq2 = gq_c[NOPE:NOPE + ROPE_HALF] * sc_c, gq_c[NOPE + ROPE_HALF:] * sc_c
    gk1, gk2 = gk_c[NOPE:NOPE + ROPE_HALF], gk_c[NOPE + ROPE_HALF:]
    zeros128 = jnp.zeros((LANES,), F32)
    bf = p["b_forget"][l].astype(F32)
    g128 = jnp.stack([_rope_a(gq1, gq2), _rope_a(gq2, gq1), _rope_a(gk1, gk2), _rope_a(gk2, gk1),
                      jnp.pad(bf, (0, LANES - N_HEADS)), zeros128, zeros128, zeros128])
    bfrow = jnp.broadcast_to(jnp.pad(bf, (0, HF_ROWS - N_HEADS))[:, None], (HF_ROWS, LANES))

    wr = jnp.concatenate([p["w_router_expert"][l], p["w_router_group"][l]], axis=1)
    wr = jnp.pad(wr, ((0, 0), (0, LANES - N_EXPERTS - N_GROUPS))).astype(F32)
    wrh = wr.astype(BF16)
    wrl = (wr - wrh.astype(F32)).astype(BF16)
    br = jnp.pad(jnp.concatenate([p["b_router_expert"][l], p["b_router_group"][l]]).astype(F32),
                 (0, LANES - N_EXPERTS - N_GROUPS))[None, :]
    return {
        "gmix": p["norm_mix"][l][None, :], "wf": wf, "wft": wft, "wuq": wuq_p, "wukv": wukv_p,
        "g512": g512, "g128": g128, "ncq": p["norm_cq"][l][None, :], "nckv": p["norm_ckv"][l][None, :],
        "bfrow": bfrow,
        "wg": w_in[:, o_g:].astype(BF16), "wb": p["w_branch"][l].astype(BF16), "wo": p["w_out"][l].astype(BF16),
        "gffn": p["norm_ffn"][l][None, :], "wr": jnp.concatenate([wrh, wrl], axis=1), "br": br,
        "wgu": jnp.concatenate([_group_cols(p["w_expert_gate"][l]), _group_cols(p["w_expert_up"][l])],
                               axis=-1).astype(BF16),
        "wd": p["w_expert_down"][l].reshape(N_GROUPS, EXPERTS_PER_GROUP * D_FF, D_MODEL).astype(BF16),
    }


def kernel(x, positions, rel_bias, norm_mix, w_in, b_forget, gq_a, gk_a, gq_b, gk_b, gq_c, gk_c, norm_cq, norm_ckv, w_uq, w_ukv, w_branch, w_out, norm_ffn, w_router_group, b_router_group, w_router_expert, b_router_expert, w_expert_gate, w_expert_up, w_expert_down):
    batch, seq_len, d_model = x.shape
    assert d_model == D_MODEL and seq_len % (Q_BLOCK * DILATED_PAIRS[-1][1]) == 0
    assert seq_len == DILATED_PAIRS[-1][0], "the widest dilated window is assumed to span the sequence"
    p = dict(norm_mix=norm_mix, w_in=w_in, b_forget=b_forget, gq_a=gq_a, gk_a=gk_a, gq_b=gq_b, gk_b=gk_b,
             gq_c=gq_c, gk_c=gk_c, norm_cq=norm_cq, norm_ckv=norm_ckv, w_uq=w_uq, w_ukv=w_ukv,
             w_branch=w_branch, w_out=w_out, norm_ffn=norm_ffn, w_router_group=w_router_group,
             b_router_group=b_router_group, w_router_expert=w_router_expert, b_router_expert=b_router_expert,
             w_expert_gate=w_expert_gate, w_expert_up=w_expert_up, w_expert_down=w_expert_down)
    cw = _const_weights(positions, rel_bias)
    xs = x.reshape(batch * seq_len, d_model)
    for l in range(norm_mix.shape[0]):
        lw = _layer_weights(l, p)
        qa, ka, va, qb, kb, vb, qc, kc, vc, ccol, crow = _front_call(xs, cw, lw, seq_len)
        oa = _dilated_call(qa, ka, va, cw["bias"], batch, seq_len)
        ob = _flash_call(qb, kb, vb, batch, seq_len, crow, ccol)
        oc = _flash_call(qc, kc, vc, batch, seq_len)
        x1, gates = _merge_call(xs, oa, ob, oc, lw)
        xs = _moe_call(x1, gates, cw, lw)
    return xs.reshape(batch, seq_len, d_model)
```

```python
import functools
import math

import numpy as np
import jax
import jax.numpy as jnp
from jax import lax
from jax.experimental import pallas as pl
from jax.experimental.pallas import tpu as pltpu

F32 = jnp.float32
BF16 = jnp.bfloat16

D_MODEL = 1024
N_HEADS = 8
HEAD_DIM = 64
N_PAIRS = N_HEADS // 2
PAIR_W = 2 * HEAD_DIM
DILATED_PAIRS = ((128, 1), (512, 4), (2048, 16))
Q_LORA = 256
KV_LORA = 128
NOPE = 64
ROPE = 32
ROPE_HALF = ROPE // 2
ROPE_THETA = 10000.0
N_BUCKETS = 32
MAX_DISTANCE = 2048
Q_BLOCK = 128
N_GROUPS = 4
EXPERTS_PER_GROUP = 4
N_EXPERTS = 16
D_FF = 256
EPS = 1e-6
LOG2E = math.log2(math.e)
LANES = 128

VMEM_LIMIT_BYTES = 56 * 1024 * 1024

FRONT_TM = 512
MERGE_TM = 512
MOE_TM = 512
MOE_CHUNK = 128
MOE_ROWS = MOE_TM + (N_GROUPS - 1) * MOE_CHUNK
GROUP_LANE = N_EXPERTS
FLASH_T = 256

C_A = 0
C_B = 1536
C_CQ = 3072
C_CKV = C_CQ + Q_LORA
C_KRA = C_CKV + KV_LORA
C_KRB = C_KRA + LANES
C_HF = C_KRB + LANES
N_FRONT = C_HF + LANES
HF_ROWS = 16


def _nt_dot(a, b):
    return lax.dot_general(a, b, (((1,), (1,)), ((), ())), preferred_element_type=F32)


def _dot(a, b):
    return jnp.dot(a, b, preferred_element_type=F32)


def _log_sigmoid(x):
    return jnp.minimum(x, 0.0) - jnp.log1p(jnp.exp(-jnp.abs(x)))


def _split_hi_lo(x):
    hi = x.astype(BF16)
    lo = (x - hi.astype(F32)).astype(BF16)
    return hi, lo


def _front_kernel(x_ref, gmix_ref, wf_ref, wft_ref, wuq_ref, wukv_ref, g64_ref, g32_ref, tri_ref,
                  g512_ref, g128_ref, ncq_ref, nckv_ref, bfrow_ref, ca_ref, sb_ref,
                  qa_ref, ka_ref, va_ref, qb_ref, kb_ref, vb_ref, qc_ref, kc_ref, vc_ref,
                  ccol_ref, crow_ref, carry_col, carry_row, *, tiles_per_seq):
    i = pl.program_id(0)

    @pl.when(i % tiles_per_seq == 0)
    def _():
        carry_col[...] = jnp.zeros_like(carry_col)
        carry_row[...] = jnp.zeros_like(carry_row)

    x = x_ref[...]
    ms = jnp.mean(x * x, axis=-1, keepdims=True)
    xn = (x * lax.rsqrt(ms + EPS) * gmix_ref[...]).astype(BF16)

    def group_norm(h, gmat, n, gain):
        ss = _dot((h * h).astype(BF16), gmat)
        return h * lax.rsqrt(ss * (1.0 / n) + EPS) * gain

    def store_pairs(ref, val, lo=0, w=PAIR_W):
        for p in range(N_PAIRS):
            ref[p, :, lo:lo + w] = val[:, p * PAIR_W:(p + 1) * PAIR_W].astype(BF16)

    g64 = g64_ref[...]
    for base, q_ref, k_ref, v_ref, row in ((C_A, qa_ref, ka_ref, va_ref, 0), (C_B, qb_ref, kb_ref, vb_ref, 2)):
        hq = _dot(xn, wf_ref[:, base:base + 512])
        store_pairs(q_ref, group_norm(hq, g64, HEAD_DIM, g512_ref[row:row + 1, :]))
        hk = _dot(xn, wf_ref[:, base + 512:base + 1024])
        store_pairs(k_ref, group_norm(hk, g64, HEAD_DIM, g512_ref[row + 1:row + 2, :]))
        hv = _dot(xn, wf_ref[:, base + 1024:base + 1536])
        store_pairs(v_ref, hv)

    hs = _dot(xn, wf_ref[:, C_CQ:N_FRONT])
    ca = ca_ref[...]
    sb = sb_ref[...]

    hcq = hs[:, 0:Q_LORA]
    cq = (hcq * lax.rsqrt(jnp.mean(hcq * hcq, axis=-1, keepdims=True) + EPS) * ncq_ref[...]).astype(BF16)
    qc = _dot(cq, wuq_ref[...])
    qn = group_norm(qc[:, 0:512], g64, NOPE, g512_ref[4:5, :])
    store_pairs(qc_ref, qn, 0)
    qra = qc[:, 512:1024]
    qrb = qc[:, 1024:1536]
    rs = lax.rsqrt(_dot((qra * qra).astype(BF16), g32_ref[...]) * (1.0 / ROPE) + EPS)
    ga = g128_ref[0:1, :]
    gb = g128_ref[1:2, :]
    for p in range(N_PAIRS):
        sl = slice(p * PAIR_W, (p + 1) * PAIR_W)
        qr = (qra[:, sl] * ga * ca + qrb[:, sl] * gb * sb) * rs[:, sl]
        qc_ref[p, :, PAIR_W:2 * PAIR_W] = qr.astype(BF16)

    hckv = hs[:, Q_LORA:Q_LORA + KV_LORA]
    ckv = (hckv * lax.rsqrt(jnp.mean(hckv * hckv, axis=-1, keepdims=True) + EPS) * nckv_ref[...]).astype(BF16)
    kv = _dot(ckv, wukv_ref[...])
    store_pairs(kc_ref, group_norm(kv[:, 0:512], g64, NOPE, g512_ref[5:6, :]), 0)
    store_pairs(vc_ref, kv[:, 512:1024])
    kra = hs[:, C_KRA - C_CQ:C_KRA - C_CQ + LANES]
    krb = hs[:, C_KRB - C_CQ:C_KRB - C_CQ + LANES]
    rsk = lax.rsqrt(_dot((kra * kra).astype(BF16), g32_ref[0:LANES, 0:LANES]) * (1.0 / ROPE) + EPS)
    kr = ((kra * g128_ref[2:3, :] * ca + krb * g128_ref[3:4, :] * sb) * rsk).astype(BF16)
    for p in range(N_PAIRS):
        kc_ref[p, :, PAIR_W:2 * PAIR_W] = kr

    tri = tri_ref[...]
    hf_col = hs[:, C_HF - C_CQ:C_HF - C_CQ + LANES] + g128_ref[4:5, :]
    hi, lo = _split_hi_lo(_log_sigmoid(hf_col))
    ccol = _dot(tri, hi) + _dot(tri, lo) + carry_col[0:1, :]
    ccol_ref[...] = ccol
    tm = ccol.shape[0]
    carry_col[...] = jnp.broadcast_to(ccol[tm - 1:tm, :], carry_col.shape)

    hf_row = _nt_dot(wft_ref[...], xn) + bfrow_ref[:, 0:1]
    hi, lo = _split_hi_lo(_log_sigmoid(hf_row))
    crow = _nt_dot(hi, tri) + _nt_dot(lo, tri) + carry_row[:, 0:1]
    crow_ref[...] = crow
    carry_row[...] = jnp.broadcast_to(crow[:, tm - 1:tm], carry_row.shape)


def _front_call(x2d, cw, lw, seq_len):
    t = x2d.shape[0]
    tm = FRONT_TM
    const = lambda shape: pl.BlockSpec(shape, lambda i: (0,) * len(shape))
    pair_out = lambda w: pl.BlockSpec((N_PAIRS, tm, w), lambda i: (0, i, 0))
    pair_shape = lambda w: jax.ShapeDtypeStruct((N_PAIRS, t, w), BF16)
    in_specs = [
        pl.BlockSpec((tm, D_MODEL), lambda i: (i, 0)),
        const((1, D_MODEL)),
        const((D_MODEL, N_FRONT)),
        const((HF_ROWS, D_MODEL)),
        const((Q_LORA, 1536)),
        const((KV_LORA, 1024)),
        const((512, 512)),
        const((512, 512)),
        const((tm, tm)),
        const((8, 512)),
        const((8, LANES)),
        const((1, Q_LORA)),
        const((1, KV_LORA)),
        const((HF_ROWS, LANES)),
        pl.BlockSpec((tm, LANES), lambda i: (i, 0)),
        pl.BlockSpec((tm, LANES), lambda i: (i, 0)),
    ]
    out_specs = [pair_out(PAIR_W)] * 6 + [pair_out(2 * PAIR_W), pair_out(2 * PAIR_W), pair_out(PAIR_W),
                                           pl.BlockSpec((tm, LANES), lambda i: (i, 0)),
                                           pl.BlockSpec((HF_ROWS, tm), lambda i: (0, i))]
    out_shape = [pair_shape(PAIR_W)] * 6 + [pair_shape(2 * PAIR_W), pair_shape(2 * PAIR_W), pair_shape(PAIR_W),
                                            jax.ShapeDtypeStruct((t, LANES), F32),
                                            jax.ShapeDtypeStruct((HF_ROWS, t), F32)]
    return pl.pallas_call(
        functools.partial(_front_kernel, tiles_per_seq=seq_len // tm),
        grid=(t // tm,),
        in_specs=in_specs,
        out_specs=out_specs,
        out_shape=out_shape,
        scratch_shapes=[pltpu.VMEM((8, LANES), F32), pltpu.VMEM((HF_ROWS, LANES), F32)],
        compiler_params=pltpu.CompilerParams(dimension_semantics=("arbitrary",),
                                             vmem_limit_bytes=VMEM_LIMIT_BYTES),
        name="front",
    )(x2d, lw["gmix"], lw["wf"], lw["wft"], lw["wuq"], lw["wukv"], cw["g64"], cw["g32"], cw["tri"],
      lw["g512"], lw["g128"], lw["ncq"], lw["nckv"], lw["bfrow"], cw["ca"], cw["sb"])


def _dilated_kernel(q_ref, k_ref, v_ref, bias_ref, o_ref, oacc, lacc, s_scr, nat, res4, g4, g16, *, seq_len):
    lane = lax.broadcasted_iota(jnp.int32, (1, PAIR_W), 1)
    left = lane < HEAD_DIM
    d_mid, d_far = DILATED_PAIRS[1][1], DILATED_PAIRS[2][1]
    assert DILATED_PAIRS[0][1] == 1 and d_far == d_mid * d_mid
    n_mid, n_far = seq_len // d_mid, seq_len // d_far

    for ti, src in enumerate((q_ref, k_ref, v_ref)):
        nat[ti] = src[...].astype(F32)
    for ti in range(3):
        for c in range(d_mid):
            rows = nat[ti, pl.ds(c, n_mid, stride=d_mid), :]
            res4[ti, c * n_mid:(c + 1) * n_mid, :] = rows
            g4[ti, c * n_mid:(c + 1) * n_mid, :] = rows.astype(BF16)
    for ti in range(3):
        for c in range(d_far):
            rows = res4[ti, pl.ds((c % d_mid) * n_mid + c // d_mid, n_far, stride=d_mid), :]
            g16[ti, c * n_far:(c + 1) * n_far, :] = rows.astype(BF16)

    readers = (
        (1, lambda ti, lo, hi: (q_ref, k_ref, v_ref)[ti][lo:hi, :]),
        (d_mid, lambda ti, lo, hi: g4[ti, lo:hi, :]),
        (d_far, lambda ti, lo, hi: g16[ti, lo:hi, :]),
    )
    for di, (d, read) in enumerate(readers):
        n_sub = seq_len // d
        n_blocks = n_sub // Q_BLOCK
        blocks = [(c, nb) for c in range(d) for nb in range(n_blocks)]

        def window(ti, c, nb):
            k_lo = max(nb - 1, 0) * Q_BLOCK
            return read(ti, c * n_sub + k_lo, c * n_sub + (nb + 1) * Q_BLOCK)

        row_max = {}
        for bi, (c, nb) in enumerate(blocks):
            q = read(0, c * n_sub + nb * Q_BLOCK, c * n_sub + (nb + 1) * Q_BLOCK)
            kwin = window(1, c, nb)
            w = kwin.shape[0]
            for j in range(2):
                qh = jnp.where(left if j == 0 else jnp.logical_not(left), q, jnp.zeros_like(q))
                s = _nt_dot(qh, kwin) + bias_ref[di, j, :, 2 * Q_BLOCK - w:]
                s_scr[j, bi * Q_BLOCK:(bi + 1) * Q_BLOCK, 0:w] = s
                row_max[bi, j] = jnp.max(s, axis=-1, keepdims=True)
        for bi, (c, nb) in enumerate(blocks):
            vwin = window(2, c, nb)
            w = vwin.shape[0]
            outs, lses = [], []
            for j in range(2):
                m = row_max[bi, j]
                p = jnp.exp2(s_scr[j, bi * Q_BLOCK:(bi + 1) * Q_BLOCK, 0:w] - m)
                den = jnp.sum(p, axis=-1, keepdims=True)
                outs.append(_dot(p.astype(BF16), vwin) / den)
                lses.append(m + jnp.log2(den))
            start = nb * Q_BLOCK * d + c
            rows = pl.ds(start, Q_BLOCK) if d == 1 else pl.ds(start, Q_BLOCK, stride=d)
            oacc[di, rows, :] = jnp.where(left, outs[0], outs[1])
            lacc[di, rows, :] = jnp.where(left, lses[0], lses[1])

    l0, l1, l2 = lacc[0], lacc[1], lacc[2]
    m = jnp.maximum(jnp.maximum(l0, l1), l2)
    w0, w1, w2 = jnp.exp2(l0 - m), jnp.exp2(l1 - m), jnp.exp2(l2 - m)
    wsum = w0 + w1 + w2
    o = oacc[0] * (w0 / wsum) + oacc[1] * (w1 / wsum) + oacc[2] * (w2 / wsum)
    o_ref[...] = o.astype(BF16)


def _dilated_call(q, k, v, bias, batch, seq_len):
    n_pat = len(DILATED_PAIRS)
    seq_block = pl.BlockSpec((None, None, seq_len, PAIR_W), lambda p, b: (p, b, 0, 0))
    view = lambda a: a.reshape(N_PAIRS, batch, seq_len, PAIR_W)
    seq_scratch = lambda n, dt: pltpu.VMEM((n, seq_len, PAIR_W), dt)
    out = pl.pallas_call(
        functools.partial(_dilated_kernel, seq_len=seq_len),
        grid=(N_PAIRS, batch),
        in_specs=[seq_block, seq_block, seq_block,
                  pl.BlockSpec((n_pat, 2, Q_BLOCK, 2 * Q_BLOCK), lambda p, b: (0, p, 0, 0))],
        out_specs=seq_block,
        out_shape=jax.ShapeDtypeStruct((N_PAIRS, batch, seq_len, PAIR_W), BF16),
        scratch_shapes=[seq_scratch(n_pat, F32), seq_scratch(n_pat, F32),
                        pltpu.VMEM((2, seq_len, 2 * Q_BLOCK), F32),
                        seq_scratch(3, F32), seq_scratch(3, F32), seq_scratch(3, BF16), seq_scratch(3, BF16)],
        compiler_params=pltpu.CompilerParams(dimension_semantics=("arbitrary", "arbitrary"),
                                             vmem_limit_bytes=VMEM_LIMIT_BYTES),
        name="dilated",
    )(view(q), view(k), view(v), bias)
    return out.reshape(N_PAIRS, batch * seq_len, PAIR_W)


def _flash_kernel(*refs, kw, decay, seq_len):
    if decay:
        q_ref, k_ref, v_ref, crow_ref, ccol_ref, o_ref, s_scr, p_scr, v_scr = refs
    else:
        q_ref, k_ref, v_ref, o_ref, s_scr, p_scr, v_scr = refs
    hp = pl.program_id(0)
    t = FLASH_T
    lane = lax.broadcasted_iota(jnp.int32, (1, kw), 1)
    in_h0 = (lane < HEAD_DIM) | ((lane >= PAIR_W) & (lane < PAIR_W + ROPE))
    in_h1 = ((lane >= HEAD_DIM) & (lane < PAIR_W)) | ((lane >= PAIR_W + ROPE) & (lane < PAIR_W + 2 * ROPE))
    row = lax.broadcasted_iota(jnp.int32, (t, t), 0)
    col = lax.broadcasted_iota(jnp.int32, (t, t), 1)
    causal = col <= row
    out_lane = lax.broadcasted_iota(jnp.int32, (1, PAIR_W), 1)
    v_scr[:, 0:PAIR_W] = v_ref[...]
    v_scr[:, PAIR_W:2 * PAIR_W] = jnp.ones((seq_len, PAIR_W), BF16)
    for qi in range(seq_len // t):
        rows = slice(qi * t, (qi + 1) * t)
        q = q_ref[rows, :]
        width = (qi + 1) * t
        outs = []
        for j, sel in enumerate((in_h0, in_h1)):
            buf = 2 * (qi % 2) + j
            qh = jnp.where(sel, q, jnp.zeros_like(q))
            mx = None
            for c in range(qi + 1):
                cols = slice(c * t, (c + 1) * t)
                s = _nt_dot(qh, k_ref[cols, :])
                if decay:
                    s = s - crow_ref[pl.ds(2 * hp + j, 1), cols] * LOG2E
                if c == qi:
                    s = jnp.where(causal, s, -jnp.inf)
                s_scr[buf, :, cols] = s
                half = jnp.maximum(s[:, 0:t // 2], s[:, t // 2:t])
                mx = half if mx is None else jnp.maximum(mx, half)
            shift = jnp.max(mx, axis=-1, keepdims=True)
            if decay:
                head = (2 * hp + j).astype(F32)
                cc = ccol_ref[rows, :]
                lane_c = lax.broadcasted_iota(jnp.int32, cc.shape, 1).astype(F32)
                cq = jnp.sum(jnp.where(lane_c == head, cc, 0.0), axis=-1, keepdims=True) * LOG2E
                shift = (shift + cq) - cq
            for c in range(qi + 1):
                cols = slice(c * t, (c + 1) * t)
                p_scr[buf, :, cols] = jnp.exp2(s_scr[buf, :, cols] - shift).astype(BF16)
            res = _dot(p_scr[buf, :, 0:width], v_scr[0:width, :])
            outs.append(res[:, 0:PAIR_W] / res[:, PAIR_W:2 * PAIR_W])
        o_ref[rows, :] = jnp.where(out_lane < HEAD_DIM, outs[0], outs[1]).astype(BF16)


def _flash_call(q, k, v, batch, seq_len, crow=None, ccol=None):
    kw = q.shape[-1]
    decay = crow is not None
    view = lambda a: a.reshape(N_PAIRS, batch, seq_len, a.shape[-1])
    args = [view(q), view(k), view(v)]
    seq_block = lambda w: pl.BlockSpec((None, None, seq_len, w), lambda p, b: (p, b, 0, 0))
    in_specs = [seq_block(kw), seq_block(kw), seq_block(PAIR_W)]
    if decay:
        args += [crow, ccol]
        in_specs += [pl.BlockSpec((HF_ROWS, seq_len), lambda p, b: (0, b)),
                     pl.BlockSpec((seq_len, LANES), lambda p, b: (b, 0))]
    out = pl.pallas_call(
        functools.partial(_flash_kernel, kw=kw, decay=decay, seq_len=seq_len),
        grid=(N_PAIRS, batch),
        in_specs=in_specs,
        out_specs=seq_block(PAIR_W),
        out_shape=jax.ShapeDtypeStruct((N_PAIRS, batch, seq_len, PAIR_W), BF16),
        scratch_shapes=[pltpu.VMEM((4, FLASH_T, seq_len), F32), pltpu.VMEM((4, FLASH_T, seq_len), BF16),
                        pltpu.VMEM((seq_len, 2 * PAIR_W), BF16)],
        compiler_params=pltpu.CompilerParams(dimension_semantics=("arbitrary",) * 2,
                                             vmem_limit_bytes=VMEM_LIMIT_BYTES),
        name="flash_fox" if decay else "flash_mla",
    )(*args)
    return out.reshape(N_PAIRS, batch * seq_len, PAIR_W)


def _merge_kernel(x_ref, oa_ref, ob_ref, oc_ref, gmix_ref, wg_ref, wb_ref, wo_ref, gffn_ref,
                  wr_ref, br_ref, x1_ref, gates_ref):
    x = x_ref[...]
    ms = jnp.mean(x * x, axis=-1, keepdims=True)
    xn = (x * lax.rsqrt(ms + EPS) * gmix_ref[...]).astype(BF16)
    mixed = None
    for g, o_ref in enumerate((oa_ref, ob_ref, oc_ref)):
        o = jnp.concatenate([o_ref[p] for p in range(N_PAIRS)], axis=-1)
        gate = jax.nn.sigmoid(_dot(xn, wg_ref[:, g * D_MODEL:(g + 1) * D_MODEL]))
        term = gate * _dot(o, wb_ref[g])
        mixed = term if mixed is None else mixed + term
    x1 = x + _dot(mixed.astype(BF16), wo_ref[...])
    x1_ref[...] = x1

    ms1 = jnp.mean(x1 * x1, axis=-1, keepdims=True)
    xf = x1 * lax.rsqrt(ms1 + EPS) * gffn_ref[...]
    hi, lo = _split_hi_lo(xf)
    tm = xf.shape[0]
    parts = _dot(jnp.concatenate([hi, lo], axis=0), wr_ref[...])
    lg = (parts[0:tm, 0:LANES] + parts[0:tm, LANES:2 * LANES]
          + parts[tm:2 * tm, 0:LANES] + parts[tm:2 * tm, LANES:2 * LANES]) + br_ref[...]
    lane = lax.broadcasted_iota(jnp.int32, lg.shape, 1)
    lane_f = lane.astype(F32)
    neg = -jnp.inf
    far = float(LANES)
    gl = jnp.where((lane >= N_EXPERTS) & (lane < N_EXPERTS + N_GROUPS), lg, neg)
    gmax = jnp.max(gl, axis=-1, keepdims=True)
    pg_top = 1.0 / jnp.sum(jnp.exp(gl - gmax), axis=-1, keepdims=True)
    gidx = jnp.min(jnp.where(gl == gmax, lane_f, far), axis=-1, keepdims=True) - float(N_EXPERTS)
    in_group = (lane < N_EXPERTS) & ((lane // EXPERTS_PER_GROUP).astype(F32) == gidx)
    ev = jnp.where(in_group, lg, neg)
    v1 = jnp.max(ev, axis=-1, keepdims=True)
    i1 = jnp.min(jnp.where(ev == v1, lane_f, far), axis=-1, keepdims=True)
    ev2 = jnp.where(lane_f == i1, neg, ev)
    v2 = jnp.max(ev2, axis=-1, keepdims=True)
    i2 = jnp.min(jnp.where(ev2 == v2, lane_f, far), axis=-1, keepdims=True)
    e2 = jnp.exp(v2 - v1)
    den = 1.0 + e2
    w1 = (1.0 / den) * pg_top
    w2 = (e2 / den) * pg_top
    gates_ref[...] = jnp.where(lane_f == i1, w1, jnp.where(lane_f == i2, w2,
                                                           jnp.where(lane == GROUP_LANE, gidx, 0.0)))


def _merge_call(x2d, oa, ob, oc, lw):
    t = x2d.shape[0]
    tm = MERGE_TM
    const = lambda shape: pl.BlockSpec(shape, lambda i: (0,) * len(shape))
    row_tile = lambda w: pl.BlockSpec((tm, w), lambda i: (i, 0))
    pair_in = pl.BlockSpec((N_PAIRS, tm, PAIR_W), lambda i: (0, i, 0))
    return pl.pallas_call(
        _merge_kernel,
        grid=(t // tm,),
        in_specs=[row_tile(D_MODEL), pair_in, pair_in, pair_in, const((1, D_MODEL)),
                  const((D_MODEL, 3 * D_MODEL)), const((3, 512, D_MODEL)), const((D_MODEL, D_MODEL)),
                  const((1, D_MODEL)), const((D_MODEL, 2 * LANES)), const((1, LANES))],
        out_specs=[row_tile(D_MODEL), row_tile(LANES)],
        out_shape=[jax.ShapeDtypeStruct((t, D_MODEL), F32), jax.ShapeDtypeStruct((t, LANES), F32)],
        compiler_params=pltpu.CompilerParams(dimension_semantics=("arbitrary",),
                                             vmem_limit_bytes=VMEM_LIMIT_BYTES),
        name="merge",
    )(x2d, oa, ob, oc, lw["gmix"], lw["wg"], lw["wb"], lw["wo"], lw["gffn"], lw["wr"], lw["br"])


def _moe_kernel(x_ref, gates_ref, gffn_ref, lstrict_ref, ustrict_ref, wgu_ref, wd_ref, o_ref,
                xs_ref, gs_ref, ys_ref):
    x = x_ref[...]
    tm = x.shape[0]
    ms = jnp.mean(x * x, axis=-1, keepdims=True)
    xn = (x * lax.rsqrt(ms + EPS) * gffn_ref[...]).astype(BF16)

    gates = gates_ref[...]
    lane = lax.broadcasted_iota(jnp.int32, gates.shape, 1)
    gid = jnp.sum(jnp.where(lane == GROUP_LANE, gates, 0.0), axis=-1, keepdims=True)
    onehot = jnp.where((lane.astype(F32) == gid) & (lane < N_GROUPS), 1.0, 0.0)
    before = _dot(lstrict_ref[...], onehot.astype(BF16))
    counts = before[tm - 1:tm, :] + onehot[tm - 1:tm, :]
    n_chunks = jnp.floor((counts + (MOE_CHUNK - 1.0)) * (1.0 / MOE_CHUNK))
    seg_start = _dot(jnp.broadcast_to(n_chunks, (8, LANES)).astype(BF16), ustrict_ref[...])[0:1, :] * MOE_CHUNK
    rank = jnp.sum(onehot * (before + seg_start), axis=-1, keepdims=True)
    rank_row = jnp.transpose(jnp.broadcast_to(rank, (tm, LANES)))[0:1, :]
    slot_col = lax.broadcasted_iota(jnp.int32, (MOE_ROWS, 1), 0).astype(F32)
    slot_row = lax.broadcasted_iota(jnp.int32, (1, MOE_ROWS), 1).astype(F32)
    perm = jnp.where(slot_col == rank_row, 1.0, 0.0).astype(BF16)
    perm_t = jnp.where(rank == slot_row, 1.0, 0.0).astype(BF16)

    xs_ref[...] = _dot(perm, xn).astype(BF16)
    g_hi, g_lo = _split_hi_lo(gates)
    g_sorted = _dot(perm, jnp.concatenate([g_hi, g_lo], axis=-1))
    gs_ref[...] = g_sorted[:, 0:LANES] + g_sorted[:, LANES:2 * LANES]
    ys_ref[...] = jnp.zeros_like(ys_ref)

    nc = [n_chunks[0, g].astype(jnp.int32) for g in range(N_GROUPS)]
    ends = [nc[0], nc[0] + nc[1], nc[0] + nc[1] + nc[2]]
    total = ends[2] + nc[3]
    ff_all = EXPERTS_PER_GROUP * D_FF

    def chunk(k, carry):
        g = sum((k >= e).astype(jnp.int32) for e in ends)
        r0 = pl.multiple_of(k * MOE_CHUNK, MOE_CHUNK)
        xc = xs_ref[pl.ds(r0, MOE_CHUNK), :]
        gc = gs_ref[pl.ds(r0, MOE_CHUNK), :]
        lane_c = lax.broadcasted_iota(jnp.int32, gc.shape, 1)
        parts = []
        for e in range(EXPERTS_PER_GROUP):
            h_gate = _dot(xc, wgu_ref[g, :, e * D_FF:(e + 1) * D_FF])
            h_up = _dot(xc, wgu_ref[g, :, ff_all + e * D_FF:ff_all + (e + 1) * D_FF])
            gate = jnp.sum(jnp.where(lane_c == g * EXPERTS_PER_GROUP + e, gc, 0.0), axis=-1, keepdims=True)
            parts.append((jax.nn.silu(h_gate) * h_up * gate).astype(BF16))
        y = _dot(jnp.concatenate(parts, axis=-1), wd_ref[g])
        ys_ref[pl.ds(r0, MOE_CHUNK), :] = y.astype(BF16)
        return carry

    lax.fori_loop(0, total, chunk, 0)
    o_ref[...] = x + _dot(perm_t, ys_ref[...])


def _moe_call(x1, gates, cw, lw):
    t = x1.shape[0]
    tm = MOE_TM
    ff_all = EXPERTS_PER_GROUP * D_FF
    const = lambda shape: pl.BlockSpec(shape, lambda i: (0,) * len(shape))
    resident = lambda shape: pl.BlockSpec(shape, lambda i: (0,) * len(shape), pipeline_mode=pl.Buffered(1))
    return pl.pallas_call(
        _moe_kernel,
        grid=(t // tm,),
        in_specs=[pl.BlockSpec((tm, D_MODEL), lambda i: (i, 0)),
                  pl.BlockSpec((tm, LANES), lambda i: (i, 0)),
                  const((1, D_MODEL)), const((tm, tm)), const((LANES, LANES)),
                  resident((N_GROUPS, D_MODEL, 2 * ff_all)),
                  resident((N_GROUPS, ff_all, D_MODEL))],
        out_specs=pl.BlockSpec((tm, D_MODEL), lambda i: (i, 0)),
        out_shape=jax.ShapeDtypeStruct((t, D_MODEL), F32),
        scratch_shapes=[pltpu.VMEM((MOE_ROWS, D_MODEL), BF16), pltpu.VMEM((MOE_ROWS, LANES), F32),
                        pltpu.VMEM((MOE_ROWS, D_MODEL), BF16)],
        compiler_params=pltpu.CompilerParams(dimension_semantics=("arbitrary",),
                                             vmem_limit_bytes=VMEM_LIMIT_BYTES),
        name="moe",
    )(x1, gates, lw["gffn"], cw["lstrict"], cw["ustrict"], lw["wgu"], lw["wd"])


def _t5_bucket(dist):
    max_exact = N_BUCKETS // 2
    log_ratio = np.log(np.maximum(dist, max_exact) / max_exact) / np.log(MAX_DISTANCE / max_exact)
    large = np.minimum(max_exact + (log_ratio * (N_BUCKETS - max_exact)).astype(np.int32), N_BUCKETS - 1)
    return np.where(dist < max_exact, dist, large).astype(np.int32)


def _dilated_bias(rel_bias):
    tables = []
    span = 3 * Q_BLOCK
    for window, dil in DILATED_PAIRS:
        assert window // dil == Q_BLOCK
        per_dist = rel_bias[_t5_bucket(np.arange(Q_BLOCK + 1) * dil)].astype(F32).T * LOG2E
        diag = jnp.concatenate([jnp.full((N_HEADS, Q_BLOCK), -jnp.inf, F32), per_dist[:, ::-1],
                                jnp.full((N_HEADS, span - 2 * Q_BLOCK), -jnp.inf, F32)], axis=1)
        skew = jnp.tile(diag, (1, Q_BLOCK))[:, :Q_BLOCK * span].reshape(N_HEADS, Q_BLOCK, span)
        tables.append(skew[:, :, Q_BLOCK:])
    return jnp.stack(tables, axis=0)


def _rope_a(v1, v2):
    pad = jnp.zeros(v1.shape[:-1] + (LANES - 4 * ROPE_HALF,), v1.dtype)
    return jnp.concatenate([v1, v2, v1, v2, pad], axis=-1)


def _const_weights(positions, rel_bias):
    b, s = positions.shape
    inv_freq = ROPE_THETA ** (-jnp.arange(ROPE_HALF, dtype=F32) / ROPE_HALF)
    ang = positions.astype(F32).reshape(b * s, 1) * inv_freq
    cos, sin = jnp.cos(ang), jnp.sin(ang)
    idx = np.arange(512)
    return {
        "ca": _rope_a(cos, cos),
        "sb": _rope_a(-sin, sin),
        "g64": jnp.asarray(idx[:, None] // HEAD_DIM == idx[None, :] // HEAD_DIM, BF16),
        "g32": jnp.asarray(idx[:, None] // ROPE == idx[None, :] // ROPE, BF16),
        "tri": jnp.asarray(np.arange(FRONT_TM)[None, :] <= np.arange(FRONT_TM)[:, None], BF16),
        "lstrict": jnp.asarray(np.arange(MOE_TM)[None, :] < np.arange(MOE_TM)[:, None], BF16),
        "ustrict": jnp.asarray(np.arange(LANES)[:, None] < np.arange(LANES)[None, :], BF16),
        "bias": _dilated_bias(rel_bias),
    }


def _group_cols(w):
    w = w.reshape(N_GROUPS, EXPERTS_PER_GROUP, D_MODEL, D_FF)
    return w.transpose(0, 2, 1, 3).reshape(N_GROUPS, D_MODEL, EXPERTS_PER_GROUP * D_FF)


def _layer_weights(l, p):
    w_in = p["w_in"][l]
    o_hf = 3072
    o_cq = o_hf + N_HEADS
    o_ckv = o_cq + Q_LORA
    o_kr = o_ckv + KV_LORA
    o_g = o_kr + ROPE
    kr1 = w_in[:, o_kr:o_kr + ROPE_HALF]
    kr2 = w_in[:, o_kr + ROPE_HALF:o_kr + ROPE]
    hf_w = w_in[:, o_hf:o_cq]
    wf = jnp.concatenate([
        w_in[:, 0:3072], w_in[:, o_cq:o_ckv], w_in[:, o_ckv:o_kr],
        _rope_a(kr1, kr2), _rope_a(kr2, kr1),
        jnp.pad(hf_w, ((0, 0), (0, LANES - N_HEADS))),
    ], axis=1).astype(BF16)
    wft = jnp.pad(hf_w.T, ((0, HF_ROWS - N_HEADS), (0, 0))).astype(BF16)

    wuq = p["w_uq"][l].reshape(Q_LORA, N_HEADS, NOPE + ROPE)
    q_nope = wuq[:, :, :NOPE].reshape(Q_LORA, N_HEADS * NOPE)
    q1 = wuq[:, :, NOPE:NOPE + ROPE_HALF]
    q2 = wuq[:, :, NOPE + ROPE_HALF:]

    def pair_rope(v1, v2):
        x = jnp.concatenate([v1, v2], axis=-1).reshape(Q_LORA, N_PAIRS, 2 * ROPE)
        return jnp.pad(x, ((0, 0), (0, 0), (0, LANES - 2 * ROPE))).reshape(Q_LORA, N_PAIRS * LANES)

    wuq_p = jnp.concatenate([q_nope, pair_rope(q1, q2), pair_rope(q2, q1)], axis=1).astype(BF16)
    wukv = p["w_ukv"][l].reshape(KV_LORA, N_HEADS, NOPE + HEAD_DIM)
    wukv_p = jnp.concatenate([wukv[:, :, :NOPE].reshape(KV_LORA, 512),
                              wukv[:, :, NOPE:].reshape(KV_LORA, 512)], axis=1).astype(BF16)

    tile8 = lambda g: jnp.tile(g, N_HEADS)
    sc_ab = LOG2E / math.sqrt(HEAD_DIM)
    sc_c = LOG2E / math.sqrt(NOPE + ROPE)
    gq_c, gk_c = p["gq_c"][l], p["gk_c"][l]
    zeros512 = jnp.zeros((512,), F32)
    g512 = jnp.stack([tile8(p["gq_a"][l]) * sc_ab, tile8(p["gk_a"][l]),
                      tile8(p["gq_b"][l]) * sc_ab, tile8(p["gk_b"][l]),
                      tile8(gq_c[:NOPE]) * sc_c, tile8(gk_c[:NOPE]), zeros512, zeros512])
    gq1, gq2 = gq_c[NOPE:NOPE + ROPE_HALF] * sc_c, gq_c[NOPE + ROPE_HALF:] * sc_c
    gk1, gk2 = gk_c[NOPE:NOPE + ROPE_HALF], gk_c[NOPE + ROPE_HALF:]
    zeros128 = jnp.zeros((LANES,), F32)
    bf = p["b_forget"][l].astype(F32)
    g128 = jnp.stack([_rope_a(gq1, gq2), _rope_a(gq2, gq1), _rope_a(gk1, gk2), _rope_a(gk2, gk1),
                      jnp.pad(bf, (0, LANES - N_HEADS)), zeros128, zeros128, zeros128])
    bfrow = jnp.broadcast_to(jnp.pad(bf, (0, HF_ROWS - N_HEADS))[:, None], (HF_ROWS, LANES))

    wr = jnp.concatenate([p["w_router_expert"][l], p["w_router_group"][l]], axis=1)
    wr = jnp.pad(wr, ((0, 0), (0, LANES - N_EXPERTS - N_GROUPS))).astype(F32)
    wrh = wr.astype(BF16)
    wrl = (wr - wrh.astype(F32)).astype(BF16)
    br = jnp.pad(jnp.concatenate([p["b_router_expert"][l], p["b_router_group"][l]]).astype(F32),
                 (0, LANES - N_EXPERTS - N_GROUPS))[None, :]
    return {
        "gmix": p["norm_mix"][l][None, :], "wf": wf, "wft": wft, "wuq": wuq_p, "wukv": wukv_p,
        "g512": g512, "g128": g128, "ncq": p["norm_cq"][l][None, :], "nckv": p["norm_ckv"][l][None, :],
        "bfrow": bfrow,
        "wg": w_in[:, o_g:].astype(BF16), "wb": p["w_branch"][l].astype(BF16), "wo": p["w_out"][l].astype(BF16),
        "gffn": p["norm_ffn"][l][None, :], "wr": jnp.concatenate([wrh, wrl], axis=1), "br": br,
        "wgu": jnp.concatenate([_group_cols(p["w_expert_gate"][l]), _group_cols(p["w_expert_up"][l])],
                               axis=-1).astype(BF16),
        "wd": p["w_expert_down"][l].reshape(N_GROUPS, EXPERTS_PER_GROUP * D_FF, D_MODEL).astype(BF16),
    }


def kernel(x, positions, rel_bias, norm_mix, w_in, b_forget, gq_a, gk_a, gq_b, gk_b, gq_c, gk_c, norm_cq, norm_ckv, w_uq, w_ukv, w_branch, w_out, norm_ffn, w_router_group, b_router_group, w_router_expert, b_router_expert, w_expert_gate, w_expert_up, w_expert_down):
    batch, seq_len, d_model = x.shape
    assert d_model == D_MODEL and seq_len % (Q_BLOCK * DILATED_PAIRS[-1][1]) == 0
    assert seq_len == DILATED_PAIRS[-1][0], "the widest dilated window is assumed to span the sequence"
    p = dict(norm_mix=norm_mix, w_in=w_in, b_forget=b_forget, gq_a=gq_a, gk_a=gk_a, gq_b=gq_b, gk_b=gk_b,
             gq_c=gq_c, gk_c=gk_c, norm_cq=norm_cq, norm_ckv=norm_ckv, w_uq=w_uq, w_ukv=w_ukv,
             w_branch=w_branch, w_out=w_out, norm_ffn=norm_ffn, w_router_group=w_router_group,
             b_router_group=b_router_group, w_router_expert=w_router_expert, b_router_expert=b_router_expert,
             w_expert_gate=w_expert_gate, w_expert_up=w_expert_up, w_expert_down=w_expert_down)
    cw = _const_weights(positions, rel_bias)
    xs = x.reshape(batch * seq_len, d_model)
    for l in range(norm_mix.shape[0]):
        lw = _layer_weights(l, p)
        qa, ka, va, qb, kb, vb, qc, kc, vc, ccol, crow = _front_call(xs, cw, lw, seq_len)
        oa = _dilated_call(qa, ka, va, cw["bias"], batch, seq_len)
        ob = _flash_call(qb, kb, vb, batch, seq_len, crow, ccol)
        oc = _flash_call(qc, kc, vc, batch, seq_len)
        x1, gates = _merge_call(xs, oa, ob, oc, lw)
        xs = _moe_call(x1, gates, cw, lw)
    return xs.reshape(batch, seq_len, d_model)
```

```python
import functools
import math

import numpy as np
import jax
import jax.numpy as jnp
from jax import lax
from jax.experimental import pallas as pl
from jax.experimental.pallas import tpu as pltpu

F32 = jnp.float32
BF16 = jnp.bfloat16

D_MODEL = 1024
N_HEADS = 8
HEAD_DIM = 64
N_PAIRS = N_HEADS // 2
PAIR_W = 2 * HEAD_DIM
DILATED_PAIRS = ((128, 1), (512, 4), (2048, 16))
Q_LORA = 256
KV_LORA = 128
NOPE = 64
ROPE = 32
ROPE_HALF = ROPE // 2
ROPE_THETA = 10000.0
N_BUCKETS = 32
MAX_DISTANCE = 2048
Q_BLOCK = 128
N_GROUPS = 4
EXPERTS_PER_GROUP = 4
N_EXPERTS = 16
D_FF = 256
EPS = 1e-6
LOG2E = math.log2(math.e)
LANES = 128

VMEM_LIMIT_BYTES = 56 * 1024 * 1024

FRONT_TM = 512
MERGE_TM = 512
MOE_TM = 512
MOE_CHUNK = 128
MOE_ROWS = MOE_TM + (N_GROUPS - 1) * MOE_CHUNK
GROUP_LANE = N_EXPERTS
FLASH_T = 256

C_A = 0
C_B = 1536
C_CQ = 3072
C_CKV = C_CQ + Q_LORA
C_KRA = C_CKV + KV_LORA
C_KRB = C_KRA + LANES
C_HF = C_KRB + LANES
N_FRONT = C_HF + LANES
HF_ROWS = 16
GROUP_SLAB = 256


def _nt_dot(a, b):
    return lax.dot_general(a, b, (((1,), (1,)), ((), ())), preferred_element_type=F32)


def _dot(a, b):
    return jnp.dot(a, b, preferred_element_type=F32)


def _log_sigmoid(x):
    return jnp.minimum(x, 0.0) - jnp.log1p(jnp.exp(-jnp.abs(x)))


def _split_hi_lo(x):
    hi = x.astype(BF16)
    lo = (x - hi.astype(F32)).astype(BF16)
    return hi, lo


def _front_kernel(x_ref, gmix_ref, wf_ref, wft_ref, wuq_ref, wukv_ref, g64_ref, g32_ref, tri_ref,
                  g512_ref, g128_ref, ncq_ref, nckv_ref, bfrow_ref, ca_ref, sb_ref,
                  qa_ref, ka_ref, va_ref, qb_ref, kb_ref, vb_ref, qc_ref, kc_ref, vc_ref,
                  ccol_ref, crow_ref, carry_col, carry_row, *, tiles_per_seq):
    i = pl.program_id(0)

    @pl.when(i % tiles_per_seq == 0)
    def _():
        carry_col[...] = jnp.zeros_like(carry_col)
        carry_row[...] = jnp.zeros_like(carry_row)

    x = x_ref[...]
    ms = jnp.mean(x * x, axis=-1, keepdims=True)
    xn = (x * lax.rsqrt(ms + EPS) * gmix_ref[...]).astype(BF16)

    def group_sums(h, gmat):
        sq = (h * h).astype(BF16)
        slab = gmat.shape[0]
        return jnp.concatenate([_dot(sq[:, i:i + slab], gmat) for i in range(0, h.shape[1], slab)], axis=-1)

    def group_norm(h, gmat, n, gain):
        return h * lax.rsqrt(group_sums(h, gmat) * (1.0 / n) + EPS) * gain

    def store_pairs(ref, val, lo=0, w=PAIR_W):
        for p in range(N_PAIRS):
            ref[p, :, lo:lo + w] = val[:, p * PAIR_W:(p + 1) * PAIR_W].astype(BF16)

    g64 = g64_ref[...]
    for base, q_ref, k_ref, v_ref, row in ((C_A, qa_ref, ka_ref, va_ref, 0), (C_B, qb_ref, kb_ref, vb_ref, 2)):
        hq = _dot(xn, wf_ref[:, base:base + 512])
        store_pairs(q_ref, group_norm(hq, g64, HEAD_DIM, g512_ref[row:row + 1, :]))
        hk = _dot(xn, wf_ref[:, base + 512:base + 1024])
        store_pairs(k_ref, group_norm(hk, g64, HEAD_DIM, g512_ref[row + 1:row + 2, :]))
        hv = _dot(xn, wf_ref[:, base + 1024:base + 1536])
        store_pairs(v_ref, hv)

    hs = _dot(xn, wf_ref[:, C_CQ:N_FRONT])
    ca = ca_ref[...]
    sb = sb_ref[...]

    hcq = hs[:, 0:Q_LORA]
    cq = (hcq * lax.rsqrt(jnp.mean(hcq * hcq, axis=-1, keepdims=True) + EPS) * ncq_ref[...]).astype(BF16)
    qc = _dot(cq, wuq_ref[...])
    qn = group_norm(qc[:, 0:512], g64, NOPE, g512_ref[4:5, :])
    store_pairs(qc_ref, qn, 0)
    qra = qc[:, 512:1024]
    qrb = qc[:, 1024:1536]
    rs = lax.rsqrt(group_sums(qra, g32_ref[...]) * (1.0 / ROPE) + EPS)
    ga = g128_ref[0:1, :]
    gb = g128_ref[1:2, :]
    for p in range(N_PAIRS):
        sl = slice(p * PAIR_W, (p + 1) * PAIR_W)
        qr = (qra[:, sl] * ga * ca + qrb[:, sl] * gb * sb) * rs[:, sl]
        qc_ref[p, :, PAIR_W:2 * PAIR_W] = qr.astype(BF16)

    hckv = hs[:, Q_LORA:Q_LORA + KV_LORA]
    ckv = (hckv * lax.rsqrt(jnp.mean(hckv * hckv, axis=-1, keepdims=True) + EPS) * nckv_ref[...]).astype(BF16)
    kv = _dot(ckv, wukv_ref[...])
    store_pairs(kc_ref, group_norm(kv[:, 0:512], g64, NOPE, g512_ref[5:6, :]), 0)
    store_pairs(vc_ref, kv[:, 512:1024])
    kra = hs[:, C_KRA - C_CQ:C_KRA - C_CQ + LANES]
    krb = hs[:, C_KRB - C_CQ:C_KRB - C_CQ + LANES]
    rsk = lax.rsqrt(_dot((kra * kra).astype(BF16), g32_ref[0:LANES, 0:LANES]) * (1.0 / ROPE) + EPS)
    kr = ((kra * g128_ref[2:3, :] * ca + krb * g128_ref[3:4, :] * sb) * rsk).astype(BF16)
    for p in range(N_PAIRS):
        kc_ref[p, :, PAIR_W:2 * PAIR_W] = kr

    tri = tri_ref[...]
    hf_col = hs[:, C_HF - C_CQ:C_HF - C_CQ + LANES] + g128_ref[4:5, :]
    hi, lo = _split_hi_lo(_log_sigmoid(hf_col))
    ccol = _dot(tri, hi) + _dot(tri, lo) + carry_col[0:1, :]
    ccol_ref[...] = ccol
    tm = ccol.shape[0]
    carry_col[...] = jnp.broadcast_to(ccol[tm - 1:tm, :], carry_col.shape)

    hf_row = _nt_dot(wft_ref[...], xn) + bfrow_ref[:, 0:1]
    hi, lo = _split_hi_lo(_log_sigmoid(hf_row))
    crow = _nt_dot(hi, tri) + _nt_dot(lo, tri) + carry_row[:, 0:1]
    crow_ref[...] = crow
    carry_row[...] = jnp.broadcast_to(crow[:, tm - 1:tm], carry_row.shape)


def _front_call(x2d, cw, lw, seq_len):
    t = x2d.shape[0]
    tm = FRONT_TM
    const = lambda shape: pl.BlockSpec(shape, lambda i: (0,) * len(shape))
    pair_out = lambda w: pl.BlockSpec((N_PAIRS, tm, w), lambda i: (0, i, 0))
    pair_shape = lambda w: jax.ShapeDtypeStruct((N_PAIRS, t, w), BF16)
    in_specs = [
        pl.BlockSpec((tm, D_MODEL), lambda i: (i, 0)),
        const((1, D_MODEL)),
        const((D_MODEL, N_FRONT)),
        const((HF_ROWS, D_MODEL)),
        const((Q_LORA, 1536)),
        const((KV_LORA, 1024)),
        const((GROUP_SLAB, GROUP_SLAB)),
        const((GROUP_SLAB, GROUP_SLAB)),
        const((tm, tm)),
        const((8, 512)),
        const((8, LANES)),
        const((1, Q_LORA)),
        const((1, KV_LORA)),
        const((HF_ROWS, LANES)),
        pl.BlockSpec((tm, LANES), lambda i: (i, 0)),
        pl.BlockSpec((tm, LANES), lambda i: (i, 0)),
    ]
    out_specs = [pair_out(PAIR_W)] * 6 + [pair_out(2 * PAIR_W), pair_out(2 * PAIR_W), pair_out(PAIR_W),
                                           pl.BlockSpec((tm, LANES), lambda i: (i, 0)),
                                           pl.BlockSpec((HF_ROWS, tm), lambda i: (0, i))]
    out_shape = [pair_shape(PAIR_W)] * 6 + [pair_shape(2 * PAIR_W), pair_shape(2 * PAIR_W), pair_shape(PAIR_W),
                                            jax.ShapeDtypeStruct((t, LANES), F32),
                                            jax.ShapeDtypeStruct((HF_ROWS, t), F32)]
    return pl.pallas_call(
        functools.partial(_front_kernel, tiles_per_seq=seq_len // tm),
        grid=(t // tm,),
        in_specs=in_specs,
        out_specs=out_specs,
        out_shape=out_shape,
        scratch_shapes=[pltpu.VMEM((8, LANES), F32), pltpu.VMEM((HF_ROWS, LANES), F32)],
        compiler_params=pltpu.CompilerParams(dimension_semantics=("arbitrary",),
                                             vmem_limit_bytes=VMEM_LIMIT_BYTES),
        name="front",
    )(x2d, lw["gmix"], lw["wf"], lw["wft"], lw["wuq"], lw["wukv"], cw["g64"], cw["g32"], cw["tri"],
      lw["g512"], lw["g128"], lw["ncq"], lw["nckv"], lw["bfrow"], cw["ca"], cw["sb"])


def _dilated_kernel(q_ref, k_ref, v_ref, bias_ref, o_ref, oacc, lacc, s_scr, nat, res4, g4, g16, vx, *, seq_len):
    lane = lax.broadcasted_iota(jnp.int32, (1, PAIR_W), 1)
    left = lane < HEAD_DIM
    d_mid, d_far = DILATED_PAIRS[1][1], DILATED_PAIRS[2][1]
    assert DILATED_PAIRS[0][1] == 1 and d_far == d_mid * d_mid
    n_mid, n_far = seq_len // d_mid, seq_len // d_far

    vx[:, :, PAIR_W:2 * PAIR_W] = jnp.ones((len(DILATED_PAIRS), seq_len, PAIR_W), BF16)
    vx[0, :, 0:PAIR_W] = v_ref[...]
    for ti, src in enumerate((q_ref, k_ref, v_ref)):
        nat[ti] = src[...].astype(F32)

    def put(ti, pattern, lo, hi, rows):
        if ti == 2:
            vx[pattern, lo:hi, 0:PAIR_W] = rows.astype(BF16)
        else:
            (g4, g16)[pattern - 1][ti, lo:hi, :] = rows.astype(BF16)

    for ti in range(3):
        for c in range(d_mid):
            rows = nat[ti, pl.ds(c, n_mid, stride=d_mid), :]
            res4[ti, c * n_mid:(c + 1) * n_mid, :] = rows
            put(ti, 1, c * n_mid, (c + 1) * n_mid, rows)
    for ti in range(3):
        for c in range(d_far):
            rows = res4[ti, pl.ds((c % d_mid) * n_mid + c // d_mid, n_far, stride=d_mid), :]
            put(ti, 2, c * n_far, (c + 1) * n_far, rows)

    readers = (
        (1, lambda ti, lo, hi: (q_ref, k_ref)[ti][lo:hi, :]),
        (d_mid, lambda ti, lo, hi: g4[ti, lo:hi, :]),
        (d_far, lambda ti, lo, hi: g16[ti, lo:hi, :]),
    )
    for di, (d, read) in enumerate(readers):
        n_sub = seq_len // d
        n_blocks = n_sub // Q_BLOCK
        blocks = [(c, nb) for c in range(d) for nb in range(n_blocks)]

        def window(c, nb):
            k_lo = max(nb - 1, 0) * Q_BLOCK
            return c * n_sub + k_lo, c * n_sub + (nb + 1) * Q_BLOCK

        for bi, (c, nb) in enumerate(blocks):
            q = read(0, c * n_sub + nb * Q_BLOCK, c * n_sub + (nb + 1) * Q_BLOCK)
            kwin = read(1, *window(c, nb))
            w = kwin.shape[0]
            for j in range(2):
                qh = jnp.where(left if j == 0 else jnp.logical_not(left), q, jnp.zeros_like(q))
                s_scr[j, bi * Q_BLOCK:(bi + 1) * Q_BLOCK, 0:w] = (
                    _nt_dot(qh, kwin) + bias_ref[di, j, :, 2 * Q_BLOCK - w:])
        for bi, (c, nb) in enumerate(blocks):
            lo, hi = window(c, nb)
            w = hi - lo
            outs, lses = [], []
            for j in range(2):
                s = s_scr[j, bi * Q_BLOCK:(bi + 1) * Q_BLOCK, 0:w]
                m = jnp.max(s, axis=-1, keepdims=True)
                p = jnp.exp2(s - m).astype(BF16)
                res = _dot(p, vx[di, lo:hi, :])
                den = res[:, PAIR_W:2 * PAIR_W]
                outs.append(res[:, 0:PAIR_W] / den)
                lses.append(m + jnp.log2(den))
            start = nb * Q_BLOCK * d + c
            rows = pl.ds(start, Q_BLOCK) if d == 1 else pl.ds(start, Q_BLOCK, stride=d)
            oacc[di, rows, :] = jnp.where(left, outs[0], outs[1])
            lacc[di, rows, :] = jnp.where(left, lses[0], lses[1])

    l0, l1, l2 = lacc[0], lacc[1], lacc[2]
    m = jnp.maximum(jnp.maximum(l0, l1), l2)
    w0, w1, w2 = jnp.exp2(l0 - m), jnp.exp2(l1 - m), jnp.exp2(l2 - m)
    wsum = w0 + w1 + w2
    o = oacc[0] * (w0 / wsum) + oacc[1] * (w1 / wsum) + oacc[2] * (w2 / wsum)
    o_ref[...] = o.astype(BF16)


def _dilated_call(q, k, v, bias, batch, seq_len):
    n_pat = len(DILATED_PAIRS)
    seq_block = pl.BlockSpec((None, None, seq_len, PAIR_W), lambda p, b: (p, b, 0, 0))
    view = lambda a: a.reshape(N_PAIRS, batch, seq_len, PAIR_W)
    seq_scratch = lambda n, dt: pltpu.VMEM((n, seq_len, PAIR_W), dt)
    out = pl.pallas_call(
        functools.partial(_dilated_kernel, seq_len=seq_len),
        grid=(N_PAIRS, batch),
        in_specs=[seq_block, seq_block, seq_block,
                  pl.BlockSpec((n_pat, 2, Q_BLOCK, 2 * Q_BLOCK), lambda p, b: (0, p, 0, 0))],
        out_specs=seq_block,
        out_shape=jax.ShapeDtypeStruct((N_PAIRS, batch, seq_len, PAIR_W), BF16),
        scratch_shapes=[seq_scratch(n_pat, F32), seq_scratch(n_pat, F32),
                        pltpu.VMEM((2, seq_len, 2 * Q_BLOCK), F32),
                        seq_scratch(3, F32), seq_scratch(3, F32), seq_scratch(2, BF16), seq_scratch(2, BF16),
                        pltpu.VMEM((n_pat, seq_len, 2 * PAIR_W), BF16)],
        compiler_params=pltpu.CompilerParams(dimension_semantics=("arbitrary", "arbitrary"),
                                             vmem_limit_bytes=VMEM_LIMIT_BYTES),
        name="dilated",
    )(view(q), view(k), view(v), bias)
    return out.reshape(N_PAIRS, batch * seq_len, PAIR_W)


def _flash_kernel(*refs, kw, decay, seq_len):
    if decay:
        q_ref, k_ref, v_ref, crow_ref, ccol_ref, o_ref, s_scr, p_scr, v_scr = refs
    else:
        q_ref, k_ref, v_ref, o_ref, s_scr, p_scr, v_scr = refs
    hp = pl.program_id(0)
    t = FLASH_T
    lane = lax.broadcasted_iota(jnp.int32, (1, kw), 1)
    in_h0 = (lane < HEAD_DIM) | ((lane >= PAIR_W) & (lane < PAIR_W + ROPE))
    in_h1 = ((lane >= HEAD_DIM) & (lane < PAIR_W)) | ((lane >= PAIR_W + ROPE) & (lane < PAIR_W + 2 * ROPE))
    row = lax.broadcasted_iota(jnp.int32, (t, t), 0)
    col = lax.broadcasted_iota(jnp.int32, (t, t), 1)
    causal = col <= row
    out_lane = lax.broadcasted_iota(jnp.int32, (1, PAIR_W), 1)
    v_scr[:, 0:PAIR_W] = v_ref[...]
    v_scr[:, PAIR_W:2 * PAIR_W] = jnp.ones((seq_len, PAIR_W), BF16)
    for qi in range(seq_len // t):
        rows = slice(qi * t, (qi + 1) * t)
        q = q_ref[rows, :]
        width = (qi + 1) * t
        outs = []
        for j, sel in enumerate((in_h0, in_h1)):
            buf = 2 * (qi % 2) + j
            qh = jnp.where(sel, q, jnp.zeros_like(q))
            mx = None
            for c in range(qi + 1):
                cols = slice(c * t, (c + 1) * t)
                s = _nt_dot(qh, k_ref[cols, :])
                if decay:
                    s = s - crow_ref[pl.ds(2 * hp + j, 1), cols] * LOG2E
                if c == qi:
                    s = jnp.where(causal, s, -jnp.inf)
                s_scr[buf, :, cols] = s
                half = jnp.maximum(s[:, 0:t // 2], s[:, t // 2:t])
                mx = half if mx is None else jnp.maximum(mx, half)
            shift = jnp.max(mx, axis=-1, keepdims=True)
            if decay:
                head = (2 * hp + j).astype(F32)
                cc = ccol_ref[rows, :]
                lane_c = lax.broadcasted_iota(jnp.int32, cc.shape, 1).astype(F32)
                cq = jnp.sum(jnp.where(lane_c == head, cc, 0.0), axis=-1, keepdims=True) * LOG2E
                shift = (shift + cq) - cq
            for c in range(qi + 1):
                cols = slice(c * t, (c + 1) * t)
                p_scr[buf, :, cols] = jnp.exp2(s_scr[buf, :, cols] - shift).astype(BF16)
            res = _dot(p_scr[buf, :, 0:width], v_scr[0:width, :])
            outs.append(res[:, 0:PAIR_W] / res[:, PAIR_W:2 * PAIR_W])
        o_ref[rows, :] = jnp.where(out_lane < HEAD_DIM, outs[0], outs[1]).astype(BF16)


def _flash_call(q, k, v, batch, seq_len, crow=None, ccol=None):
    kw = q.shape[-1]
    decay = crow is not None
    view = lambda a: a.reshape(N_PAIRS, batch, seq_len, a.shape[-1])
    args = [view(q), view(k), view(v)]
    seq_block = lambda w: pl.BlockSpec((None, None, seq_len, w), lambda p, b: (p, b, 0, 0))
    in_specs = [seq_block(kw), seq_block(kw), seq_block(PAIR_W)]
    if decay:
        args += [crow, ccol]
        in_specs += [pl.BlockSpec((HF_ROWS, seq_len), lambda p, b: (0, b)),
                     pl.BlockSpec((seq_len, LANES), lambda p, b: (b, 0))]
    out = pl.pallas_call(
        functools.partial(_flash_kernel, kw=kw, decay=decay, seq_len=seq_len),
        grid=(N_PAIRS, batch),
        in_specs=in_specs,
        out_specs=seq_block(PAIR_W),
        out_shape=jax.ShapeDtypeStruct((N_PAIRS, batch, seq_len, PAIR_W), BF16),
        scratch_shapes=[pltpu.VMEM((4, FLASH_T, seq_len), F32), pltpu.VMEM((4, FLASH_T, seq_len), BF16),
                        pltpu.VMEM((seq_len, 2 * PAIR_W), BF16)],
        compiler_params=pltpu.CompilerParams(dimension_semantics=("arbitrary",) * 2,
                                             vmem_limit_bytes=VMEM_LIMIT_BYTES),
        name="flash_fox" if decay else "flash_mla",
    )(*args)
    return out.reshape(N_PAIRS, batch * seq_len, PAIR_W)


def _merge_kernel(x_ref, oa_ref, ob_ref, oc_ref, gmix_ref, wg_ref, wb_ref, wo_ref, gffn_ref,
                  wr_ref, br_ref, x1_ref, gates_ref):
    x = x_ref[...]
    ms = jnp.mean(x * x, axis=-1, keepdims=True)
    xn = (x * lax.rsqrt(ms + EPS) * gmix_ref[...]).astype(BF16)
    mixed = None
    for g, o_ref in enumerate((oa_ref, ob_ref, oc_ref)):
        o = jnp.concatenate([o_ref[p] for p in range(N_PAIRS)], axis=-1)
        gate = jax.nn.sigmoid(_dot(xn, wg_ref[:, g * D_MODEL:(g + 1) * D_MODEL]))
        term = gate * _dot(o, wb_ref[g])
        mixed = term if mixed is None else mixed + term
    x1 = x + _dot(mixed.astype(BF16), wo_ref[...])
    x1_ref[...] = x1

    ms1 = jnp.mean(x1 * x1, axis=-1, keepdims=True)
    xf = x1 * lax.rsqrt(ms1 + EPS) * gffn_ref[...]
    hi, lo = _split_hi_lo(xf)
    tm = xf.shape[0]
    parts = _dot(jnp.concatenate([hi, lo], axis=0), wr_ref[...])
    lg = (parts[0:tm, 0:LANES] + parts[0:tm, LANES:2 * LANES]
          + parts[tm:2 * tm, 0:LANES] + parts[tm:2 * tm, LANES:2 * LANES]) + br_ref[...]
    lane = lax.broadcasted_iota(jnp.int32, lg.shape, 1)
    lane_f = lane.astype(F32)
    neg = -jnp.inf
    far = float(LANES)
    gl = jnp.where((lane >= N_EXPERTS) & (lane < N_EXPERTS + N_GROUPS), lg, neg)
    gmax = jnp.max(gl, axis=-1, keepdims=True)
    pg_top = 1.0 / jnp.sum(jnp.exp(gl - gmax), axis=-1, keepdims=True)
    gidx = jnp.min(jnp.where(gl == gmax, lane_f, far), axis=-1, keepdims=True) - float(N_EXPERTS)
    in_group = (lane < N_EXPERTS) & ((lane // EXPERTS_PER_GROUP).astype(F32) == gidx)
    ev = jnp.where(in_group, lg, neg)
    v1 = jnp.max(ev, axis=-1, keepdims=True)
    i1 = jnp.min(jnp.where(ev == v1, lane_f, far), axis=-1, keepdims=True)
    ev2 = jnp.where(lane_f == i1, neg, ev)
    v2 = jnp.max(ev2, axis=-1, keepdims=True)
    i2 = jnp.min(jnp.where(ev2 == v2, lane_f, far), axis=-1, keepdims=True)
    e2 = jnp.exp(v2 - v1)
    den = 1.0 + e2
    w1 = (1.0 / den) * pg_top
    w2 = (e2 / den) * pg_top
    gates_ref[...] = jnp.where(lane_f == i1, w1, jnp.where(lane_f == i2, w2,
                                                           jnp.where(lane == GROUP_LANE, gidx, 0.0)))


def _merge_call(x2d, oa, ob, oc, lw):
    t = x2d.shape[0]
    tm = MERGE_TM
    const = lambda shape: pl.BlockSpec(shape, lambda i: (0,) * len(shape))
    row_tile = lambda w: pl.BlockSpec((tm, w), lambda i: (i, 0))
    pair_in = pl.BlockSpec((N_PAIRS, tm, PAIR_W), lambda i: (0, i, 0))
    return pl.pallas_call(
        _merge_kernel,
        grid=(t // tm,),
        in_specs=[row_tile(D_MODEL), pair_in, pair_in, pair_in, const((1, D_MODEL)),
                  const((D_MODEL, 3 * D_MODEL)), const((3, 512, D_MODEL)), const((D_MODEL, D_MODEL)),
                  const((1, D_MODEL)), const((D_MODEL, 2 * LANES)), const((1, LANES))],
        out_specs=[row_tile(D_MODEL), row_tile(LANES)],
        out_shape=[jax.ShapeDtypeStruct((t, D_MODEL), F32), jax.ShapeDtypeStruct((t, LANES), F32)],
        compiler_params=pltpu.CompilerParams(dimension_semantics=("arbitrary",),
                                             vmem_limit_bytes=VMEM_LIMIT_BYTES),
        name="merge",
    )(x2d, oa, ob, oc, lw["gmix"], lw["wg"], lw["wb"], lw["wo"], lw["gffn"], lw["wr"], lw["br"])


def _moe_kernel(x_ref, gates_ref, gffn_ref, lstrict_ref, ustrict_ref, wgu_ref, wd_ref, o_ref,
                xs_ref, gs_ref, ys_ref):
    x = x_ref[...]
    tm = x.shape[0]
    ms = jnp.mean(x * x, axis=-1, keepdims=True)
    xn = (x * lax.rsqrt(ms + EPS) * gffn_ref[...]).astype(BF16)

    gates = gates_ref[...]
    lane = lax.broadcasted_iota(jnp.int32, gates.shape, 1)
    gid = jnp.sum(jnp.where(lane == GROUP_LANE, gates, 0.0), axis=-1, keepdims=True)
    onehot = jnp.where((lane.astype(F32) == gid) & (lane < N_GROUPS), 1.0, 0.0)
    before = _dot(lstrict_ref[...], onehot.astype(BF16))
    counts = before[tm - 1:tm, :] + onehot[tm - 1:tm, :]
    n_chunks = jnp.floor((counts + (MOE_CHUNK - 1.0)) * (1.0 / MOE_CHUNK))
    seg_start = _dot(jnp.broadcast_to(n_chunks, (8, LANES)).astype(BF16), ustrict_ref[...])[0:1, :] * MOE_CHUNK
    rank = jnp.sum(onehot * (before + seg_start), axis=-1, keepdims=True)
    rank_row = jnp.transpose(jnp.broadcast_to(rank, (tm, LANES)))[0:1, :]
    slot_col = lax.broadcasted_iota(jnp.int32, (MOE_ROWS, 1), 0).astype(F32)
    slot_row = lax.broadcasted_iota(jnp.int32, (1, MOE_ROWS), 1).astype(F32)
    perm = jnp.where(slot_col == rank_row, 1.0, 0.0).astype(BF16)
    perm_t = jnp.where(rank == slot_row, 1.0, 0.0).astype(BF16)

    xs_ref[...] = _dot(perm, xn).astype(BF16)
    g_hi, g_lo = _split_hi_lo(gates)
    g_sorted = _dot(perm, jnp.concatenate([g_hi, g_lo], axis=-1))
    gs_ref[...] = g_sorted[:, 0:LANES] + g_sorted[:, LANES:2 * LANES]
    ys_ref[...] = jnp.zeros_like(ys_ref)

    nc = [n_chunks[0, g].astype(jnp.int32) for g in range(N_GROUPS)]
    ends = [nc[0], nc[0] + nc[1], nc[0] + nc[1] + nc[2]]
    total = ends[2] + nc[3]
    ff_all = EXPERTS_PER_GROUP * D_FF

    def chunk(k, carry):
        g = sum((k >= e).astype(jnp.int32) for e in ends)
        r0 = pl.multiple_of(k * MOE_CHUNK, MOE_CHUNK)
        xc = xs_ref[pl.ds(r0, MOE_CHUNK), :]
        gc = gs_ref[pl.ds(r0, MOE_CHUNK), :]
        lane_c = lax.broadcasted_iota(jnp.int32, gc.shape, 1)
        parts = []
        for e in range(EXPERTS_PER_GROUP):
            h_gate = _dot(xc, wgu_ref[g, :, e * D_FF:(e + 1) * D_FF])
            h_up = _dot(xc, wgu_ref[g, :, ff_all + e * D_FF:ff_all + (e + 1) * D_FF])
            gate = jnp.sum(jnp.where(lane_c == g * EXPERTS_PER_GROUP + e, gc, 0.0), axis=-1, keepdims=True)
            parts.append((jax.nn.silu(h_gate) * h_up * gate).astype(BF16))
        y = _dot(jnp.concatenate(parts, axis=-1), wd_ref[g])
        ys_ref[pl.ds(r0, MOE_CHUNK), :] = y.astype(BF16)
        return carry

    lax.fori_loop(0, total, chunk, 0)
    o_ref[...] = x + _dot(perm_t, ys_ref[...])


def _moe_call(x1, gates, cw, lw):
    t = x1.shape[0]
    tm = MOE_TM
    ff_all = EXPERTS_PER_GROUP * D_FF
    const = lambda shape: pl.BlockSpec(shape, lambda i: (0,) * len(shape))
    resident = lambda shape: pl.BlockSpec(shape, lambda i: (0,) * len(shape), pipeline_mode=pl.Buffered(1))
    return pl.pallas_call(
        _moe_kernel,
        grid=(t // tm,),
        in_specs=[pl.BlockSpec((tm, D_MODEL), lambda i: (i, 0)),
                  pl.BlockSpec((tm, LANES), lambda i: (i, 0)),
                  const((1, D_MODEL)), const((tm, tm)), const((LANES, LANES)),
                  resident((N_GROUPS, D_MODEL, 2 * ff_all)),
                  resident((N_GROUPS, ff_all, D_MODEL))],
        out_specs=pl.BlockSpec((tm, D_MODEL), lambda i: (i, 0)),
        out_shape=jax.ShapeDtypeStruct((t, D_MODEL), F32),
        scratch_shapes=[pltpu.VMEM((MOE_ROWS, D_MODEL), BF16), pltpu.VMEM((MOE_ROWS, LANES), F32),
                        pltpu.VMEM((MOE_ROWS, D_MODEL), BF16)],
        compiler_params=pltpu.CompilerParams(dimension_semantics=("arbitrary",),
                                             vmem_limit_bytes=VMEM_LIMIT_BYTES),
        name="moe",
    )(x1, gates, lw["gffn"], cw["lstrict"], cw["ustrict"], lw["wgu"], lw["wd"])


def _t5_bucket(dist):
    max_exact = N_BUCKETS // 2
    log_ratio = np.log(np.maximum(dist, max_exact) / max_exact) / np.log(MAX_DISTANCE / max_exact)
    large = np.minimum(max_exact + (log_ratio * (N_BUCKETS - max_exact)).astype(np.int32), N_BUCKETS - 1)
    return np.where(dist < max_exact, dist, large).astype(np.int32)


def _dilated_bias(rel_bias):
    tables = []
    span = 3 * Q_BLOCK
    for window, dil in DILATED_PAIRS:
        assert window // dil == Q_BLOCK
        per_dist = rel_bias[_t5_bucket(np.arange(Q_BLOCK + 1) * dil)].astype(F32).T * LOG2E
        diag = jnp.concatenate([jnp.full((N_HEADS, Q_BLOCK), -jnp.inf, F32), per_dist[:, ::-1],
                                jnp.full((N_HEADS, span - 2 * Q_BLOCK), -jnp.inf, F32)], axis=1)
        skew = jnp.tile(diag, (1, Q_BLOCK))[:, :Q_BLOCK * span].reshape(N_HEADS, Q_BLOCK, span)
        tables.append(skew[:, :, Q_BLOCK:])
    return jnp.stack(tables, axis=0)


def _rope_a(v1, v2):
    pad = jnp.zeros(v1.shape[:-1] + (LANES - 4 * ROPE_HALF,), v1.dtype)
    return jnp.concatenate([v1, v2, v1, v2, pad], axis=-1)


def _const_weights(positions, rel_bias):
    b, s = positions.shape
    inv_freq = ROPE_THETA ** (-jnp.arange(ROPE_HALF, dtype=F32) / ROPE_HALF)
    ang = positions.astype(F32).reshape(b * s, 1) * inv_freq
    cos, sin = jnp.cos(ang), jnp.sin(ang)
    idx = np.arange(GROUP_SLAB)
    return {
        "ca": _rope_a(cos, cos),
        "sb": _rope_a(-sin, sin),
        "g64": jnp.asarray(idx[:, None] // HEAD_DIM == idx[None, :] // HEAD_DIM, BF16),
        "g32": jnp.asarray(idx[:, None] // ROPE == idx[None, :] // ROPE, BF16),
        "tri": jnp.asarray(np.arange(FRONT_TM)[None, :] <= np.arange(FRONT_TM)[:, None], BF16),
        "lstrict": jnp.asarray(np.arange(MOE_TM)[None, :] < np.arange(MOE_TM)[:, None], BF16),
        "ustrict": jnp.asarray(np.arange(LANES)[:, None] < np.arange(LANES)[None, :], BF16),
        "bias": _dilated_bias(rel_bias),
    }


def _group_cols(w):
    w = w.reshape(N_GROUPS, EXPERTS_PER_GROUP, D_MODEL, D_FF)
    return w.transpose(0, 2, 1, 3).reshape(N_GROUPS, D_MODEL, EXPERTS_PER_GROUP * D_FF)


def _layer_weights(l, p):
    w_in = p["w_in"][l]
    o_hf = 3072
    o_cq = o_hf + N_HEADS
    o_ckv = o_cq + Q_LORA
    o_kr = o_ckv + KV_LORA
    o_g = o_kr + ROPE
    kr1 = w_in[:, o_kr:o_kr + ROPE_HALF]
    kr2 = w_in[:, o_kr + ROPE_HALF:o_kr + ROPE]
    hf_w = w_in[:, o_hf:o_cq]
    wf = jnp.concatenate([
        w_in[:, 0:3072], w_in[:, o_cq:o_ckv], w_in[:, o_ckv:o_kr],
        _rope_a(kr1, kr2), _rope_a(kr2, kr1),
        jnp.pad(hf_w, ((0, 0), (0, LANES - N_HEADS))),
    ], axis=1).astype(BF16)
    wft = jnp.pad(hf_w.T, ((0, HF_ROWS - N_HEADS), (0, 0))).astype(BF16)

    wuq = p["w_uq"][l].reshape(Q_LORA, N_HEADS, NOPE + ROPE)
    q_nope = wuq[:, :, :NOPE].reshape(Q_LORA, N_HEADS * NOPE)
    q1 = wuq[:, :, NOPE:NOPE + ROPE_HALF]
    q2 = wuq[:, :, NOPE + ROPE_HALF:]

    def pair_rope(v1, v2):
        x = jnp.concatenate([v1, v2], axis=-1).reshape(Q_LORA, N_PAIRS, 2 * ROPE)
        return jnp.pad(x, ((0, 0), (0, 0), (0, LANES - 2 * ROPE))).reshape(Q_LORA, N_PAIRS * LANES)

    wuq_p = jnp.concatenate([q_nope, pair_rope(q1, q2), pair_rope(q2, q1)], axis=1).astype(BF16)
    wukv = p["w_ukv"][l].reshape(KV_LORA, N_HEADS, NOPE + HEAD_DIM)
    wukv_p = jnp.concatenate([wukv[:, :, :NOPE].reshape(KV_LORA, 512),
                              wukv[:, :, NOPE:].reshape(KV_LORA, 512)], axis=1).astype(BF16)

    tile8 = lambda g: jnp.tile(g, N_HEADS)
    sc_ab = LOG2E / math.sqrt(HEAD_DIM)
    sc_c = LOG2E / math.sqrt(NOPE + ROPE)
    gq_c, gk_c = p["gq_c"][l], p["gk_c"][l]
    zeros512 = jnp.zeros((512,), F32)
    g512 = jnp.stack([tile8(p["gq_a"][l]) * sc_ab, tile8(p["gk_a"][l]),
                      tile8(p["gq_b"][l]) * sc_ab, tile8(p["gk_b"][l]),
                      tile8(gq_c[:NOPE]) * sc_c, tile8(gk_c[:NOPE]), zeros512, zeros512])
    gq1, gq2 = gq_c[NOPE:NOPE + ROPE_HALF] * sc_c, gq_c[NOPE + ROPE_HALF:] * sc_c
    gk1, gk2 = gk_c[NOPE:NOPE + ROPE_HALF], gk_c[NOPE + ROPE_HALF:]
    zeros128 = jnp.zeros((LANES,), F32)
    bf = p["b_forget"][l].astype(F32)
    g128 = jnp.stack([_rope_a(gq1, gq2), _rope_a(gq2, gq1), _rope_a(gk1, gk2), _rope_a(gk2, gk1),
                      jnp.pad(bf, (0, LANES - N_HEADS)), zeros128, zeros128, zeros128])
    bfrow = jnp.broadcast_to(jnp.pad(bf, (0, HF_ROWS - N_HEADS))[:, None], (HF_ROWS, LANES))

    wr = jnp.concatenate([p["w_router_expert"][l], p["w_router_group"][l]], axis=1)
    wr = jnp.pad(wr, ((0, 0), (0, LANES - N_EXPERTS - N_GROUPS))).astype(F32)
    wrh = wr.astype(BF16)
    wrl = (wr - wrh.astype(F32)).astype(BF16)
    br = jnp.pad(jnp.concatenate([p["b_router_expert"][l], p["b_router_group"][l]]).astype(F32),
                 (0, LANES - N_EXPERTS - N_GROUPS))[None, :]
    return {
        "gmix": p["norm_mix"][l][None, :], "wf": wf, "wft": wft, "wuq": wuq_p, "wukv": wukv_p,
        "g512": g512, "g128": g128, "ncq": p["norm_cq"][l][None, :], "nckv": p["norm_ckv"][l][None, :],
        "bfrow": bfrow,
        "wg": w_in[:, o_g:].astype(BF16), "wb": p["w_branch"][l].astype(BF16), "wo": p["w_out"][l].astype(BF16),
        "gffn": p["norm_ffn"][l][None, :], "wr": jnp.concatenate([wrh, wrl], axis=1), "br": br,
        "wgu": jnp.concatenate([_group_cols(p["w_expert_gate"][l]), _group_cols(p["w_expert_up"][l])],
                               axis=-1).astype(BF16),
        "wd": p["w_expert_down"][l].reshape(N_GROUPS, EXPERTS_PER_GROUP * D_FF, D_MODEL).astype(BF16),
    }


def kernel(x, positions, rel_bias, norm_mix, w_in, b_forget, gq_a, gk_a, gq_b, gk_b, gq_c, gk_c, norm_cq, norm_ckv, w_uq, w_ukv, w_branch, w_out, norm_ffn, w_router_group, b_router_group, w_router_expert, b_router_expert, w_expert_gate, w_expert_up, w_expert_down):
    batch, seq_len, d_model = x.shape
    assert d_model == D_MODEL and seq_len % (Q_BLOCK * DILATED_PAIRS[-1][1]) == 0
    assert seq_len == DILATED_PAIRS[-1][0], "the widest dilated window is assumed to span the sequence"
    p = dict(norm_mix=norm_mix, w_in=w_in, b_forget=b_forget, gq_a=gq_a, gk_a=gk_a, gq_b=gq_b, gk_b=gk_b,
             gq_c=gq_c, gk_c=gk_c, norm_cq=norm_cq, norm_ckv=norm_ckv, w_uq=w_uq, w_ukv=w_ukv,
             w_branch=w_branch, w_out=w_out, norm_ffn=norm_ffn, w_router_group=w_router_group,
             b_router_group=b_router_group, w_router_expert=w_router_expert, b_router_expert=b_router_expert,
             w_expert_gate=w_expert_gate, w_expert_up=w_expert_up, w_expert_down=w_expert_down)
    cw = _const_weights(positions, rel_bias)
    xs = x.reshape(batch * seq_len, d_model)
    for l in range(norm_mix.shape[0]):
        lw = _layer_weights(l, p)
        qa, ka, va, qb, kb, vb, qc, kc, vc, ccol, crow = _front_call(xs, cw, lw, seq_len)
        oa = _dilated_call(qa, ka, va, cw["bias"], batch, seq_len)
        ob = _flash_call(qb, kb, vb, batch, seq_len, crow, ccol)
        oc = _flash_call(qc, kc, vc, batch, seq_len)
        x1, gates = _merge_call(xs, oa, ob, oc, lw)
        xs = _moe_call(x1, gates, cw, lw)
    return xs.reshape(batch, seq_len, d_model)
```

```python
import functools
import math

import numpy as np
import jax
import jax.numpy as jnp
from jax import lax
from jax.experimental import pallas as pl
from jax.experimental.pallas import tpu as pltpu

F32 = jnp.float32
BF16 = jnp.bfloat16

D_MODEL = 1024
N_HEADS = 8
HEAD_DIM = 64
N_PAIRS = N_HEADS // 2
PAIR_W = 2 * HEAD_DIM
DILATED_PAIRS = ((128, 1), (512, 4), (2048, 16))
Q_LORA = 256
KV_LORA = 128
NOPE = 64
ROPE = 32
ROPE_HALF = ROPE // 2
ROPE_THETA = 10000.0
N_BUCKETS = 32
MAX_DISTANCE = 2048
Q_BLOCK = 128
N_GROUPS = 4
EXPERTS_PER_GROUP = 4
N_EXPERTS = 16
D_FF = 256
EPS = 1e-6
LOG2E = math.log2(math.e)
LANES = 128

VMEM_LIMIT_BYTES = 56 * 1024 * 1024

FRONT_TM = 512
MERGE_TM = 512
MOE_TM = 512
MOE_CHUNK = 128
MOE_ROWS = MOE_TM + (N_GROUPS - 1) * MOE_CHUNK
GROUP_LANE = N_EXPERTS
FLASH_T = 256

C_A = 0
C_B = 1536
C_CQ = 3072
C_CKV = C_CQ + Q_LORA
C_KRA = C_CKV + KV_LORA
C_KRB = C_KRA + LANES
C_HF = C_KRB + LANES
N_FRONT = C_HF + LANES
HF_ROWS = 16
GROUP_SLAB = 256


def _nt_dot(a, b):
    return lax.dot_general(a, b, (((1,), (1,)), ((), ())), preferred_element_type=F32)


def _dot(a, b):
    return jnp.dot(a, b, preferred_element_type=F32)


def _log_sigmoid(x):
    return jnp.minimum(x, 0.0) - jnp.log1p(jnp.exp(-jnp.abs(x)))


def _split_hi_lo(x):
    hi = x.astype(BF16)
    lo = (x - hi.astype(F32)).astype(BF16)
    return hi, lo


def _front_kernel(x_ref, gmix_ref, wab_ref, ws_ref, wft_ref, wuq_ref, wukv_ref, g64_ref, g32_ref, tri_ref,
                  g512_ref, g128_ref, ncq_ref, nckv_ref, bfrow_ref, ca_ref, sb_ref,
                  qa_ref, ka_ref, va_ref, qb_ref, kb_ref, vb_ref, qc_ref, kc_ref, vc_ref,
                  ccol_ref, crow_ref, carry_col, carry_row, *, tiles_per_seq):
    i = pl.program_id(0)

    @pl.when(i % tiles_per_seq == 0)
    def _():
        carry_col[...] = jnp.zeros_like(carry_col)
        carry_row[...] = jnp.zeros_like(carry_row)

    x = x_ref[...]
    ms = jnp.mean(x * x, axis=-1, keepdims=True)
    xn = (x * lax.rsqrt(ms + EPS) * gmix_ref[...]).astype(BF16)

    def group_sums(h, gmat):
        sq = (h * h).astype(BF16)
        slab = gmat.shape[0]
        return jnp.concatenate([_dot(sq[:, i:i + slab], gmat) for i in range(0, h.shape[1], slab)], axis=-1)

    def group_norm(h, gmat, n, gain):
        return h * lax.rsqrt(group_sums(h, gmat) * (1.0 / n) + EPS) * gain

    def store_pairs(ref, val, lo=0, w=PAIR_W):
        for p in range(N_PAIRS):
            ref[p, :, lo:lo + w] = val[:, p * PAIR_W:(p + 1) * PAIR_W].astype(BF16)

    g64 = g64_ref[...]
    for base, q_ref, k_ref, v_ref, row in ((C_A, qa_ref, ka_ref, va_ref, 0), (C_B, qb_ref, kb_ref, vb_ref, 2)):
        hq = _dot(xn, wab_ref[:, base:base + 512])
        store_pairs(q_ref, group_norm(hq, g64, HEAD_DIM, g512_ref[row:row + 1, :]))
        hk = _dot(xn, wab_ref[:, base + 512:base + 1024])
        store_pairs(k_ref, group_norm(hk, g64, HEAD_DIM, g512_ref[row + 1:row + 2, :]))
        hv = _dot(xn, wab_ref[:, base + 1024:base + 1536])
        store_pairs(v_ref, hv)

    hs = _dot(xn, ws_ref[...])
    ca = ca_ref[...]
    sb = sb_ref[...]

    hcq = hs[:, 0:Q_LORA]
    cq = (hcq * lax.rsqrt(jnp.mean(hcq * hcq, axis=-1, keepdims=True) + EPS) * ncq_ref[...]).astype(BF16)
    qc = _dot(cq, wuq_ref[...])
    qn = group_norm(qc[:, 0:512], g64, NOPE, g512_ref[4:5, :])
    store_pairs(qc_ref, qn, 0)
    qra = qc[:, 512:1024]
    qrb = qc[:, 1024:1536]
    rs = lax.rsqrt(group_sums(qra, g32_ref[...]) * (1.0 / ROPE) + EPS)
    ga = g128_ref[0:1, :]
    gb = g128_ref[1:2, :]
    for p in range(N_PAIRS):
        sl = slice(p * PAIR_W, (p + 1) * PAIR_W)
        qr = (qra[:, sl] * ga * ca + qrb[:, sl] * gb * sb) * rs[:, sl]
        qc_ref[p, :, PAIR_W:2 * PAIR_W] = qr.astype(BF16)

    hckv = hs[:, Q_LORA:Q_LORA + KV_LORA]
    ckv = (hckv * lax.rsqrt(jnp.mean(hckv * hckv, axis=-1, keepdims=True) + EPS) * nckv_ref[...]).astype(BF16)
    kv = _dot(ckv, wukv_ref[...])
    store_pairs(kc_ref, group_norm(kv[:, 0:512], g64, NOPE, g512_ref[5:6, :]), 0)
    store_pairs(vc_ref, kv[:, 512:1024])
    kra = hs[:, C_KRA - C_CQ:C_KRA - C_CQ + LANES]
    krb = hs[:, C_KRB - C_CQ:C_KRB - C_CQ + LANES]
    rsk = lax.rsqrt(_dot((kra * kra).astype(BF16), g32_ref[0:LANES, 0:LANES]) * (1.0 / ROPE) + EPS)
    kr = ((kra * g128_ref[2:3, :] * ca + krb * g128_ref[3:4, :] * sb) * rsk).astype(BF16)
    for p in range(N_PAIRS):
        kc_ref[p, :, PAIR_W:2 * PAIR_W] = kr

    tri = tri_ref[...]
    hf_col = hs[:, C_HF - C_CQ:C_HF - C_CQ + LANES] + g128_ref[4:5, :]
    hi, lo = _split_hi_lo(_log_sigmoid(hf_col))
    ccol = _dot(tri, hi) + _dot(tri, lo) + carry_col[0:1, :]
    ccol_ref[...] = ccol
    tm = ccol.shape[0]
    carry_col[...] = jnp.broadcast_to(ccol[tm - 1:tm, :], carry_col.shape)

    hf_row = _nt_dot(wft_ref[...], xn) + bfrow_ref[:, 0:1]
    hi, lo = _split_hi_lo(_log_sigmoid(hf_row))
    crow = _nt_dot(hi, tri) + _nt_dot(lo, tri) + carry_row[:, 0:1]
    crow_ref[...] = crow
    carry_row[...] = jnp.broadcast_to(crow[:, tm - 1:tm], carry_row.shape)


def _front_call(x2d, cw, lw, seq_len):
    t = x2d.shape[0]
    tm = FRONT_TM
    const = lambda shape: pl.BlockSpec(shape, lambda i: (0,) * len(shape))
    pair_out = lambda w: pl.BlockSpec((N_PAIRS, tm, w), lambda i: (0, i, 0))
    pair_shape = lambda w: jax.ShapeDtypeStruct((N_PAIRS, t, w), BF16)
    in_specs = [
        pl.BlockSpec((tm, D_MODEL), lambda i: (i, 0)),
        const((1, D_MODEL)),
        const((D_MODEL, C_CQ)),
        const((D_MODEL, N_FRONT - C_CQ)),
        const((HF_ROWS, D_MODEL)),
        const((Q_LORA, 1536)),
        const((KV_LORA, 1024)),
        const((GROUP_SLAB, GROUP_SLAB)),
        const((GROUP_SLAB, GROUP_SLAB)),
        const((tm, tm)),
        const((8, 512)),
        const((8, LANES)),
        const((1, Q_LORA)),
        const((1, KV_LORA)),
        const((HF_ROWS, LANES)),
        pl.BlockSpec((tm, LANES), lambda i: (i, 0)),
        pl.BlockSpec((tm, LANES), lambda i: (i, 0)),
    ]
    out_specs = [pair_out(PAIR_W)] * 6 + [pair_out(2 * PAIR_W), pair_out(2 * PAIR_W), pair_out(PAIR_W),
                                           pl.BlockSpec((tm, LANES), lambda i: (i, 0)),
                                           pl.BlockSpec((HF_ROWS, tm), lambda i: (0, i))]
    out_shape = [pair_shape(PAIR_W)] * 6 + [pair_shape(2 * PAIR_W), pair_shape(2 * PAIR_W), pair_shape(PAIR_W),
                                            jax.ShapeDtypeStruct((t, LANES), F32),
                                            jax.ShapeDtypeStruct((HF_ROWS, t), F32)]
    return pl.pallas_call(
        functools.partial(_front_kernel, tiles_per_seq=seq_len // tm),
        grid=(t // tm,),
        in_specs=in_specs,
        out_specs=out_specs,
        out_shape=out_shape,
        scratch_shapes=[pltpu.VMEM((8, LANES), F32), pltpu.VMEM((HF_ROWS, LANES), F32)],
        compiler_params=pltpu.CompilerParams(dimension_semantics=("arbitrary",),
                                             vmem_limit_bytes=VMEM_LIMIT_BYTES),
        name="front",
    )(x2d, lw["gmix"], lw["wab"], lw["ws"], lw["wft"], lw["wuq"], lw["wukv"], cw["g64"], cw["g32"], cw["tri"],
      lw["g512"], lw["g128"], lw["ncq"], lw["nckv"], lw["bfrow"], cw["ca"], cw["sb"])


def _dilated_kernel(q_ref, k_ref, v_ref, bias_ref, o_ref, oacc, lacc, s_scr, nat, res4, g4, g16, vx, *, seq_len):
    lane = lax.broadcasted_iota(jnp.int32, (1, PAIR_W), 1)
    left = lane < HEAD_DIM
    d_mid, d_far = DILATED_PAIRS[1][1], DILATED_PAIRS[2][1]
    assert DILATED_PAIRS[0][1] == 1 and d_far == d_mid * d_mid
    n_mid, n_far = seq_len // d_mid, seq_len // d_far

    vx[:, :, PAIR_W:2 * PAIR_W] = jnp.ones((len(DILATED_PAIRS), seq_len, PAIR_W), BF16)
    vx[0, :, 0:PAIR_W] = v_ref[...]
    for ti, src in enumerate((q_ref, k_ref, v_ref)):
        nat[ti] = src[...].astype(F32)

    def put(ti, pattern, lo, hi, rows):
        if ti == 2:
            vx[pattern, lo:hi, 0:PAIR_W] = rows.astype(BF16)
        else:
            (g4, g16)[pattern - 1][ti, lo:hi, :] = rows.astype(BF16)

    for ti in range(3):
        for c in range(d_mid):
            rows = nat[ti, pl.ds(c, n_mid, stride=d_mid), :]
            res4[ti, c * n_mid:(c + 1) * n_mid, :] = rows
            put(ti, 1, c * n_mid, (c + 1) * n_mid, rows)
    for ti in range(3):
        for c in range(d_far):
            rows = res4[ti, pl.ds((c % d_mid) * n_mid + c // d_mid, n_far, stride=d_mid), :]
            put(ti, 2, c * n_far, (c + 1) * n_far, rows)

    readers = (
        (1, lambda ti, lo, hi: (q_ref, k_ref)[ti][lo:hi, :]),
        (d_mid, lambda ti, lo, hi: g4[ti, lo:hi, :]),
        (d_far, lambda ti, lo, hi: g16[ti, lo:hi, :]),
    )
    for di, (d, read) in enumerate(readers):
        n_sub = seq_len // d
        n_blocks = n_sub // Q_BLOCK
        blocks = [(c, nb) for c in range(d) for nb in range(n_blocks)]

        def window(c, nb):
            k_lo = max(nb - 1, 0) * Q_BLOCK
            return c * n_sub + k_lo, c * n_sub + (nb + 1) * Q_BLOCK

        for bi, (c, nb) in enumerate(blocks):
            q = read(0, c * n_sub + nb * Q_BLOCK, c * n_sub + (nb + 1) * Q_BLOCK)
            kwin = read(1, *window(c, nb))
            w = kwin.shape[0]
            for j in range(2):
                qh = jnp.where(left if j == 0 else jnp.logical_not(left), q, jnp.zeros_like(q))
                s_scr[j, bi * Q_BLOCK:(bi + 1) * Q_BLOCK, 0:w] = (
                    _nt_dot(qh, kwin) + bias_ref[di, j, :, 2 * Q_BLOCK - w:])
        for bi, (c, nb) in enumerate(blocks):
            lo, hi = window(c, nb)
            w = hi - lo
            outs, lses = [], []
            for j in range(2):
                s = s_scr[j, bi * Q_BLOCK:(bi + 1) * Q_BLOCK, 0:w]
                m = jnp.max(s, axis=-1, keepdims=True)
                p = jnp.exp2(s - m).astype(BF16)
                res = _dot(p, vx[di, lo:hi, :])
                den = res[:, PAIR_W:2 * PAIR_W]
                outs.append(res[:, 0:PAIR_W] / den)
                lses.append(m + jnp.log2(den))
            start = nb * Q_BLOCK * d + c
            rows = pl.ds(start, Q_BLOCK) if d == 1 else pl.ds(start, Q_BLOCK, stride=d)
            oacc[di, rows, :] = jnp.where(left, outs[0], outs[1])
            lacc[di, rows, :] = jnp.where(left, lses[0], lses[1])

    l0, l1, l2 = lacc[0], lacc[1], lacc[2]
    m = jnp.maximum(jnp.maximum(l0, l1), l2)
    w0, w1, w2 = jnp.exp2(l0 - m), jnp.exp2(l1 - m), jnp.exp2(l2 - m)
    wsum = w0 + w1 + w2
    o = oacc[0] * (w0 / wsum) + oacc[1] * (w1 / wsum) + oacc[2] * (w2 / wsum)
    o_ref[...] = o.astype(BF16)


def _dilated_call(q, k, v, bias, batch, seq_len):
    n_pat = len(DILATED_PAIRS)
    seq_block = pl.BlockSpec((None, None, seq_len, PAIR_W), lambda p, b: (p, b, 0, 0))
    view = lambda a: a.reshape(N_PAIRS, batch, seq_len, PAIR_W)
    seq_scratch = lambda n, dt: pltpu.VMEM((n, seq_len, PAIR_W), dt)
    out = pl.pallas_call(
        functools.partial(_dilated_kernel, seq_len=seq_len),
        grid=(N_PAIRS, batch),
        in_specs=[seq_block, seq_block, seq_block,
                  pl.BlockSpec((n_pat, 2, Q_BLOCK, 2 * Q_BLOCK), lambda p, b: (0, p, 0, 0))],
        out_specs=seq_block,
        out_shape=jax.ShapeDtypeStruct((N_PAIRS, batch, seq_len, PAIR_W), BF16),
        scratch_shapes=[seq_scratch(n_pat, F32), seq_scratch(n_pat, F32),
                        pltpu.VMEM((2, seq_len, 2 * Q_BLOCK), F32),
                        seq_scratch(3, F32), seq_scratch(3, F32), seq_scratch(2, BF16), seq_scratch(2, BF16),
                        pltpu.VMEM((n_pat, seq_len, 2 * PAIR_W), BF16)],
        compiler_params=pltpu.CompilerParams(dimension_semantics=("arbitrary", "arbitrary"),
                                             vmem_limit_bytes=VMEM_LIMIT_BYTES),
        name="dilated",
    )(view(q), view(k), view(v), bias)
    return out.reshape(N_PAIRS, batch * seq_len, PAIR_W)


def _flash_kernel(*refs, kw, decay, seq_len):
    if decay:
        q_ref, k_ref, v_ref, crow_ref, ccol_ref, o_ref, s_scr, p_scr, v_scr = refs
    else:
        q_ref, k_ref, v_ref, o_ref, s_scr, p_scr, v_scr = refs
    hp = pl.program_id(0)
    t = FLASH_T
    lane = lax.broadcasted_iota(jnp.int32, (1, kw), 1)
    in_h0 = (lane < HEAD_DIM) | ((lane >= PAIR_W) & (lane < PAIR_W + ROPE))
    in_h1 = ((lane >= HEAD_DIM) & (lane < PAIR_W)) | ((lane >= PAIR_W + ROPE) & (lane < PAIR_W + 2 * ROPE))
    row = lax.broadcasted_iota(jnp.int32, (t, t), 0)
    col = lax.broadcasted_iota(jnp.int32, (t, t), 1)
    causal = col <= row
    out_lane = lax.broadcasted_iota(jnp.int32, (1, PAIR_W), 1)
    v_scr[:, 0:PAIR_W] = v_ref[...]
    v_scr[:, PAIR_W:2 * PAIR_W] = jnp.ones((seq_len, PAIR_W), BF16)
    n_tiles = seq_len // t
    order = [n_tiles - 1 - i // 2 if i % 2 == 0 else i // 2 for i in range(n_tiles)]
    for step, qi in enumerate(order):
        rows = slice(qi * t, (qi + 1) * t)
        q = q_ref[rows, :]
        width = (qi + 1) * t
        outs = []
        for j, sel in enumerate((in_h0, in_h1)):
            buf = 2 * (step % 2) + j
            qh = jnp.where(sel, q, jnp.zeros_like(q))
            mx = None
            for c in range(qi + 1):
                cols = slice(c * t, (c + 1) * t)
                s = _nt_dot(qh, k_ref[cols, :])
                if decay:
                    s = s - crow_ref[pl.ds(2 * hp + j, 1), cols] * LOG2E
                if c == qi:
                    s = jnp.where(causal, s, -jnp.inf)
                s_scr[buf, :, cols] = s
                half = jnp.maximum(s[:, 0:t // 2], s[:, t // 2:t])
                mx = half if mx is None else jnp.maximum(mx, half)
            shift = jnp.max(mx, axis=-1, keepdims=True)
            if decay:
                head = (2 * hp + j).astype(F32)
                cc = ccol_ref[rows, :]
                lane_c = lax.broadcasted_iota(jnp.int32, cc.shape, 1).astype(F32)
                cq = jnp.sum(jnp.where(lane_c == head, cc, 0.0), axis=-1, keepdims=True) * LOG2E
                shift = (shift + cq) - cq
            for c in range(qi + 1):
                cols = slice(c * t, (c + 1) * t)
                p_scr[buf, :, cols] = jnp.exp2(s_scr[buf, :, cols] - shift).astype(BF16)
            res = _dot(p_scr[buf, :, 0:width], v_scr[0:width, :])
            outs.append(res[:, 0:PAIR_W] / res[:, PAIR_W:2 * PAIR_W])
        o_ref[rows, :] = jnp.where(out_lane < HEAD_DIM, outs[0], outs[1]).astype(BF16)


def _flash_call(q, k, v, batch, seq_len, crow=None, ccol=None):
    kw = q.shape[-1]
    decay = crow is not None
    view = lambda a: a.reshape(N_PAIRS, batch, seq_len, a.shape[-1])
    args = [view(q), view(k), view(v)]
    seq_block = lambda w: pl.BlockSpec((None, None, seq_len, w), lambda p, b: (p, b, 0, 0))
    in_specs = [seq_block(kw), seq_block(kw), seq_block(PAIR_W)]
    if decay:
        args += [crow, ccol]
        in_specs += [pl.BlockSpec((HF_ROWS, seq_len), lambda p, b: (0, b)),
                     pl.BlockSpec((seq_len, LANES), lambda p, b: (b, 0))]
    out = pl.pallas_call(
        functools.partial(_flash_kernel, kw=kw, decay=decay, seq_len=seq_len),
        grid=(N_PAIRS, batch),
        in_specs=in_specs,
        out_specs=seq_block(PAIR_W),
        out_shape=jax.ShapeDtypeStruct((N_PAIRS, batch, seq_len, PAIR_W), BF16),
        scratch_shapes=[pltpu.VMEM((4, FLASH_T, seq_len), F32), pltpu.VMEM((4, FLASH_T, seq_len), BF16),
                        pltpu.VMEM((seq_len, 2 * PAIR_W), BF16)],
        compiler_params=pltpu.CompilerParams(dimension_semantics=("arbitrary",) * 2,
                                             vmem_limit_bytes=VMEM_LIMIT_BYTES),
        name="flash_fox" if decay else "flash_mla",
    )(*args)
    return out.reshape(N_PAIRS, batch * seq_len, PAIR_W)


def _merge_kernel(x_ref, oa_ref, ob_ref, oc_ref, gmix_ref, wg_ref, wb_ref, wo_ref, gffn_ref,
                  wr_ref, br_ref, x1_ref, gates_ref):
    x = x_ref[...]
    ms = jnp.mean(x * x, axis=-1, keepdims=True)
    xn = (x * lax.rsqrt(ms + EPS) * gmix_ref[...]).astype(BF16)
    mixed = None
    for g, o_ref in enumerate((oa_ref, ob_ref, oc_ref)):
        o = jnp.concatenate([o_ref[p] for p in range(N_PAIRS)], axis=-1)
        gate = jax.nn.sigmoid(_dot(xn, wg_ref[:, g * D_MODEL:(g + 1) * D_MODEL]))
        term = gate * _dot(o, wb_ref[g])
        mixed = term if mixed is None else mixed + term
    x1 = x + _dot(mixed.astype(BF16), wo_ref[...])
    x1_ref[...] = x1

    ms1 = jnp.mean(x1 * x1, axis=-1, keepdims=True)
    xf = x1 * lax.rsqrt(ms1 + EPS) * gffn_ref[...]
    hi, lo = _split_hi_lo(xf)
    tm = xf.shape[0]
    parts = _dot(jnp.concatenate([hi, lo], axis=0), wr_ref[...])
    lg = (parts[0:tm, 0:LANES] + parts[0:tm, LANES:2 * LANES]
          + parts[tm:2 * tm, 0:LANES] + parts[tm:2 * tm, LANES:2 * LANES]) + br_ref[...]
    lane = lax.broadcasted_iota(jnp.int32, lg.shape, 1)
    lane_f = lane.astype(F32)
    neg = -jnp.inf
    far = float(LANES)
    gl = jnp.where((lane >= N_EXPERTS) & (lane < N_EXPERTS + N_GROUPS), lg, neg)
    gmax = jnp.max(gl, axis=-1, keepdims=True)
    pg_top = 1.0 / jnp.sum(jnp.exp(gl - gmax), axis=-1, keepdims=True)
    gidx = jnp.min(jnp.where(gl == gmax, lane_f, far), axis=-1, keepdims=True) - float(N_EXPERTS)
    in_group = (lane < N_EXPERTS) & ((lane // EXPERTS_PER_GROUP).astype(F32) == gidx)
    ev = jnp.where(in_group, lg, neg)
    v1 = jnp.max(ev, axis=-1, keepdims=True)
    i1 = jnp.min(jnp.where(ev == v1, lane_f, far), axis=-1, keepdims=True)
    ev2 = jnp.where(lane_f == i1, neg, ev)
    v2 = jnp.max(ev2, axis=-1, keepdims=True)
    i2 = jnp.min(jnp.where(ev2 == v2, lane_f, far), axis=-1, keepdims=True)
    e2 = jnp.exp(v2 - v1)
    den = 1.0 + e2
    w1 = (1.0 / den) * pg_top
    w2 = (e2 / den) * pg_top
    gates_ref[...] = jnp.where(lane_f == i1, w1, jnp.where(lane_f == i2, w2,
                                                           jnp.where(lane == GROUP_LANE, gidx, 0.0)))


def _merge_call(x2d, oa, ob, oc, lw):
    t = x2d.shape[0]
    tm = MERGE_TM
    const = lambda shape: pl.BlockSpec(shape, lambda i: (0,) * len(shape))
    row_tile = lambda w: pl.BlockSpec((tm, w), lambda i: (i, 0))
    pair_in = pl.BlockSpec((N_PAIRS, tm, PAIR_W), lambda i: (0, i, 0))
    return pl.pallas_call(
        _merge_kernel,
        grid=(t // tm,),
        in_specs=[row_tile(D_MODEL), pair_in, pair_in, pair_in, const((1, D_MODEL)),
                  const((D_MODEL, 3 * D_MODEL)), const((3, 512, D_MODEL)), const((D_MODEL, D_MODEL)),
                  const((1, D_MODEL)), const((D_MODEL, 2 * LANES)), const((1, LANES))],
        out_specs=[row_tile(D_MODEL), row_tile(LANES)],
        out_shape=[jax.ShapeDtypeStruct((t, D_MODEL), F32), jax.ShapeDtypeStruct((t, LANES), F32)],
        compiler_params=pltpu.CompilerParams(dimension_semantics=("arbitrary",),
                                             vmem_limit_bytes=VMEM_LIMIT_BYTES),
        name="merge",
    )(x2d, oa, ob, oc, lw["gmix"], lw["wg"], lw["wb"], lw["wo"], lw["gffn"], lw["wr"], lw["br"])


def _moe_kernel(x_ref, gates_ref, gffn_ref, lstrict_ref, ustrict_ref, wg_ref, wu_ref, wd_ref, o_ref,
                xs_ref, gs_ref, ys_ref):
    x = x_ref[...]
    tm = x.shape[0]
    ms = jnp.mean(x * x, axis=-1, keepdims=True)
    xn = (x * lax.rsqrt(ms + EPS) * gffn_ref[...]).astype(BF16)

    gates = gates_ref[...]
    lane = lax.broadcasted_iota(jnp.int32, gates.shape, 1)
    gid = jnp.sum(jnp.where(lane == GROUP_LANE, gates, 0.0), axis=-1, keepdims=True)
    onehot = jnp.where((lane.astype(F32) == gid) & (lane < N_GROUPS), 1.0, 0.0)
    before = _dot(lstrict_ref[...], onehot.astype(BF16))
    counts = before[tm - 1:tm, :] + onehot[tm - 1:tm, :]
    n_chunks = jnp.floor((counts + (MOE_CHUNK - 1.0)) * (1.0 / MOE_CHUNK))
    seg_start = _dot(jnp.broadcast_to(n_chunks, (8, LANES)).astype(BF16), ustrict_ref[...])[0:1, :] * MOE_CHUNK
    rank = jnp.sum(onehot * (before + seg_start), axis=-1, keepdims=True)
    rank_row = jnp.transpose(jnp.broadcast_to(rank, (tm, LANES)))[0:1, :]
    slot_col = lax.broadcasted_iota(jnp.int32, (MOE_ROWS, 1), 0).astype(F32)
    slot_row = lax.broadcasted_iota(jnp.int32, (1, MOE_ROWS), 1).astype(F32)
    perm = jnp.where(slot_col == rank_row, 1.0, 0.0).astype(BF16)
    perm_t = jnp.where(rank == slot_row, 1.0, 0.0).astype(BF16)

    xs_ref[...] = _dot(perm, xn).astype(BF16)
    g_hi, g_lo = _split_hi_lo(gates)
    g_sorted = _dot(perm, jnp.concatenate([g_hi, g_lo], axis=-1))
    gs_ref[...] = g_sorted[:, 0:LANES] + g_sorted[:, LANES:2 * LANES]
    ys_ref[...] = jnp.zeros_like(ys_ref)

    nc = [n_chunks[0, g].astype(jnp.int32) for g in range(N_GROUPS)]
    ends = [nc[0], nc[0] + nc[1], nc[0] + nc[1] + nc[2]]
    total = ends[2] + nc[3]

    def chunk(k, carry):
        g = sum((k >= e).astype(jnp.int32) for e in ends)
        r0 = pl.multiple_of(k * MOE_CHUNK, MOE_CHUNK)
        xc = xs_ref[pl.ds(r0, MOE_CHUNK), :]
        gc = gs_ref[pl.ds(r0, MOE_CHUNK), :]
        lane_c = lax.broadcasted_iota(jnp.int32, gc.shape, 1)
        parts = []
        for e in range(EXPERTS_PER_GROUP):
            h_gate = _dot(xc, wg_ref[g * EXPERTS_PER_GROUP + e])
            h_up = _dot(xc, wu_ref[g * EXPERTS_PER_GROUP + e])
            gate = jnp.sum(jnp.where(lane_c == g * EXPERTS_PER_GROUP + e, gc, 0.0), axis=-1, keepdims=True)
            parts.append((jax.nn.silu(h_gate) * h_up * gate).astype(BF16))
        y = sum(_dot(parts[e], wd_ref[g * EXPERTS_PER_GROUP + e]) for e in range(EXPERTS_PER_GROUP))
        ys_ref[pl.ds(r0, MOE_CHUNK), :] = y.astype(BF16)
        return carry

    lax.fori_loop(0, total, chunk, 0)
    o_ref[...] = x + _dot(perm_t, ys_ref[...])


def _moe_call(x1, gates, cw, lw):
    t = x1.shape[0]
    tm = MOE_TM
    const = lambda shape: pl.BlockSpec(shape, lambda i: (0,) * len(shape))
    resident = lambda shape: pl.BlockSpec(shape, lambda i: (0,) * len(shape), pipeline_mode=pl.Buffered(1))
    return pl.pallas_call(
        _moe_kernel,
        grid=(t // tm,),
        in_specs=[pl.BlockSpec((tm, D_MODEL), lambda i: (i, 0)),
                  pl.BlockSpec((tm, LANES), lambda i: (i, 0)),
                  const((1, D_MODEL)), const((tm, tm)), const((LANES, LANES)),
                  resident((N_EXPERTS, D_MODEL, D_FF)), resident((N_EXPERTS, D_MODEL, D_FF)),
                  resident((N_EXPERTS, D_FF, D_MODEL))],
        out_specs=pl.BlockSpec((tm, D_MODEL), lambda i: (i, 0)),
        out_shape=jax.ShapeDtypeStruct((t, D_MODEL), F32),
        scratch_shapes=[pltpu.VMEM((MOE_ROWS, D_MODEL), BF16), pltpu.VMEM((MOE_ROWS, LANES), F32),
                        pltpu.VMEM((MOE_ROWS, D_MODEL), BF16)],
        compiler_params=pltpu.CompilerParams(dimension_semantics=("arbitrary",),
                                             vmem_limit_bytes=VMEM_LIMIT_BYTES),
        name="moe",
    )(x1, gates, lw["gffn"], cw["lstrict"], cw["ustrict"], lw["weg"], lw["weu"], lw["wed"])


def _t5_bucket(dist):
    max_exact = N_BUCKETS // 2
    log_ratio = np.log(np.maximum(dist, max_exact) / max_exact) / np.log(MAX_DISTANCE / max_exact)
    large = np.minimum(max_exact + (log_ratio * (N_BUCKETS - max_exact)).astype(np.int32), N_BUCKETS - 1)
    return np.where(dist < max_exact, dist, large).astype(np.int32)


def _dilated_bias(rel_bias):
    tables = []
    span = 3 * Q_BLOCK
    for window, dil in DILATED_PAIRS:
        assert window // dil == Q_BLOCK
        per_dist = rel_bias[_t5_bucket(np.arange(Q_BLOCK + 1) * dil)].astype(F32).T * LOG2E
        diag = jnp.concatenate([jnp.full((N_HEADS, Q_BLOCK), -jnp.inf, F32), per_dist[:, ::-1],
                                jnp.full((N_HEADS, span - 2 * Q_BLOCK), -jnp.inf, F32)], axis=1)
        skew = jnp.tile(diag, (1, Q_BLOCK))[:, :Q_BLOCK * span].reshape(N_HEADS, Q_BLOCK, span)
        tables.append(skew[:, :, Q_BLOCK:])
    return jnp.stack(tables, axis=0)


def _rope_a(v1, v2):
    pad = jnp.zeros(v1.shape[:-1] + (LANES - 4 * ROPE_HALF,), v1.dtype)
    return jnp.concatenate([v1, v2, v1, v2, pad], axis=-1)


def _const_weights(positions, rel_bias):
    b, s = positions.shape
    inv_freq = ROPE_THETA ** (-jnp.arange(ROPE_HALF, dtype=F32) / ROPE_HALF)
    ang = positions.astype(F32).reshape(b * s, 1) * inv_freq
    cos, sin = jnp.cos(ang), jnp.sin(ang)
    idx = np.arange(GROUP_SLAB)
    return {
        "ca": _rope_a(cos, cos),
        "sb": _rope_a(-sin, sin),
        "g64": jnp.asarray(idx[:, None] // HEAD_DIM == idx[None, :] // HEAD_DIM, BF16),
        "g32": jnp.asarray(idx[:, None] // ROPE == idx[None, :] // ROPE, BF16),
        "tri": jnp.asarray(np.arange(FRONT_TM)[None, :] <= np.arange(FRONT_TM)[:, None], BF16),
        "lstrict": jnp.asarray(np.arange(MOE_TM)[None, :] < np.arange(MOE_TM)[:, None], BF16),
        "ustrict": jnp.asarray(np.arange(LANES)[:, None] < np.arange(LANES)[None, :], BF16),
        "bias": _dilated_bias(rel_bias),
    }


def _layer_weights(l, p):
    w_in = p["w_in"][l]
    o_hf = 3072
    o_cq = o_hf + N_HEADS
    o_ckv = o_cq + Q_LORA
    o_kr = o_ckv + KV_LORA
    o_g = o_kr + ROPE
    kr1 = w_in[:, o_kr:o_kr + ROPE_HALF]
    kr2 = w_in[:, o_kr + ROPE_HALF:o_kr + ROPE]
    hf_w = w_in[:, o_hf:o_cq]
    w_small = jnp.concatenate([
        w_in[:, o_cq:o_ckv], w_in[:, o_ckv:o_kr], _rope_a(kr1, kr2), _rope_a(kr2, kr1),
        jnp.pad(hf_w, ((0, 0), (0, LANES - N_HEADS))),
    ], axis=1).astype(BF16)
    wft = jnp.pad(hf_w.T, ((0, HF_ROWS - N_HEADS), (0, 0))).astype(BF16)

    wuq = p["w_uq"][l].reshape(Q_LORA, N_HEADS, NOPE + ROPE)
    q_nope = wuq[:, :, :NOPE].reshape(Q_LORA, N_HEADS * NOPE)
    q1 = wuq[:, :, NOPE:NOPE + ROPE_HALF]
    q2 = wuq[:, :, NOPE + ROPE_HALF:]

    def pair_rope(v1, v2):
        x = jnp.concatenate([v1, v2], axis=-1).reshape(Q_LORA, N_PAIRS, 2 * ROPE)
        return jnp.pad(x, ((0, 0), (0, 0), (0, LANES - 2 * ROPE))).reshape(Q_LORA, N_PAIRS * LANES)

    wuq_p = jnp.concatenate([q_nope, pair_rope(q1, q2), pair_rope(q2, q1)], axis=1).astype(BF16)
    wukv = p["w_ukv"][l].reshape(KV_LORA, N_HEADS, NOPE + HEAD_DIM)
    wukv_p = jnp.concatenate([wukv[:, :, :NOPE].reshape(KV_LORA, 512),
                              wukv[:, :, NOPE:].reshape(KV_LORA, 512)], axis=1).astype(BF16)

    tile8 = lambda g: jnp.tile(g, N_HEADS)
    sc_ab = LOG2E / math.sqrt(HEAD_DIM)
    sc_c = LOG2E / math.sqrt(NOPE + ROPE)
    gq_c, gk_c = p["gq_c"][l], p["gk_c"][l]
    zeros512 = jnp.zeros((512,), F32)
    g512 = jnp.stack([tile8(p["gq_a"][l]) * sc_ab, tile8(p["gk_a"][l]),
                      tile8(p["gq_b"][l]) * sc_ab, tile8(p["gk_b"][l]),
                      tile8(gq_c[:NOPE]) * sc_c, tile8(gk_c[:NOPE]), zeros512, zeros512])
    gq1, gq2 = gq_c[NOPE:NOPE + ROPE_HALF] * sc_c, gq_c[NOPE + ROPE_HALF:] * sc_c
    gk1, gk2 = gk_c[NOPE:NOPE + ROPE_HALF], gk_c[NOPE + ROPE_HALF:]
    zeros128 = jnp.zeros((LANES,), F32)
    bf = p["b_forget"][l].astype(F32)
    g128 = jnp.stack([_rope_a(gq1, gq2), _rope_a(gq2, gq1), _rope_a(gk1, gk2), _rope_a(gk2, gk1),
                      jnp.pad(bf, (0, LANES - N_HEADS)), zeros128, zeros128, zeros128])
    bfrow = jnp.broadcast_to(jnp.pad(bf, (0, HF_ROWS - N_HEADS))[:, None], (HF_ROWS, LANES))

    wr = jnp.concatenate([p["w_router_expert"][l], p["w_router_group"][l]], axis=1)
    wr = jnp.pad(wr, ((0, 0), (0, LANES - N_EXPERTS - N_GROUPS))).astype(F32)
    wrh = wr.astype(BF16)
    wrl = (wr - wrh.astype(F32)).astype(BF16)
    br = jnp.pad(jnp.concatenate([p["b_router_expert"][l], p["b_router_group"][l]]).astype(F32),
                 (0, LANES - N_EXPERTS - N_GROUPS))[None, :]
    return {
        "gmix": p["norm_mix"][l][None, :], "wab": w_in[:, 0:C_CQ].astype(BF16), "ws": w_small, "wft": wft, "wuq": wuq_p, "wukv": wukv_p,
        "g512": g512, "g128": g128, "ncq": p["norm_cq"][l][None, :], "nckv": p["norm_ckv"][l][None, :],
        "bfrow": bfrow,
        "wg": w_in[:, o_g:].astype(BF16), "wb": p["w_branch"][l].astype(BF16), "wo": p["w_out"][l].astype(BF16),
        "gffn": p["norm_ffn"][l][None, :], "wr": jnp.concatenate([wrh, wrl], axis=1), "br": br,
        "weg": p["w_expert_gate"][l].astype(BF16), "weu": p["w_expert_up"][l].astype(BF16),
        "wed": p["w_expert_down"][l].astype(BF16),
    }


def kernel(x, positions, rel_bias, norm_mix, w_in, b_forget, gq_a, gk_a, gq_b, gk_b, gq_c, gk_c, norm_cq, norm_ckv, w_uq, w_ukv, w_branch, w_out, norm_ffn, w_router_group, b_router_group, w_router_expert, b_router_expert, w_expert_gate, w_expert_up, w_expert_down):
    batch, seq_len, d_model = x.shape
    assert d_model == D_MODEL and seq_len % (Q_BLOCK * DILATED_PAIRS[-1][1]) == 0
    assert seq_len == DILATED_PAIRS[-1][0], "the widest dilated window is assumed to span the sequence"
    p = dict(norm_mix=norm_mix, w_in=w_in, b_forget=b_forget, gq_a=gq_a, gk_a=gk_a, gq_b=gq_b, gk_b=gk_b,
             gq_c=gq_c, gk_c=gk_c, norm_cq=norm_cq, norm_ckv=norm_ckv, w_uq=w_uq, w_ukv=w_ukv,
             w_branch=w_branch, w_out=w_out, norm_ffn=norm_ffn, w_router_group=w_router_group,
             b_router_group=b_router_group, w_router_expert=w_router_expert, b_router_expert=b_router_expert,
             w_expert_gate=w_expert_gate, w_expert_up=w_expert_up, w_expert_down=w_expert_down)
    cw = _const_weights(positions, rel_bias)
    xs = x.reshape(batch * seq_len, d_model)
    for l in range(norm_mix.shape[0]):
        lw = _layer_weights(l, p)
        qa, ka, va, qb, kb, vb, qc, kc, vc, ccol, crow = _front_call(xs, cw, lw, seq_len)
        oa = _dilated_call(qa, ka, va, cw["bias"], batch, seq_len)
        ob = _flash_call(qb, kb, vb, batch, seq_len, crow, ccol)
        oc = _flash_call(qc, kc, vc, batch, seq_len)
        x1, gates = _merge_call(xs, oa, ob, oc, lw)
        xs = _moe_call(x1, gates, cw, lw)
    return xs.reshape(batch, seq_len, d_model)
```

```python
import functools
import math

import numpy as np
import jax
import jax.numpy as jnp
from jax import lax
from jax.experimental import pallas as pl
from jax.experimental.pallas import tpu as pltpu

F32 = jnp.float32
BF16 = jnp.bfloat16

D_MODEL = 1024
N_HEADS = 8
HEAD_DIM = 64
N_PAIRS = N_HEADS // 2
PAIR_W = 2 * HEAD_DIM
DILATED_PAIRS = ((128, 1), (512, 4), (2048, 16))
Q_LORA = 256
KV_LORA = 128
NOPE = 64
ROPE = 32
ROPE_HALF = ROPE // 2
ROPE_THETA = 10000.0
N_BUCKETS = 32
MAX_DISTANCE = 2048
Q_BLOCK = 128
N_GROUPS = 4
EXPERTS_PER_GROUP = 4
N_EXPERTS = 16
D_FF = 256
EPS = 1e-6
LOG2E = math.log2(math.e)
LANES = 128

VMEM_LIMIT_BYTES = 56 * 1024 * 1024

FRONT_TM = 512
MERGE_TM = 512
MOE_TM = 512
MOE_CHUNK = 128
MOE_ROWS = MOE_TM + (N_GROUPS - 1) * MOE_CHUNK
GROUP_LANE = N_EXPERTS
FLASH_T = 256

C_A = 0
C_B = 1536
C_CQ = 3072
C_CKV = C_CQ + Q_LORA
C_KRA = C_CKV + KV_LORA
C_KRB = C_KRA + LANES
C_HF = C_KRB + LANES
N_FRONT = C_HF + LANES
HF_ROWS = 16
GROUP_SLAB = 256


def _nt_dot(a, b):
    return lax.dot_general(a, b, (((1,), (1,)), ((), ())), preferred_element_type=F32)


def _dot(a, b):
    return jnp.dot(a, b, preferred_element_type=F32)


def _log_sigmoid(x):
    return jnp.minimum(x, 0.0) - jnp.log1p(jnp.exp(-jnp.abs(x)))


def _split_hi_lo(x):
    hi = x.astype(BF16)
    lo = (x - hi.astype(F32)).astype(BF16)
    return hi, lo


def _rope_a(v1, v2):
    pad = jnp.zeros(v1.shape[:-1] + (LANES - 4 * ROPE_HALF,), v1.dtype)
    return jnp.concatenate([v1, v2, v1, v2, pad], axis=-1)


def _front_kernel(x_ref, gmix_ref, wab_ref, ws_ref, wft_ref, wuq_ref, wukv_ref, g64_ref, g32_ref, tri_ref,
                  g512_ref, g128_ref, ncq_ref, nckv_ref, bfrow_ref, cos_ref, sin_ref,
                  qa_ref, ka_ref, va_ref, qb_ref, kb_ref, vb_ref, qc_ref, kc_ref, vc_ref,
                  ccol_ref, crow_ref, carry_col, carry_row, *, tiles_per_seq):
    i = pl.program_id(0)

    @pl.when(i % tiles_per_seq == 0)
    def _():
        carry_col[...] = jnp.zeros_like(carry_col)
        carry_row[...] = jnp.zeros_like(carry_row)

    x = x_ref[...]
    ms = jnp.mean(x * x, axis=-1, keepdims=True)
    xn = (x * lax.rsqrt(ms + EPS) * gmix_ref[...]).astype(BF16)

    def group_sums(h, gmat):
        sq = (h * h).astype(BF16)
        slab = gmat.shape[0]
        return jnp.concatenate([_dot(sq[:, i:i + slab], gmat) for i in range(0, h.shape[1], slab)], axis=-1)

    def group_norm(h, gmat, n, gain):
        return h * lax.rsqrt(group_sums(h, gmat) * (1.0 / n) + EPS) * gain

    def store_pairs(ref, val, lo=0, w=PAIR_W):
        for p in range(N_PAIRS):
            ref[p, :, lo:lo + w] = val[:, p * PAIR_W:(p + 1) * PAIR_W].astype(BF16)

    g64 = g64_ref[...]
    for base, q_ref, k_ref, v_ref, row in ((C_A, qa_ref, ka_ref, va_ref, 0), (C_B, qb_ref, kb_ref, vb_ref, 2)):
        hq = _dot(xn, wab_ref[:, base:base + 512])
        store_pairs(q_ref, group_norm(hq, g64, HEAD_DIM, g512_ref[row:row + 1, :]))
        hk = _dot(xn, wab_ref[:, base + 512:base + 1024])
        store_pairs(k_ref, group_norm(hk, g64, HEAD_DIM, g512_ref[row + 1:row + 2, :]))
        hv = _dot(xn, wab_ref[:, base + 1024:base + 1536])
        store_pairs(v_ref, hv)

    hs = _dot(xn, ws_ref[...])
    cos, sin = cos_ref[...], sin_ref[...]
    ca = _rope_a(cos, cos)
    sb = _rope_a(-sin, sin)

    hcq = hs[:, 0:Q_LORA]
    cq = (hcq * lax.rsqrt(jnp.mean(hcq * hcq, axis=-1, keepdims=True) + EPS) * ncq_ref[...]).astype(BF16)
    qc = _dot(cq, wuq_ref[...])
    qn = group_norm(qc[:, 0:512], g64, NOPE, g512_ref[4:5, :])
    store_pairs(qc_ref, qn, 0)
    qra = qc[:, 512:1024]
    qrb = qc[:, 1024:1536]
    rs = lax.rsqrt(group_sums(qra, g32_ref[...]) * (1.0 / ROPE) + EPS)
    ga = g128_ref[0:1, :]
    gb = g128_ref[1:2, :]
    for p in range(N_PAIRS):
        sl = slice(p * PAIR_W, (p + 1) * PAIR_W)
        qr = (qra[:, sl] * ga * ca + qrb[:, sl] * gb * sb) * rs[:, sl]
        qc_ref[p, :, PAIR_W:2 * PAIR_W] = qr.astype(BF16)

    hckv = hs[:, Q_LORA:Q_LORA + KV_LORA]
    ckv = (hckv * lax.rsqrt(jnp.mean(hckv * hckv, axis=-1, keepdims=True) + EPS) * nckv_ref[...]).astype(BF16)
    kv = _dot(ckv, wukv_ref[...])
    store_pairs(kc_ref, group_norm(kv[:, 0:512], g64, NOPE, g512_ref[5:6, :]), 0)
    store_pairs(vc_ref, kv[:, 512:1024])
    kra = hs[:, C_KRA - C_CQ:C_KRA - C_CQ + LANES]
    krb = hs[:, C_KRB - C_CQ:C_KRB - C_CQ + LANES]
    rsk = lax.rsqrt(_dot((kra * kra).astype(BF16), g32_ref[0:LANES, 0:LANES]) * (1.0 / ROPE) + EPS)
    kr = ((kra * g128_ref[2:3, :] * ca + krb * g128_ref[3:4, :] * sb) * rsk).astype(BF16)
    for p in range(N_PAIRS):
        kc_ref[p, :, PAIR_W:2 * PAIR_W] = kr

    tri = tri_ref[...]
    hf_col = hs[:, C_HF - C_CQ:C_HF - C_CQ + LANES] + g128_ref[4:5, :]
    hi, lo = _split_hi_lo(_log_sigmoid(hf_col))
    ccol = _dot(tri, hi) + _dot(tri, lo) + carry_col[0:1, :]
    ccol_ref[...] = ccol
    tm = ccol.shape[0]
    carry_col[...] = jnp.broadcast_to(ccol[tm - 1:tm, :], carry_col.shape)

    hf_row = _nt_dot(wft_ref[...], xn) + bfrow_ref[:, 0:1]
    hi, lo = _split_hi_lo(_log_sigmoid(hf_row))
    crow = _nt_dot(hi, tri) + _nt_dot(lo, tri) + carry_row[:, 0:1]
    crow_ref[...] = crow
    carry_row[...] = jnp.broadcast_to(crow[:, tm - 1:tm], carry_row.shape)


def _front_call(x2d, cw, lw, seq_len):
    t = x2d.shape[0]
    tm = FRONT_TM
    const = lambda shape: pl.BlockSpec(shape, lambda i: (0,) * len(shape))
    pair_out = lambda w: pl.BlockSpec((N_PAIRS, tm, w), lambda i: (0, i, 0))
    pair_shape = lambda w: jax.ShapeDtypeStruct((N_PAIRS, t, w), BF16)
    in_specs = [
        pl.BlockSpec((tm, D_MODEL), lambda i: (i, 0)),
        const((1, D_MODEL)),
        const((D_MODEL, C_CQ)),
        const((D_MODEL, N_FRONT - C_CQ)),
        const((HF_ROWS, D_MODEL)),
        const((Q_LORA, 1536)),
        const((KV_LORA, 1024)),
        const((GROUP_SLAB, GROUP_SLAB)),
        const((GROUP_SLAB, GROUP_SLAB)),
        const((tm, tm)),
        const((8, 512)),
        const((8, LANES)),
        const((1, Q_LORA)),
        const((1, KV_LORA)),
        const((HF_ROWS, LANES)),
        pl.BlockSpec((tm, ROPE_HALF), lambda i: (i, 0)),
        pl.BlockSpec((tm, ROPE_HALF), lambda i: (i, 0)),
    ]
    out_specs = [pair_out(PAIR_W)] * 6 + [pair_out(2 * PAIR_W), pair_out(2 * PAIR_W), pair_out(PAIR_W),
                                           pl.BlockSpec((tm, LANES), lambda i: (i, 0)),
                                           pl.BlockSpec((HF_ROWS, tm), lambda i: (0, i))]
    out_shape = [pair_shape(PAIR_W)] * 6 + [pair_shape(2 * PAIR_W), pair_shape(2 * PAIR_W), pair_shape(PAIR_W),
                                            jax.ShapeDtypeStruct((t, LANES), F32),
                                            jax.ShapeDtypeStruct((HF_ROWS, t), F32)]
    return pl.pallas_call(
        functools.partial(_front_kernel, tiles_per_seq=seq_len // tm),
        grid=(t // tm,),
        in_specs=in_specs,
        out_specs=out_specs,
        out_shape=out_shape,
        scratch_shapes=[pltpu.VMEM((8, LANES), F32), pltpu.VMEM((HF_ROWS, LANES), F32)],
        compiler_params=pltpu.CompilerParams(dimension_semantics=("arbitrary",),
                                             vmem_limit_bytes=VMEM_LIMIT_BYTES),
        name="front",
    )(x2d, lw["gmix"], lw["wab"], lw["ws"], lw["wft"], lw["wuq"], lw["wukv"], cw["g64"], cw["g32"], cw["tri"],
      lw["g512"], lw["g128"], lw["ncq"], lw["nckv"], lw["bfrow"], cw["cos"], cw["sin"])


def _dilated_kernel(q_ref, k_ref, v_ref, bias_ref, o_ref, oacc, lacc, s_scr, nat, res4, g4, g16, vx, *, seq_len):
    lane = lax.broadcasted_iota(jnp.int32, (1, PAIR_W), 1)
    left = lane < HEAD_DIM
    d_mid, d_far = DILATED_PAIRS[1][1], DILATED_PAIRS[2][1]
    assert DILATED_PAIRS[0][1] == 1 and d_far == d_mid * d_mid
    n_mid, n_far = seq_len // d_mid, seq_len // d_far

    vx[:, :, PAIR_W:2 * PAIR_W] = jnp.ones((len(DILATED_PAIRS), seq_len, PAIR_W), BF16)
    vx[0, :, 0:PAIR_W] = v_ref[...]
    for ti, src in enumerate((q_ref, k_ref, v_ref)):
        nat[ti] = src[...].astype(F32)

    def put(ti, pattern, lo, hi, rows):
        if ti == 2:
            vx[pattern, lo:hi, 0:PAIR_W] = rows.astype(BF16)
        else:
            (g4, g16)[pattern - 1][ti, lo:hi, :] = rows.astype(BF16)

    for ti in range(3):
        for c in range(d_mid):
            rows = nat[ti, pl.ds(c, n_mid, stride=d_mid), :]
            res4[ti, c * n_mid:(c + 1) * n_mid, :] = rows
            put(ti, 1, c * n_mid, (c + 1) * n_mid, rows)
    for ti in range(3):
        for c in range(d_far):
            rows = res4[ti, pl.ds((c % d_mid) * n_mid + c // d_mid, n_far, stride=d_mid), :]
            put(ti, 2, c * n_far, (c + 1) * n_far, rows)

    readers = (
        (1, lambda ti, lo, hi: (q_ref, k_ref)[ti][lo:hi, :]),
        (d_mid, lambda ti, lo, hi: g4[ti, lo:hi, :]),
        (d_far, lambda ti, lo, hi: g16[ti, lo:hi, :]),
    )
    for di, (d, read) in enumerate(readers):
        n_sub = seq_len // d
        n_blocks = n_sub // Q_BLOCK
        blocks = [(c, nb) for c in range(d) for nb in range(n_blocks)]

        def window(c, nb):
            k_lo = max(nb - 1, 0) * Q_BLOCK
            return c * n_sub + k_lo, c * n_sub + (nb + 1) * Q_BLOCK

        for bi, (c, nb) in enumerate(blocks):
            q = read(0, c * n_sub + nb * Q_BLOCK, c * n_sub + (nb + 1) * Q_BLOCK)
            kwin = read(1, *window(c, nb))
            w = kwin.shape[0]
            for j in range(2):
                qh = jnp.where(left if j == 0 else jnp.logical_not(left), q, jnp.zeros_like(q))
                s_scr[j, bi * Q_BLOCK:(bi + 1) * Q_BLOCK, 0:w] = (
                    _nt_dot(qh, kwin) + bias_ref[di, j, :, 2 * Q_BLOCK - w:])
        for bi, (c, nb) in enumerate(blocks):
            lo, hi = window(c, nb)
            w = hi - lo
            outs, lses = [], []
            for j in range(2):
                s = s_scr[j, bi * Q_BLOCK:(bi + 1) * Q_BLOCK, 0:w]
                m = jnp.max(s, axis=-1, keepdims=True)
                p = jnp.exp2(s - m).astype(BF16)
                res = _dot(p, vx[di, lo:hi, :])
                den = res[:, PAIR_W:2 * PAIR_W]
                outs.append(res[:, 0:PAIR_W] / den)
                lses.append(m + jnp.log2(den))
            start = nb * Q_BLOCK * d + c
            rows = pl.ds(start, Q_BLOCK) if d == 1 else pl.ds(start, Q_BLOCK, stride=d)
            oacc[di, rows, :] = jnp.where(left, outs[0], outs[1])
            lacc[di, rows, :] = jnp.where(left, lses[0], lses[1])

    l0, l1, l2 = lacc[0], lacc[1], lacc[2]
    m = jnp.maximum(jnp.maximum(l0, l1), l2)
    w0, w1, w2 = jnp.exp2(l0 - m), jnp.exp2(l1 - m), jnp.exp2(l2 - m)
    wsum = w0 + w1 + w2
    o = oacc[0] * (w0 / wsum) + oacc[1] * (w1 / wsum) + oacc[2] * (w2 / wsum)
    o_ref[...] = o.astype(BF16)


def _dilated_call(q, k, v, bias, batch, seq_len):
    n_pat = len(DILATED_PAIRS)
    seq_block = pl.BlockSpec((None, None, seq_len, PAIR_W), lambda p, b: (p, b, 0, 0))
    view = lambda a: a.reshape(N_PAIRS, batch, seq_len, PAIR_W)
    seq_scratch = lambda n, dt: pltpu.VMEM((n, seq_len, PAIR_W), dt)
    out = pl.pallas_call(
        functools.partial(_dilated_kernel, seq_len=seq_len),
        grid=(N_PAIRS, batch),
        in_specs=[seq_block, seq_block, seq_block,
                  pl.BlockSpec((n_pat, 2, Q_BLOCK, 2 * Q_BLOCK), lambda p, b: (0, p, 0, 0))],
        out_specs=seq_block,
        out_shape=jax.ShapeDtypeStruct((N_PAIRS, batch, seq_len, PAIR_W), BF16),
        scratch_shapes=[seq_scratch(n_pat, F32), seq_scratch(n_pat, F32),
                        pltpu.VMEM((2, seq_len, 2 * Q_BLOCK), F32),
                        seq_scratch(3, F32), seq_scratch(3, F32), seq_scratch(2, BF16), seq_scratch(2, BF16),
                        pltpu.VMEM((n_pat, seq_len, 2 * PAIR_W), BF16)],
        compiler_params=pltpu.CompilerParams(dimension_semantics=("arbitrary", "arbitrary"),
                                             vmem_limit_bytes=VMEM_LIMIT_BYTES),
        name="dilated",
    )(view(q), view(k), view(v), bias)
    return out.reshape(N_PAIRS, batch * seq_len, PAIR_W)


def _flash_kernel(*refs, kw, decay, seq_len):
    if decay:
        q_ref, k_ref, v_ref, crow_ref, ccol_ref, o_ref, s_scr, p_scr, v_scr = refs
    else:
        q_ref, k_ref, v_ref, o_ref, s_scr, p_scr, v_scr = refs
    hp = pl.program_id(0)
    t = FLASH_T
    lane = lax.broadcasted_iota(jnp.int32, (1, kw), 1)
    in_h0 = (lane < HEAD_DIM) | ((lane >= PAIR_W) & (lane < PAIR_W + ROPE))
    in_h1 = ((lane >= HEAD_DIM) & (lane < PAIR_W)) | ((lane >= PAIR_W + ROPE) & (lane < PAIR_W + 2 * ROPE))
    row = lax.broadcasted_iota(jnp.int32, (t, t), 0)
    col = lax.broadcasted_iota(jnp.int32, (t, t), 1)
    causal = col <= row
    out_lane = lax.broadcasted_iota(jnp.int32, (1, PAIR_W), 1)
    v_scr[:, 0:PAIR_W] = v_ref[...]
    v_scr[:, PAIR_W:2 * PAIR_W] = jnp.ones((seq_len, PAIR_W), BF16)
    n_tiles = seq_len // t
    order = [n_tiles - 1 - i // 2 if i % 2 == 0 else i // 2 for i in range(n_tiles)]
    for step, qi in enumerate(order):
        rows = slice(qi * t, (qi + 1) * t)
        q = q_ref[rows, :]
        width = (qi + 1) * t
        outs = []
        for j, sel in enumerate((in_h0, in_h1)):
            buf = 2 * (step % 2) + j
            qh = jnp.where(sel, q, jnp.zeros_like(q))
            mx = None
            for c in range(qi + 1):
                cols = slice(c * t, (c + 1) * t)
                s = _nt_dot(qh, k_ref[cols, :])
                if decay:
                    s = s - crow_ref[pl.ds(2 * hp + j, 1), cols] * LOG2E
                if c == qi:
                    s = jnp.where(causal, s, -jnp.inf)
                s_scr[buf, :, cols] = s
                half = jnp.maximum(s[:, 0:t // 2], s[:, t // 2:t])
                mx = half if mx is None else jnp.maximum(mx, half)
            shift = jnp.max(mx, axis=-1, keepdims=True)
            if decay:
                head = (2 * hp + j).astype(F32)
                cc = ccol_ref[rows, :]
                lane_c = lax.broadcasted_iota(jnp.int32, cc.shape, 1).astype(F32)
                cq = jnp.sum(jnp.where(lane_c == head, cc, 0.0), axis=-1, keepdims=True) * LOG2E
                shift = (shift + cq) - cq
            for c in range(qi + 1):
                cols = slice(c * t, (c + 1) * t)
                p_scr[buf, :, cols] = jnp.exp2(s_scr[buf, :, cols] - shift).astype(BF16)
            res = _dot(p_scr[buf, :, 0:width], v_scr[0:width, :])
            outs.append(res[:, 0:PAIR_W] / res[:, PAIR_W:2 * PAIR_W])
        o_ref[rows, :] = jnp.where(out_lane < HEAD_DIM, outs[0], outs[1]).astype(BF16)


def _flash_call(q, k, v, batch, seq_len, crow=None, ccol=None):
    kw = q.shape[-1]
    decay = crow is not None
    view = lambda a: a.reshape(N_PAIRS, batch, seq_len, a.shape[-1])
    args = [view(q), view(k), view(v)]
    seq_block = lambda w: pl.BlockSpec((None, None, seq_len, w), lambda p, b: (p, b, 0, 0))
    in_specs = [seq_block(kw), seq_block(kw), seq_block(PAIR_W)]
    if decay:
        args += [crow, ccol]
        in_specs += [pl.BlockSpec((HF_ROWS, seq_len), lambda p, b: (0, b)),
                     pl.BlockSpec((seq_len, LANES), lambda p, b: (b, 0))]
    out = pl.pallas_call(
        functools.partial(_flash_kernel, kw=kw, decay=decay, seq_len=seq_len),
        grid=(N_PAIRS, batch),
        in_specs=in_specs,
        out_specs=seq_block(PAIR_W),
        out_shape=jax.ShapeDtypeStruct((N_PAIRS, batch, seq_len, PAIR_W), BF16),
        scratch_shapes=[pltpu.VMEM((4, FLASH_T, seq_len), F32), pltpu.VMEM((4, FLASH_T, seq_len), BF16),
                        pltpu.VMEM((seq_len, 2 * PAIR_W), BF16)],
        compiler_params=pltpu.CompilerParams(dimension_semantics=("arbitrary",) * 2,
                                             vmem_limit_bytes=VMEM_LIMIT_BYTES),
        name="flash_fox" if decay else "flash_mla",
    )(*args)
    return out.reshape(N_PAIRS, batch * seq_len, PAIR_W)


def _merge_kernel(x_ref, oa_ref, ob_ref, oc_ref, gmix_ref, wg_ref, wb_ref, wo_ref, gffn_ref,
                  wr_ref, br_ref, x1_ref, gates_ref):
    x = x_ref[...]
    ms = jnp.mean(x * x, axis=-1, keepdims=True)
    xn = (x * lax.rsqrt(ms + EPS) * gmix_ref[...]).astype(BF16)
    mixed = None
    for g, o_ref in enumerate((oa_ref, ob_ref, oc_ref)):
        o = jnp.concatenate([o_ref[p] for p in range(N_PAIRS)], axis=-1)
        gate = jax.nn.sigmoid(_dot(xn, wg_ref[:, g * D_MODEL:(g + 1) * D_MODEL]))
        term = gate * _dot(o, wb_ref[g])
        mixed = term if mixed is None else mixed + term
    x1 = x + _dot(mixed.astype(BF16), wo_ref[...])
    x1_ref[...] = x1

    ms1 = jnp.mean(x1 * x1, axis=-1, keepdims=True)
    xf = x1 * lax.rsqrt(ms1 + EPS) * gffn_ref[...]
    hi, lo = _split_hi_lo(xf)
    tm = xf.shape[0]
    parts = _dot(jnp.concatenate([hi, lo], axis=0), wr_ref[...])
    lg = (parts[0:tm, 0:LANES] + parts[0:tm, LANES:2 * LANES]
          + parts[tm:2 * tm, 0:LANES] + parts[tm:2 * tm, LANES:2 * LANES]) + br_ref[...]
    lane = lax.broadcasted_iota(jnp.int32, lg.shape, 1)
    lane_f = lane.astype(F32)
    neg = -jnp.inf
    far = float(LANES)
    gl = jnp.where((lane >= N_EXPERTS) & (lane < N_EXPERTS + N_GROUPS), lg, neg)
    gmax = jnp.max(gl, axis=-1, keepdims=True)
    pg_top = 1.0 / jnp.sum(jnp.exp(gl - gmax), axis=-1, keepdims=True)
    gidx = jnp.min(jnp.where(gl == gmax, lane_f, far), axis=-1, keepdims=True) - float(N_EXPERTS)
    in_group = (lane < N_EXPERTS) & ((lane // EXPERTS_PER_GROUP).astype(F32) == gidx)
    ev = jnp.where(in_group, lg, neg)
    v1 = jnp.max(ev, axis=-1, keepdims=True)
    i1 = jnp.min(jnp.where(ev == v1, lane_f, far), axis=-1, keepdims=True)
    ev2 = jnp.where(lane_f == i1, neg, ev)
    v2 = jnp.max(ev2, axis=-1, keepdims=True)
    i2 = jnp.min(jnp.where(ev2 == v2, lane_f, far), axis=-1, keepdims=True)
    e2 = jnp.exp(v2 - v1)
    den = 1.0 + e2
    w1 = (1.0 / den) * pg_top
    w2 = (e2 / den) * pg_top
    gates_ref[...] = jnp.where(lane_f == i1, w1, jnp.where(lane_f == i2, w2,
                                                           jnp.where(lane == GROUP_LANE, gidx, 0.0)))


def _merge_call(x2d, oa, ob, oc, lw):
    t = x2d.shape[0]
    tm = MERGE_TM
    const = lambda shape: pl.BlockSpec(shape, lambda i: (0,) * len(shape))
    row_tile = lambda w: pl.BlockSpec((tm, w), lambda i: (i, 0))
    pair_in = pl.BlockSpec((N_PAIRS, tm, PAIR_W), lambda i: (0, i, 0))
    return pl.pallas_call(
        _merge_kernel,
        grid=(t // tm,),
        in_specs=[row_tile(D_MODEL), pair_in, pair_in, pair_in, const((1, D_MODEL)),
                  const((D_MODEL, 3 * D_MODEL)), const((3, 512, D_MODEL)), const((D_MODEL, D_MODEL)),
                  const((1, D_MODEL)), const((D_MODEL, 2 * LANES)), const((1, LANES))],
        out_specs=[row_tile(D_MODEL), row_tile(LANES)],
        out_shape=[jax.ShapeDtypeStruct((t, D_MODEL), F32), jax.ShapeDtypeStruct((t, LANES), F32)],
        compiler_params=pltpu.CompilerParams(dimension_semantics=("arbitrary",),
                                             vmem_limit_bytes=VMEM_LIMIT_BYTES),
        name="merge",
    )(x2d, oa, ob, oc, lw["gmix"], lw["wg"], lw["wb"], lw["wo"], lw["gffn"], lw["wr"], lw["br"])


def _moe_kernel(x_ref, gates_ref, gffn_ref, lstrict_ref, ustrict_ref, wg_ref, wu_ref, wd_ref, o_ref,
                xs_ref, gs_ref, ys_ref):
    x = x_ref[...]
    tm = x.shape[0]
    ms = jnp.mean(x * x, axis=-1, keepdims=True)
    xn = (x * lax.rsqrt(ms + EPS) * gffn_ref[...]).astype(BF16)

    gates = gates_ref[...]
    lane = lax.broadcasted_iota(jnp.int32, gates.shape, 1)
    gid = jnp.sum(jnp.where(lane == GROUP_LANE, gates, 0.0), axis=-1, keepdims=True)
    onehot = jnp.where((lane.astype(F32) == gid) & (lane < N_GROUPS), 1.0, 0.0)
    before = _dot(lstrict_ref[...], onehot.astype(BF16))
    counts = before[tm - 1:tm, :] + onehot[tm - 1:tm, :]
    n_chunks = jnp.floor((counts + (MOE_CHUNK - 1.0)) * (1.0 / MOE_CHUNK))
    seg_start = _dot(jnp.broadcast_to(n_chunks, (8, LANES)).astype(BF16), ustrict_ref[...])[0:1, :] * MOE_CHUNK
    rank = jnp.sum(onehot * (before + seg_start), axis=-1, keepdims=True)
    rank_row = jnp.transpose(jnp.broadcast_to(rank, (tm, LANES)))[0:1, :]
    slot_col = lax.broadcasted_iota(jnp.int32, (MOE_ROWS, 1), 0).astype(F32)
    slot_row = lax.broadcasted_iota(jnp.int32, (1, MOE_ROWS), 1).astype(F32)
    perm = jnp.where(slot_col == rank_row, 1.0, 0.0).astype(BF16)
    perm_t = jnp.where(rank == slot_row, 1.0, 0.0).astype(BF16)

    xs_ref[...] = _dot(perm, xn).astype(BF16)
    g_hi, g_lo = _split_hi_lo(gates)
    g_sorted = _dot(perm, jnp.concatenate([g_hi, g_lo], axis=-1))
    gs_ref[...] = g_sorted[:, 0:LANES] + g_sorted[:, LANES:2 * LANES]
    ys_ref[...] = jnp.zeros_like(ys_ref)

    nc = [n_chunks[0, g].astype(jnp.int32) for g in range(N_GROUPS)]
    ends = [nc[0], nc[0] + nc[1], nc[0] + nc[1] + nc[2]]
    total = ends[2] + nc[3]

    def chunk(k, carry):
        g = sum((k >= e).astype(jnp.int32) for e in ends)
        r0 = pl.multiple_of(k * MOE_CHUNK, MOE_CHUNK)
        xc = xs_ref[pl.ds(r0, MOE_CHUNK), :]
        gc = gs_ref[pl.ds(r0, MOE_CHUNK), :]
        lane_c = lax.broadcasted_iota(jnp.int32, gc.shape, 1)
        parts = []
        for e in range(EXPERTS_PER_GROUP):
            h_gate = _dot(xc, wg_ref[g * EXPERTS_PER_GROUP + e])
            h_up = _dot(xc, wu_ref[g * EXPERTS_PER_GROUP + e])
            gate = jnp.sum(jnp.where(lane_c == g * EXPERTS_PER_GROUP + e, gc, 0.0), axis=-1, keepdims=True)
            parts.append((jax.nn.silu(h_gate) * h_up * gate).astype(BF16))
        y = sum(_dot(parts[e], wd_ref[g * EXPERTS_PER_GROUP + e]) for e in range(EXPERTS_PER_GROUP))
        ys_ref[pl.ds(r0, MOE_CHUNK), :] = y.astype(BF16)
        return carry

    lax.fori_loop(0, total, chunk, 0)
    o_ref[...] = x + _dot(perm_t, ys_ref[...])


def _moe_call(x1, gates, cw, lw):
    t = x1.shape[0]
    tm = MOE_TM
    const = lambda shape: pl.BlockSpec(shape, lambda i: (0,) * len(shape))
    resident = lambda shape: pl.BlockSpec(shape, lambda i: (0,) * len(shape), pipeline_mode=pl.Buffered(1))
    return pl.pallas_call(
        _moe_kernel,
        grid=(t // tm,),
        in_specs=[pl.BlockSpec((tm, D_MODEL), lambda i: (i, 0)),
                  pl.BlockSpec((tm, LANES), lambda i: (i, 0)),
                  const((1, D_MODEL)), const((tm, tm)), const((LANES, LANES)),
                  resident((N_EXPERTS, D_MODEL, D_FF)), resident((N_EXPERTS, D_MODEL, D_FF)),
                  resident((N_EXPERTS, D_FF, D_MODEL))],
        out_specs=pl.BlockSpec((tm, D_MODEL), lambda i: (i, 0)),
        out_shape=jax.ShapeDtypeStruct((t, D_MODEL), F32),
        scratch_shapes=[pltpu.VMEM((MOE_ROWS, D_MODEL), BF16), pltpu.VMEM((MOE_ROWS, LANES), F32),
                        pltpu.VMEM((MOE_ROWS, D_MODEL), BF16)],
        compiler_params=pltpu.CompilerParams(dimension_semantics=("arbitrary",),
                                             vmem_limit_bytes=VMEM_LIMIT_BYTES),
        name="moe",
    )(x1, gates, lw["gffn"], cw["lstrict"], cw["ustrict"], lw["weg"], lw["weu"], lw["wed"])


def _t5_bucket(dist):
    max_exact = N_BUCKETS // 2
    log_ratio = np.log(np.maximum(dist, max_exact) / max_exact) / np.log(MAX_DISTANCE / max_exact)
    large = np.minimum(max_exact + (log_ratio * (N_BUCKETS - max_exact)).astype(np.int32), N_BUCKETS - 1)
    return np.where(dist < max_exact, dist, large).astype(np.int32)


def _dilated_bias(rel_bias):
    tables = []
    span = 3 * Q_BLOCK
    for window, dil in DILATED_PAIRS:
        assert window // dil == Q_BLOCK
        per_dist = rel_bias[_t5_bucket(np.arange(Q_BLOCK + 1) * dil)].astype(F32).T * LOG2E
        diag = jnp.concatenate([jnp.full((N_HEADS, Q_BLOCK), -jnp.inf, F32), per_dist[:, ::-1],
                                jnp.full((N_HEADS, span - 2 * Q_BLOCK), -jnp.inf, F32)], axis=1)
        skew = jnp.tile(diag, (1, Q_BLOCK))[:, :Q_BLOCK * span].reshape(N_HEADS, Q_BLOCK, span)
        tables.append(skew[:, :, Q_BLOCK:])
    return jnp.stack(tables, axis=0)


def _const_weights(positions, rel_bias):
    b, s = positions.shape
    inv_freq = ROPE_THETA ** (-jnp.arange(ROPE_HALF, dtype=F32) / ROPE_HALF)
    ang = positions.astype(F32).reshape(b * s, 1) * inv_freq
    cos, sin = jnp.cos(ang), jnp.sin(ang)
    idx = np.arange(GROUP_SLAB)
    return {
        "cos": cos, "sin": sin,
        "g64": jnp.asarray(idx[:, None] // HEAD_DIM == idx[None, :] // HEAD_DIM, BF16),
        "g32": jnp.asarray(idx[:, None] // ROPE == idx[None, :] // ROPE, BF16),
        "tri": jnp.asarray(np.arange(FRONT_TM)[None, :] <= np.arange(FRONT_TM)[:, None], BF16),
        "lstrict": jnp.asarray(np.arange(MOE_TM)[None, :] < np.arange(MOE_TM)[:, None], BF16),
        "ustrict": jnp.asarray(np.arange(LANES)[:, None] < np.arange(LANES)[None, :], BF16),
        "bias": _dilated_bias(rel_bias),
    }


def _layer_weights(l, p):
    w_in = p["w_in"][l]
    o_hf = 3072
    o_cq = o_hf + N_HEADS
    o_ckv = o_cq + Q_LORA
    o_kr = o_ckv + KV_LORA
    o_g = o_kr + ROPE
    kr1 = w_in[:, o_kr:o_kr + ROPE_HALF]
    kr2 = w_in[:, o_kr + ROPE_HALF:o_kr + ROPE]
    hf_w = w_in[:, o_hf:o_cq]
    w_small = jnp.concatenate([
        w_in[:, o_cq:o_ckv], w_in[:, o_ckv:o_kr], _rope_a(kr1, kr2), _rope_a(kr2, kr1),
        jnp.pad(hf_w, ((0, 0), (0, LANES - N_HEADS))),
    ], axis=1).astype(BF16)
    wft = jnp.pad(hf_w.T, ((0, HF_ROWS - N_HEADS), (0, 0))).astype(BF16)

    wuq = p["w_uq"][l].reshape(Q_LORA, N_HEADS, NOPE + ROPE)
    q_nope = wuq[:, :, :NOPE].reshape(Q_LORA, N_HEADS * NOPE)
    q1 = wuq[:, :, NOPE:NOPE + ROPE_HALF]
    q2 = wuq[:, :, NOPE + ROPE_HALF:]

    def pair_rope(v1, v2):
        x = jnp.concatenate([v1, v2], axis=-1).reshape(Q_LORA, N_PAIRS, 2 * ROPE)
        return jnp.pad(x, ((0, 0), (0, 0), (0, LANES - 2 * ROPE))).reshape(Q_LORA, N_PAIRS * LANES)

    wuq_p = jnp.concatenate([q_nope, pair_rope(q1, q2), pair_rope(q2, q1)], axis=1).astype(BF16)
    wukv = p["w_ukv"][l].reshape(KV_LORA, N_HEADS, NOPE + HEAD_DIM)
    wukv_p = jnp.concatenate([wukv[:, :, :NOPE].reshape(KV_LORA, 512),
                              wukv[:, :, NOPE:].reshape(KV_LORA, 512)], axis=1).astype(BF16)

    tile8 = lambda g: jnp.tile(g, N_HEADS)
    sc_ab = LOG2E / math.sqrt(HEAD_DIM)
    sc_c = LOG2E / math.sqrt(NOPE + ROPE)
    gq_c, gk_c = p["gq_c"][l], p["gk_c"][l]
    zeros512 = jnp.zeros((512,), F32)
    g512 = jnp.stack([tile8(p["gq_a"][l]) * sc_ab, tile8(p["gk_a"][l]),
                      tile8(p["gq_b"][l]) * sc_ab, tile8(p["gk_b"][l]),
                      tile8(gq_c[:NOPE]) * sc_c, tile8(gk_c[:NOPE]), zeros512, zeros512])
    gq1, gq2 = gq_c[NOPE:NOPE + ROPE_HALF] * sc_c, gq_c[NOPE + ROPE_HALF:] * sc_c
    gk1, gk2 = gk_c[NOPE:NOPE + ROPE_HALF], gk_c[NOPE + ROPE_HALF:]
    zeros128 = jnp.zeros((LANES,), F32)
    bf = p["b_forget"][l].astype(F32)
    g128 = jnp.stack([_rope_a(gq1, gq2), _rope_a(gq2, gq1), _rope_a(gk1, gk2), _rope_a(gk2, gk1),
                      jnp.pad(bf, (0, LANES - N_HEADS)), zeros128, zeros128, zeros128])
    bfrow = jnp.broadcast_to(jnp.pad(bf, (0, HF_ROWS - N_HEADS))[:, None], (HF_ROWS, LANES))

    wr = jnp.concatenate([p["w_router_expert"][l], p["w_router_group"][l]], axis=1)
    wr = jnp.pad(wr, ((0, 0), (0, LANES - N_EXPERTS - N_GROUPS))).astype(F32)
    wrh = wr.astype(BF16)
    wrl = (wr - wrh.astype(F32)).astype(BF16)
    br = jnp.pad(jnp.concatenate([p["b_router_expert"][l], p["b_router_group"][l]]).astype(F32),
                 (0, LANES - N_EXPERTS - N_GROUPS))[None, :]
    return {
        "gmix": p["norm_mix"][l][None, :], "wab": w_in[:, 0:C_CQ].astype(BF16), "ws": w_small, "wft": wft, "wuq": wuq_p, "wukv": wukv_p,
        "g512": g512, "g128": g128, "ncq": p["norm_cq"][l][None, :], "nckv": p["norm_ckv"][l][None, :],
        "bfrow": bfrow,
        "wg": w_in[:, o_g:].astype(BF16), "wb": p["w_branch"][l].astype(BF16), "wo": p["w_out"][l].astype(BF16),
        "gffn": p["norm_ffn"][l][None, :], "wr": jnp.concatenate([wrh, wrl], axis=1), "br": br,
        "weg": p["w_expert_gate"][l].astype(BF16), "weu": p["w_expert_up"][l].astype(BF16),
        "wed": p["w_expert_down"][l].astype(BF16),
    }


def kernel(x, positions, rel_bias, norm_mix, w_in, b_forget, gq_a, gk_a, gq_b, gk_b, gq_c, gk_c, norm_cq, norm_ckv, w_uq, w_ukv, w_branch, w_out, norm_ffn, w_router_group, b_router_group, w_router_expert, b_router_expert, w_expert_gate, w_expert_up, w_expert_down):
    batch, seq_len, d_model = x.shape
    assert d_model == D_MODEL and seq_len % (Q_BLOCK * DILATED_PAIRS[-1][1]) == 0
    assert seq_len == DILATED_PAIRS[-1][0], "the widest dilated window is assumed to span the sequence"
    p = dict(norm_mix=norm_mix, w_in=w_in, b_forget=b_forget, gq_a=gq_a, gk_a=gk_a, gq_b=gq_b, gk_b=gk_b,
             gq_c=gq_c, gk_c=gk_c, norm_cq=norm_cq, norm_ckv=norm_ckv, w_uq=w_uq, w_ukv=w_ukv,
             w_branch=w_branch, w_out=w_out, norm_ffn=norm_ffn, w_router_group=w_router_group,
             b_router_group=b_router_group, w_router_expert=w_router_expert, b_router_expert=b_router_expert,
             w_expert_gate=w_expert_gate, w_expert_up=w_expert_up, w_expert_down=w_expert_down)
    cw = _const_weights(positions, rel_bias)
    xs = x.reshape(batch * seq_len, d_model)
    for l in range(norm_mix.shape[0]):
        lw = _layer_weights(l, p)
        qa, ka, va, qb, kb, vb, qc, kc, vc, ccol, crow = _front_call(xs, cw, lw, seq_len)
        oa = _dilated_call(qa, ka, va, cw["bias"], batch, seq_len)
        ob = _flash_call(qb, kb, vb, batch, seq_len, crow, ccol)
        oc = _flash_call(qc, kc, vc, batch, seq_len)
        x1, gates = _merge_call(xs, oa, ob, oc, lw)
        xs = _moe_call(x1, gates, cw, lw)
    return xs.reshape(batch, seq_len, d_model)
```

```python
import functools
import math

import numpy as np
import jax
import jax.numpy as jnp
from jax import lax
from jax.experimental import pallas as pl
from jax.experimental.pallas import tpu as pltpu

F32 = jnp.float32
BF16 = jnp.bfloat16

D_MODEL = 1024
N_HEADS = 8
HEAD_DIM = 64
N_PAIRS = N_HEADS // 2
PAIR_W = 2 * HEAD_DIM
DILATED_PAIRS = ((128, 1), (512, 4), (2048, 16))
Q_LORA = 256
KV_LORA = 128
NOPE = 64
ROPE = 32
ROPE_HALF = ROPE // 2
ROPE_THETA = 10000.0
N_BUCKETS = 32
MAX_DISTANCE = 2048
Q_BLOCK = 128
N_GROUPS = 4
EXPERTS_PER_GROUP = 4
N_EXPERTS = 16
D_FF = 256
EPS = 1e-6
LOG2E = math.log2(math.e)
LANES = 128

VMEM_LIMIT_BYTES = 56 * 1024 * 1024

FRONT_TM = 512
MERGE_TM = 1024
MOE_TM = 512
MOE_CHUNK = 128
MOE_ROWS = MOE_TM + (N_GROUPS - 1) * MOE_CHUNK
GROUP_LANE = N_EXPERTS
FLASH_T = 256

C_A = 0
C_B = 1536
C_CQ = 3072
C_CKV = C_CQ + Q_LORA
C_KRA = C_CKV + KV_LORA
C_KRB = C_KRA + LANES
C_HF = C_KRB + LANES
N_FRONT = C_HF + LANES
HF_ROWS = 16
GROUP_SLAB = 256


def _nt_dot(a, b):
    return lax.dot_general(a, b, (((1,), (1,)), ((), ())), preferred_element_type=F32)


def _dot(a, b):
    return jnp.dot(a, b, preferred_element_type=F32)


def _log_sigmoid(x):
    return jnp.minimum(x, 0.0) - jnp.log1p(jnp.exp(-jnp.abs(x)))


def _split_hi_lo(x):
    hi = x.astype(BF16)
    lo = (x - hi.astype(F32)).astype(BF16)
    return hi, lo


def _rope_a(v1, v2):
    pad = jnp.zeros(v1.shape[:-1] + (LANES - 4 * ROPE_HALF,), v1.dtype)
    return jnp.concatenate([v1, v2, v1, v2, pad], axis=-1)


def _front_kernel(x_ref, gmix_ref, wab_ref, ws_ref, wuq_ref, wukv_ref, g64_ref, g32_ref, tri_ref,
                  g512_ref, g128_ref, ncq_ref, nckv_ref, cos_ref, sin_ref,
                  qa_ref, ka_ref, va_ref, qb_ref, kb_ref, vb_ref, qc_ref, kc_ref, vc_ref,
                  ccol_ref, crow_ref, carry_col, *, tiles_per_seq):
    i = pl.program_id(0)

    @pl.when(i % tiles_per_seq == 0)
    def _():
        carry_col[...] = jnp.zeros_like(carry_col)

    x = x_ref[...]
    ms = jnp.mean(x * x, axis=-1, keepdims=True)
    xn = (x * lax.rsqrt(ms + EPS) * gmix_ref[...]).astype(BF16)

    def group_sums(h, gmat):
        sq = (h * h).astype(BF16)
        slab = gmat.shape[0]
        return jnp.concatenate([_dot(sq[:, i:i + slab], gmat) for i in range(0, h.shape[1], slab)], axis=-1)

    def group_norm(h, gmat, n, gain):
        return h * lax.rsqrt(group_sums(h, gmat) * (1.0 / n) + EPS) * gain

    def store_pairs(ref, val, lo=0, w=PAIR_W):
        for p in range(N_PAIRS):
            ref[p, :, lo:lo + w] = val[:, p * PAIR_W:(p + 1) * PAIR_W].astype(BF16)

    g64 = g64_ref[...]
    for base, q_ref, k_ref, v_ref, row in ((C_A, qa_ref, ka_ref, va_ref, 0), (C_B, qb_ref, kb_ref, vb_ref, 2)):
        hq = _dot(xn, wab_ref[:, base:base + 512])
        store_pairs(q_ref, group_norm(hq, g64, HEAD_DIM, g512_ref[row:row + 1, :]))
        hk = _dot(xn, wab_ref[:, base + 512:base + 1024])
        store_pairs(k_ref, group_norm(hk, g64, HEAD_DIM, g512_ref[row + 1:row + 2, :]))
        hv = _dot(xn, wab_ref[:, base + 1024:base + 1536])
        store_pairs(v_ref, hv)

    hs = _dot(xn, ws_ref[...])
    cos, sin = cos_ref[...], sin_ref[...]
    ca = _rope_a(cos, cos)
    sb = _rope_a(-sin, sin)

    hcq = hs[:, 0:Q_LORA]
    cq = (hcq * lax.rsqrt(jnp.mean(hcq * hcq, axis=-1, keepdims=True) + EPS) * ncq_ref[...]).astype(BF16)
    qc = _dot(cq, wuq_ref[...])
    qn = group_norm(qc[:, 0:512], g64, NOPE, g512_ref[4:5, :])
    store_pairs(qc_ref, qn, 0)
    qra = qc[:, 512:1024]
    qrb = qc[:, 1024:1536]
    rs = lax.rsqrt(group_sums(qra, g32_ref[...]) * (1.0 / ROPE) + EPS)
    ga = g128_ref[0:1, :]
    gb = g128_ref[1:2, :]
    for p in range(N_PAIRS):
        sl = slice(p * PAIR_W, (p + 1) * PAIR_W)
        qr = (qra[:, sl] * ga * ca + qrb[:, sl] * gb * sb) * rs[:, sl]
        qc_ref[p, :, PAIR_W:2 * PAIR_W] = qr.astype(BF16)

    hckv = hs[:, Q_LORA:Q_LORA + KV_LORA]
    ckv = (hckv * lax.rsqrt(jnp.mean(hckv * hckv, axis=-1, keepdims=True) + EPS) * nckv_ref[...]).astype(BF16)
    kv = _dot(ckv, wukv_ref[...])
    store_pairs(kc_ref, group_norm(kv[:, 0:512], g64, NOPE, g512_ref[5:6, :]), 0)
    store_pairs(vc_ref, kv[:, 512:1024])
    kra = hs[:, C_KRA - C_CQ:C_KRA - C_CQ + LANES]
    krb = hs[:, C_KRB - C_CQ:C_KRB - C_CQ + LANES]
    rsk = lax.rsqrt(_dot((kra * kra).astype(BF16), g32_ref[0:LANES, 0:LANES]) * (1.0 / ROPE) + EPS)
    kr = ((kra * g128_ref[2:3, :] * ca + krb * g128_ref[3:4, :] * sb) * rsk).astype(BF16)
    for p in range(N_PAIRS):
        kc_ref[p, :, PAIR_W:2 * PAIR_W] = kr

    tri = tri_ref[...]
    hf_col = hs[:, C_HF - C_CQ:C_HF - C_CQ + LANES] + g128_ref[4:5, :]
    hi, lo = _split_hi_lo(_log_sigmoid(hf_col))
    ccol = _dot(tri, hi) + _dot(tri, lo) + carry_col[0:1, :]
    ccol_ref[...] = ccol
    tm = ccol.shape[0]
    carry_col[...] = jnp.broadcast_to(ccol[tm - 1:tm, :], carry_col.shape)

    crow_ref[...] = jnp.transpose(ccol)[0:HF_ROWS, :]


def _front_call(x2d, cw, lw, seq_len):
    t = x2d.shape[0]
    tm = FRONT_TM
    const = lambda shape: pl.BlockSpec(shape, lambda i: (0,) * len(shape))
    pair_out = lambda w: pl.BlockSpec((N_PAIRS, tm, w), lambda i: (0, i, 0))
    pair_shape = lambda w: jax.ShapeDtypeStruct((N_PAIRS, t, w), BF16)
    in_specs = [
        pl.BlockSpec((tm, D_MODEL), lambda i: (i, 0)),
        const((1, D_MODEL)),
        const((D_MODEL, C_CQ)),
        const((D_MODEL, N_FRONT - C_CQ)),
        const((Q_LORA, 1536)),
        const((KV_LORA, 1024)),
        const((GROUP_SLAB, GROUP_SLAB)),
        const((GROUP_SLAB, GROUP_SLAB)),
        const((tm, tm)),
        const((8, 512)),
        const((8, LANES)),
        const((1, Q_LORA)),
        const((1, KV_LORA)),
        pl.BlockSpec((tm, ROPE_HALF), lambda i: (i, 0)),
        pl.BlockSpec((tm, ROPE_HALF), lambda i: (i, 0)),
    ]
    out_specs = [pair_out(PAIR_W)] * 6 + [pair_out(2 * PAIR_W), pair_out(2 * PAIR_W), pair_out(PAIR_W),
                                           pl.BlockSpec((tm, LANES), lambda i: (i, 0)),
                                           pl.BlockSpec((HF_ROWS, tm), lambda i: (0, i))]
    out_shape = [pair_shape(PAIR_W)] * 6 + [pair_shape(2 * PAIR_W), pair_shape(2 * PAIR_W), pair_shape(PAIR_W),
                                            jax.ShapeDtypeStruct((t, LANES), F32),
                                            jax.ShapeDtypeStruct((HF_ROWS, t), F32)]
    return pl.pallas_call(
        functools.partial(_front_kernel, tiles_per_seq=seq_len // tm),
        grid=(t // tm,),
        in_specs=in_specs,
        out_specs=out_specs,
        out_shape=out_shape,
        scratch_shapes=[pltpu.VMEM((8, LANES), F32)],
        compiler_params=pltpu.CompilerParams(dimension_semantics=("arbitrary",),
                                             vmem_limit_bytes=VMEM_LIMIT_BYTES),
        name="front",
    )(x2d, lw["gmix"], lw["wab"], lw["ws"], lw["wuq"], lw["wukv"], cw["g64"], cw["g32"], cw["tri"],
      lw["g512"], lw["g128"], lw["ncq"], lw["nckv"], cw["cos"], cw["sin"])


def _dilated_kernel(q_ref, k_ref, v_ref, bias_ref, o_ref, oacc, lacc, s_scr, nat, res4, g4, g16, vx, *, seq_len):
    lane = lax.broadcasted_iota(jnp.int32, (1, PAIR_W), 1)
    left = lane < HEAD_DIM
    d_mid, d_far = DILATED_PAIRS[1][1], DILATED_PAIRS[2][1]
    assert DILATED_PAIRS[0][1] == 1 and d_far == d_mid * d_mid
    n_mid, n_far = seq_len // d_mid, seq_len // d_far

    vx[:, :, PAIR_W:2 * PAIR_W] = jnp.ones((len(DILATED_PAIRS), seq_len, PAIR_W), BF16)
    vx[0, :, 0:PAIR_W] = v_ref[...]
    for ti, src in enumerate((q_ref, k_ref, v_ref)):
        nat[ti] = src[...].astype(F32)

    def put(ti, pattern, lo, hi, rows):
        if ti == 2:
            vx[pattern, lo:hi, 0:PAIR_W] = rows.astype(BF16)
        else:
            (g4, g16)[pattern - 1][ti, lo:hi, :] = rows.astype(BF16)

    for ti in range(3):
        for c in range(d_mid):
            rows = nat[ti, pl.ds(c, n_mid, stride=d_mid), :]
            res4[ti, c * n_mid:(c + 1) * n_mid, :] = rows
            put(ti, 1, c * n_mid, (c + 1) * n_mid, rows)
    for ti in range(3):
        for c in range(d_far):
            rows = res4[ti, pl.ds((c % d_mid) * n_mid + c // d_mid, n_far, stride=d_mid), :]
            put(ti, 2, c * n_far, (c + 1) * n_far, rows)

    readers = (
        (1, lambda ti, lo, hi: (q_ref, k_ref)[ti][lo:hi, :]),
        (d_mid, lambda ti, lo, hi: g4[ti, lo:hi, :]),
        (d_far, lambda ti, lo, hi: g16[ti, lo:hi, :]),
    )
    for di, (d, read) in enumerate(readers):
        n_sub = seq_len // d
        n_blocks = n_sub // Q_BLOCK
        blocks = [(c, nb) for c in range(d) for nb in range(n_blocks)]

        def window(c, nb):
            k_lo = max(nb - 1, 0) * Q_BLOCK
            return c * n_sub + k_lo, c * n_sub + (nb + 1) * Q_BLOCK

        for bi, (c, nb) in enumerate(blocks):
            q = read(0, c * n_sub + nb * Q_BLOCK, c * n_sub + (nb + 1) * Q_BLOCK)
            kwin = read(1, *window(c, nb))
            w = kwin.shape[0]
            for j in range(2):
                qh = jnp.where(left if j == 0 else jnp.logical_not(left), q, jnp.zeros_like(q))
                s_scr[j, bi * Q_BLOCK:(bi + 1) * Q_BLOCK, 0:w] = (
                    _nt_dot(qh, kwin) + bias_ref[di, j, :, 2 * Q_BLOCK - w:])
        for bi, (c, nb) in enumerate(blocks):
            lo, hi = window(c, nb)
            w = hi - lo
            outs, lses = [], []
            for j in range(2):
                s = s_scr[j, bi * Q_BLOCK:(bi + 1) * Q_BLOCK, 0:w]
                m = jnp.max(s, axis=-1, keepdims=True)
                p = jnp.exp2(s - m).astype(BF16)
                res = _dot(p, vx[di, lo:hi, :])
                den = res[:, PAIR_W:2 * PAIR_W]
                outs.append(res[:, 0:PAIR_W] / den)
                lses.append(m + jnp.log2(den))
            start = nb * Q_BLOCK * d + c
            rows = pl.ds(start, Q_BLOCK) if d == 1 else pl.ds(start, Q_BLOCK, stride=d)
            oacc[di, rows, :] = jnp.where(left, outs[0], outs[1])
            lacc[di, rows, :] = jnp.where(left, lses[0], lses[1])

    l0, l1, l2 = lacc[0], lacc[1], lacc[2]
    m = jnp.maximum(jnp.maximum(l0, l1), l2)
    w0, w1, w2 = jnp.exp2(l0 - m), jnp.exp2(l1 - m), jnp.exp2(l2 - m)
    wsum = w0 + w1 + w2
    o = oacc[0] * (w0 / wsum) + oacc[1] * (w1 / wsum) + oacc[2] * (w2 / wsum)
    o_ref[...] = o.astype(BF16)


def _dilated_call(q, k, v, bias, batch, seq_len):
    n_pat = len(DILATED_PAIRS)
    seq_block = pl.BlockSpec((None, None, seq_len, PAIR_W), lambda p, b: (p, b, 0, 0))
    view = lambda a: a.reshape(N_PAIRS, batch, seq_len, PAIR_W)
    seq_scratch = lambda n, dt: pltpu.VMEM((n, seq_len, PAIR_W), dt)
    out = pl.pallas_call(
        functools.partial(_dilated_kernel, seq_len=seq_len),
        grid=(N_PAIRS, batch),
        in_specs=[seq_block, seq_block, seq_block,
                  pl.BlockSpec((n_pat, 2, Q_BLOCK, 2 * Q_BLOCK), lambda p, b: (0, p, 0, 0))],
        out_specs=seq_block,
        out_shape=jax.ShapeDtypeStruct((N_PAIRS, batch, seq_len, PAIR_W), BF16),
        scratch_shapes=[seq_scratch(n_pat, F32), seq_scratch(n_pat, F32),
                        pltpu.VMEM((2, seq_len, 2 * Q_BLOCK), F32),
                        seq_scratch(3, F32), seq_scratch(3, F32), seq_scratch(2, BF16), seq_scratch(2, BF16),
                        pltpu.VMEM((n_pat, seq_len, 2 * PAIR_W), BF16)],
        compiler_params=pltpu.CompilerParams(dimension_semantics=("arbitrary", "arbitrary"),
                                             vmem_limit_bytes=VMEM_LIMIT_BYTES),
        name="dilated",
    )(view(q), view(k), view(v), bias)
    return out.reshape(N_PAIRS, batch * seq_len, PAIR_W)


def _flash_kernel(*refs, kw, decay, seq_len):
    if decay:
        q_ref, k_ref, v_ref, crow_ref, ccol_ref, o_ref, s_scr, p_scr, v_scr = refs
    else:
        q_ref, k_ref, v_ref, o_ref, s_scr, p_scr, v_scr = refs
    hp = pl.program_id(0)
    t = FLASH_T
    lane = lax.broadcasted_iota(jnp.int32, (1, kw), 1)
    in_h0 = (lane < HEAD_DIM) | ((lane >= PAIR_W) & (lane < PAIR_W + ROPE))
    in_h1 = ((lane >= HEAD_DIM) & (lane < PAIR_W)) | ((lane >= PAIR_W + ROPE) & (lane < PAIR_W + 2 * ROPE))
    row = lax.broadcasted_iota(jnp.int32, (t, t), 0)
    col = lax.broadcasted_iota(jnp.int32, (t, t), 1)
    causal = col <= row
    out_lane = lax.broadcasted_iota(jnp.int32, (1, PAIR_W), 1)
    v_scr[:, 0:PAIR_W] = v_ref[...]
    v_scr[:, PAIR_W:2 * PAIR_W] = jnp.ones((seq_len, PAIR_W), BF16)
    n_tiles = seq_len // t
    order = [n_tiles - 1 - i // 2 if i % 2 == 0 else i // 2 for i in range(n_tiles)]
    for step, qi in enumerate(order):
        rows = slice(qi * t, (qi + 1) * t)
        q = q_ref[rows, :]
        width = (qi + 1) * t
        buf = step % 2
        q2 = jnp.concatenate([jnp.where(sel, q, jnp.zeros_like(q)) for sel in (in_h0, in_h1)], axis=0)
        mx = [None, None]
        for c in range(qi + 1):
            cols = slice(c * t, (c + 1) * t)
            s2 = _nt_dot(q2, k_ref[cols, :])
            for j in range(2):
                s = s2[j * t:(j + 1) * t, :]
                if decay:
                    s = s - crow_ref[pl.ds(2 * hp + j, 1), cols] * LOG2E
                if c == qi:
                    s = jnp.where(causal, s, -jnp.inf)
                s_scr[buf, j * t:(j + 1) * t, cols] = s
                half = jnp.maximum(s[:, 0:t // 2], s[:, t // 2:t])
                mx[j] = half if mx[j] is None else jnp.maximum(mx[j], half)
        for j in range(2):
            shift = jnp.max(mx[j], axis=-1, keepdims=True)
            if decay:
                head = (2 * hp + j).astype(F32)
                cc = ccol_ref[rows, :]
                lane_c = lax.broadcasted_iota(jnp.int32, cc.shape, 1).astype(F32)
                cq = jnp.sum(jnp.where(lane_c == head, cc, 0.0), axis=-1, keepdims=True) * LOG2E
                shift = (shift + cq) - cq
            for c in range(qi + 1):
                cols = slice(c * t, (c + 1) * t)
                hrows = slice(j * t, (j + 1) * t)
                p_scr[buf, hrows, cols] = jnp.exp2(s_scr[buf, hrows, cols] - shift).astype(BF16)
        res = _dot(p_scr[buf, :, 0:width], v_scr[0:width, :])
        out = res[:, 0:PAIR_W] / res[:, PAIR_W:2 * PAIR_W]
        o_ref[rows, :] = jnp.where(out_lane < HEAD_DIM, out[0:t, :], out[t:2 * t, :]).astype(BF16)


def _flash_call(q, k, v, batch, seq_len, crow=None, ccol=None):
    kw = q.shape[-1]
    decay = crow is not None
    view = lambda a: a.reshape(N_PAIRS, batch, seq_len, a.shape[-1])
    args = [view(q), view(k), view(v)]
    seq_block = lambda w: pl.BlockSpec((None, None, seq_len, w), lambda p, b: (p, b, 0, 0))
    in_specs = [seq_block(kw), seq_block(kw), seq_block(PAIR_W)]
    if decay:
        args += [crow, ccol]
        in_specs += [pl.BlockSpec((HF_ROWS, seq_len), lambda p, b: (0, b)),
                     pl.BlockSpec((seq_len, LANES), lambda p, b: (b, 0))]
    out = pl.pallas_call(
        functools.partial(_flash_kernel, kw=kw, decay=decay, seq_len=seq_len),
        grid=(N_PAIRS, batch),
        in_specs=in_specs,
        out_specs=seq_block(PAIR_W),
        out_shape=jax.ShapeDtypeStruct((N_PAIRS, batch, seq_len, PAIR_W), BF16),
        scratch_shapes=[pltpu.VMEM((2, 2 * FLASH_T, seq_len), F32), pltpu.VMEM((2, 2 * FLASH_T, seq_len), BF16),
                        pltpu.VMEM((seq_len, 2 * PAIR_W), BF16)],
        compiler_params=pltpu.CompilerParams(dimension_semantics=("arbitrary",) * 2,
                                             vmem_limit_bytes=VMEM_LIMIT_BYTES),
        name="flash_fox" if decay else "flash_mla",
    )(*args)
    return out.reshape(N_PAIRS, batch * seq_len, PAIR_W)


def _merge_kernel(x_ref, oa_ref, ob_ref, oc_ref, gmix_ref, wg_ref, wb_ref, wo_ref, gffn_ref,
                  wr_ref, br_ref, x1_ref, gates_ref):
    x = x_ref[...]
    ms = jnp.mean(x * x, axis=-1, keepdims=True)
    xn = (x * lax.rsqrt(ms + EPS) * gmix_ref[...]).astype(BF16)
    mixed = None
    for g, o_ref in enumerate((oa_ref, ob_ref, oc_ref)):
        o = jnp.concatenate([o_ref[p] for p in range(N_PAIRS)], axis=-1)
        gate = jax.nn.sigmoid(_dot(xn, wg_ref[:, g * D_MODEL:(g + 1) * D_MODEL]))
        term = gate * _dot(o, wb_ref[g])
        mixed = term if mixed is None else mixed + term
    x1 = x + _dot(mixed.astype(BF16), wo_ref[...])
    x1_ref[...] = x1

    ms1 = jnp.mean(x1 * x1, axis=-1, keepdims=True)
    xf = x1 * lax.rsqrt(ms1 + EPS) * gffn_ref[...]
    hi, lo = _split_hi_lo(xf)
    tm = xf.shape[0]
    parts = _dot(jnp.concatenate([hi, lo], axis=0), wr_ref[...])
    lg = (parts[0:tm, 0:LANES] + parts[0:tm, LANES:2 * LANES]
          + parts[tm:2 * tm, 0:LANES] + parts[tm:2 * tm, LANES:2 * LANES]) + br_ref[...]
    lane = lax.broadcasted_iota(jnp.int32, lg.shape, 1)
    lane_f = lane.astype(F32)
    neg = -jnp.inf
    far = float(LANES)
    gl = jnp.where((lane >= N_EXPERTS) & (lane < N_EXPERTS + N_GROUPS), lg, neg)
    gmax = jnp.max(gl, axis=-1, keepdims=True)
    pg_top = 1.0 / jnp.sum(jnp.exp(gl - gmax), axis=-1, keepdims=True)
    gidx = jnp.min(jnp.where(gl == gmax, lane_f, far), axis=-1, keepdims=True) - float(N_EXPERTS)
    in_group = (lane < N_EXPERTS) & ((lane // EXPERTS_PER_GROUP).astype(F32) == gidx)
    ev = jnp.where(in_group, lg, neg)
    v1 = jnp.max(ev, axis=-1, keepdims=True)
    i1 = jnp.min(jnp.where(ev == v1, lane_f, far), axis=-1, keepdims=True)
    ev2 = jnp.where(lane_f == i1, neg, ev)
    v2 = jnp.max(ev2, axis=-1, keepdims=True)
    i2 = jnp.min(jnp.where(ev2 == v2, lane_f, far), axis=-1, keepdims=True)
    e2 = jnp.exp(v2 - v1)
    den = 1.0 + e2
    w1 = (1.0 / den) * pg_top
    w2 = (e2 / den) * pg_top
    gates_ref[...] = jnp.where(lane_f == i1, w1, jnp.where(lane_f == i2, w2,
                                                           jnp.where(lane == GROUP_LANE, gidx, 0.0)))


def _merge_call(x2d, oa, ob, oc, lw):
    t = x2d.shape[0]
    tm = MERGE_TM
    const = lambda shape: pl.BlockSpec(shape, lambda i: (0,) * len(shape), pipeline_mode=pl.Buffered(1))
    row_tile = lambda w: pl.BlockSpec((tm, w), lambda i: (i, 0))
    pair_in = pl.BlockSpec((N_PAIRS, tm, PAIR_W), lambda i: (0, i, 0))
    return pl.pallas_call(
        _merge_kernel,
        grid=(t // tm,),
        in_specs=[row_tile(D_MODEL), pair_in, pair_in, pair_in, const((1, D_MODEL)),
                  const((D_MODEL, 3 * D_MODEL)), const((3, 512, D_MODEL)), const((D_MODEL, D_MODEL)),
                  const((1, D_MODEL)), const((D_MODEL, 2 * LANES)), const((1, LANES))],
        out_specs=[row_tile(D_MODEL), row_tile(LANES)],
        out_shape=[jax.ShapeDtypeStruct((t, D_MODEL), F32), jax.ShapeDtypeStruct((t, LANES), F32)],
        compiler_params=pltpu.CompilerParams(dimension_semantics=("arbitrary",),
                                             vmem_limit_bytes=VMEM_LIMIT_BYTES),
        name="merge",
    )(x2d, oa, ob, oc, lw["gmix"], lw["wg"], lw["wb"], lw["wo"], lw["gffn"], lw["wr"], lw["br"])


def _moe_kernel(x_ref, gates_ref, gffn_ref, lstrict_ref, ustrict_ref, wg_ref, wu_ref, wd_ref, o_ref,
                xs_ref, gs_ref, ys_ref):
    x = x_ref[...]
    tm = x.shape[0]
    ms = jnp.mean(x * x, axis=-1, keepdims=True)
    xn = (x * lax.rsqrt(ms + EPS) * gffn_ref[...]).astype(BF16)

    gates = gates_ref[...]
    lane = lax.broadcasted_iota(jnp.int32, gates.shape, 1)
    gid = jnp.sum(jnp.where(lane == GROUP_LANE, gates, 0.0), axis=-1, keepdims=True)
    onehot = jnp.where((lane.astype(F32) == gid) & (lane < N_GROUPS), 1.0, 0.0)
    before = _dot(lstrict_ref[...], onehot.astype(BF16))
    counts = before[tm - 1:tm, :] + onehot[tm - 1:tm, :]
    n_chunks = jnp.floor((counts + (MOE_CHUNK - 1.0)) * (1.0 / MOE_CHUNK))
    seg_start = _dot(jnp.broadcast_to(n_chunks, (8, LANES)).astype(BF16), ustrict_ref[...])[0:1, :] * MOE_CHUNK
    rank = jnp.sum(onehot * (before + seg_start), axis=-1, keepdims=True)
    rank_row = jnp.transpose(jnp.broadcast_to(rank, (tm, LANES)))[0:1, :]
    slot_col = lax.broadcasted_iota(jnp.int32, (MOE_ROWS, 1), 0).astype(F32)
    slot_row = lax.broadcasted_iota(jnp.int32, (1, MOE_ROWS), 1).astype(F32)
    perm = jnp.where(slot_col == rank_row, 1.0, 0.0).astype(BF16)
    perm_t = jnp.where(rank == slot_row, 1.0, 0.0).astype(BF16)

    xs_ref[...] = _dot(perm, xn).astype(BF16)
    g_hi, g_lo = _split_hi_lo(gates)
    g_sorted = _dot(perm, jnp.concatenate([g_hi, g_lo], axis=-1))
    gs_ref[...] = g_sorted[:, 0:LANES] + g_sorted[:, LANES:2 * LANES]
    ys_ref[...] = jnp.zeros_like(ys_ref)

    nc = [n_chunks[0, g].astype(jnp.int32) for g in range(N_GROUPS)]
    ends = [nc[0], nc[0] + nc[1], nc[0] + nc[1] + nc[2]]
    total = ends[2] + nc[3]

    def chunk(k, carry):
        g = sum((k >= e).astype(jnp.int32) for e in ends)
        r0 = pl.multiple_of(k * MOE_CHUNK, MOE_CHUNK)
        xc = xs_ref[pl.ds(r0, MOE_CHUNK), :]
        gc = gs_ref[pl.ds(r0, MOE_CHUNK), :]
        lane_c = lax.broadcasted_iota(jnp.int32, gc.shape, 1)
        parts = []
        for e in range(EXPERTS_PER_GROUP):
            h_gate = _dot(xc, wg_ref[g * EXPERTS_PER_GROUP + e])
            h_up = _dot(xc, wu_ref[g * EXPERTS_PER_GROUP + e])
            gate = jnp.sum(jnp.where(lane_c == g * EXPERTS_PER_GROUP + e, gc, 0.0), axis=-1, keepdims=True)
            parts.append((jax.nn.silu(h_gate) * h_up * gate).astype(BF16))
        y = sum(_dot(parts[e], wd_ref[g * EXPERTS_PER_GROUP + e]) for e in range(EXPERTS_PER_GROUP))
        ys_ref[pl.ds(r0, MOE_CHUNK), :] = y.astype(BF16)
        return carry

    lax.fori_loop(0, total, chunk, 0)
    o_ref[...] = x + _dot(perm_t, ys_ref[...])


def _moe_call(x1, gates, cw, lw):
    t = x1.shape[0]
    tm = MOE_TM
    const = lambda shape: pl.BlockSpec(shape, lambda i: (0,) * len(shape))
    resident = lambda shape: pl.BlockSpec(shape, lambda i: (0,) * len(shape), pipeline_mode=pl.Buffered(1))
    return pl.pallas_call(
        _moe_kernel,
        grid=(t // tm,),
        in_specs=[pl.BlockSpec((tm, D_MODEL), lambda i: (i, 0)),
                  pl.BlockSpec((tm, LANES), lambda i: (i, 0)),
                  const((1, D_MODEL)), const((tm, tm)), const((LANES, LANES)),
                  resident((N_EXPERTS, D_MODEL, D_FF)), resident((N_EXPERTS, D_MODEL, D_FF)),
                  resident((N_EXPERTS, D_FF, D_MODEL))],
        out_specs=pl.BlockSpec((tm, D_MODEL), lambda i: (i, 0)),
        out_shape=jax.ShapeDtypeStruct((t, D_MODEL), F32),
        scratch_shapes=[pltpu.VMEM((MOE_ROWS, D_MODEL), BF16), pltpu.VMEM((MOE_ROWS, LANES), F32),
                        pltpu.VMEM((MOE_ROWS, D_MODEL), BF16)],
        compiler_params=pltpu.CompilerParams(dimension_semantics=("arbitrary",),
                                             vmem_limit_bytes=VMEM_LIMIT_BYTES),
        name="moe",
    )(x1, gates, lw["gffn"], cw["lstrict"], cw["ustrict"], lw["weg"], lw["weu"], lw["wed"])


def _t5_bucket(dist):
    max_exact = N_BUCKETS // 2
    log_ratio = np.log(np.maximum(dist, max_exact) / max_exact) / np.log(MAX_DISTANCE / max_exact)
    large = np.minimum(max_exact + (log_ratio * (N_BUCKETS - max_exact)).astype(np.int32), N_BUCKETS - 1)
    return np.where(dist < max_exact, dist, large).astype(np.int32)


def _dilated_bias(rel_bias):
    tables = []
    span = 3 * Q_BLOCK
    for window, dil in DILATED_PAIRS:
        assert window // dil == Q_BLOCK
        per_dist = rel_bias[_t5_bucket(np.arange(Q_BLOCK + 1) * dil)].astype(F32).T * LOG2E
        diag = jnp.concatenate([jnp.full((N_HEADS, Q_BLOCK), -jnp.inf, F32), per_dist[:, ::-1],
                                jnp.full((N_HEADS, span - 2 * Q_BLOCK), -jnp.inf, F32)], axis=1)
        skew = jnp.tile(diag, (1, Q_BLOCK))[:, :Q_BLOCK * span].reshape(N_HEADS, Q_BLOCK, span)
        tables.append(skew[:, :, Q_BLOCK:])
    return jnp.stack(tables, axis=0)


def _const_weights(positions, rel_bias):
    b, s = positions.shape
    inv_freq = ROPE_THETA ** (-jnp.arange(ROPE_HALF, dtype=F32) / ROPE_HALF)
    ang = positions.astype(F32).reshape(b * s, 1) * inv_freq
    cos, sin = jnp.cos(ang), jnp.sin(ang)
    idx = np.arange(GROUP_SLAB)
    return {
        "cos": cos, "sin": sin,
        "g64": jnp.asarray(idx[:, None] // HEAD_DIM == idx[None, :] // HEAD_DIM, BF16),
        "g32": jnp.asarray(idx[:, None] // ROPE == idx[None, :] // ROPE, BF16),
        "tri": jnp.asarray(np.arange(FRONT_TM)[None, :] <= np.arange(FRONT_TM)[:, None], BF16),
        "lstrict": jnp.asarray(np.arange(MOE_TM)[None, :] < np.arange(MOE_TM)[:, None], BF16),
        "ustrict": jnp.asarray(np.arange(LANES)[:, None] < np.arange(LANES)[None, :], BF16),
        "bias": _dilated_bias(rel_bias),
    }


def _layer_weights(l, p):
    w_in = p["w_in"][l]
    o_hf = 3072
    o_cq = o_hf + N_HEADS
    o_ckv = o_cq + Q_LORA
    o_kr = o_ckv + KV_LORA
    o_g = o_kr + ROPE
    kr1 = w_in[:, o_kr:o_kr + ROPE_HALF]
    kr2 = w_in[:, o_kr + ROPE_HALF:o_kr + ROPE]
    hf_w = w_in[:, o_hf:o_cq]
    w_small = jnp.concatenate([
        w_in[:, o_cq:o_ckv], w_in[:, o_ckv:o_kr], _rope_a(kr1, kr2), _rope_a(kr2, kr1),
        jnp.pad(hf_w, ((0, 0), (0, LANES - N_HEADS))),
    ], axis=1).astype(BF16)

    wuq = p["w_uq"][l].reshape(Q_LORA, N_HEADS, NOPE + ROPE)
    q_nope = wuq[:, :, :NOPE].reshape(Q_LORA, N_HEADS * NOPE)
    q1 = wuq[:, :, NOPE:NOPE + ROPE_HALF]
    q2 = wuq[:, :, NOPE + ROPE_HALF:]

    def pair_rope(v1, v2):
        x = jnp.concatenate([v1, v2], axis=-1).reshape(Q_LORA, N_PAIRS, 2 * ROPE)
        return jnp.pad(x, ((0, 0), (0, 0), (0, LANES - 2 * ROPE))).reshape(Q_LORA, N_PAIRS * LANES)

    wuq_p = jnp.concatenate([q_nope, pair_rope(q1, q2), pair_rope(q2, q1)], axis=1).astype(BF16)
    wukv = p["w_ukv"][l].reshape(KV_LORA, N_HEADS, NOPE + HEAD_DIM)
    wukv_p = jnp.concatenate([wukv[:, :, :NOPE].reshape(KV_LORA, 512),
                              wukv[:, :, NOPE:].reshape(KV_LORA, 512)], axis=1).astype(BF16)

    tile8 = lambda g: jnp.tile(g, N_HEADS)
    sc_ab = LOG2E / math.sqrt(HEAD_DIM)
    sc_c = LOG2E / math.sqrt(NOPE + ROPE)
    gq_c, gk_c = p["gq_c"][l], p["gk_c"][l]
    zeros512 = jnp.zeros((512,), F32)
    g512 = jnp.stack([tile8(p["gq_a"][l]) * sc_ab, tile8(p["gk_a"][l]),
                      tile8(p["gq_b"][l]) * sc_ab, tile8(p["gk_b"][l]),
                      tile8(gq_c[:NOPE]) * sc_c, tile8(gk_c[:NOPE]), zeros512, zeros512])
    gq1, gq2 = gq_c[NOPE:NOPE + ROPE_HALF] * sc_c, gq_c[NOPE + ROPE_HALF:] * sc_c
    gk1, gk2 = gk_c[NOPE:NOPE + ROPE_HALF], gk_c[NOPE + ROPE_HALF:]
    zeros128 = jnp.zeros((LANES,), F32)
    bf = p["b_forget"][l].astype(F32)
    g128 = jnp.stack([_rope_a(gq1, gq2), _rope_a(gq2, gq1), _rope_a(gk1, gk2), _rope_a(gk2, gk1),
                      jnp.pad(bf, (0, LANES - N_HEADS)), zeros128, zeros128, zeros128])

    wr = jnp.concatenate([p["w_router_expert"][l], p["w_router_group"][l]], axis=1)
    wr = jnp.pad(wr, ((0, 0), (0, LANES - N_EXPERTS - N_GROUPS))).astype(F32)
    wrh = wr.astype(BF16)
    wrl = (wr - wrh.astype(F32)).astype(BF16)
    br = jnp.pad(jnp.concatenate([p["b_router_expert"][l], p["b_router_group"][l]]).astype(F32),
                 (0, LANES - N_EXPERTS - N_GROUPS))[None, :]
    return {
        "gmix": p["norm_mix"][l][None, :], "wab": w_in[:, 0:C_CQ].astype(BF16), "ws": w_small, "wuq": wuq_p, "wukv": wukv_p,
        "g512": g512, "g128": g128, "ncq": p["norm_cq"][l][None, :], "nckv": p["norm_ckv"][l][None, :],
        "wg": w_in[:, o_g:].astype(BF16), "wb": p["w_branch"][l].astype(BF16), "wo": p["w_out"][l].astype(BF16),
        "gffn": p["norm_ffn"][l][None, :], "wr": jnp.concatenate([wrh, wrl], axis=1), "br": br,
        "weg": p["w_expert_gate"][l].astype(BF16), "weu": p["w_expert_up"][l].astype(BF16),
        "wed": p["w_expert_down"][l].astype(BF16),
    }


def kernel(x, positions, rel_bias, norm_mix, w_in, b_forget, gq_a, gk_a, gq_b, gk_b, gq_c, gk_c, norm_cq, norm_ckv, w_uq, w_ukv, w_branch, w_out, norm_ffn, w_router_group, b_router_group, w_router_expert, b_router_expert, w_expert_gate, w_expert_up, w_expert_down):
    batch, seq_len, d_model = x.shape
    assert d_model == D_MODEL and seq_len % (Q_BLOCK * DILATED_PAIRS[-1][1]) == 0
    assert seq_len == DILATED_PAIRS[-1][0], "the widest dilated window is assumed to span the sequence"
    p = dict(norm_mix=norm_mix, w_in=w_in, b_forget=b_forget, gq_a=gq_a, gk_a=gk_a, gq_b=gq_b, gk_b=gk_b,
             gq_c=gq_c, gk_c=gk_c, norm_cq=norm_cq, norm_ckv=norm_ckv, w_uq=w_uq, w_ukv=w_ukv,
             w_branch=w_branch, w_out=w_out, norm_ffn=norm_ffn, w_router_group=w_router_group,
             b_router_group=b_router_group, w_router_expert=w_router_expert, b_router_expert=b_router_expert,
             w_expert_gate=w_expert_gate, w_expert_up=w_expert_up, w_expert_down=w_expert_down)
    cw = _const_weights(positions, rel_bias)
    xs = x.reshape(batch * seq_len, d_model)
    for l in range(norm_mix.shape[0]):
        lw = _layer_weights(l, p)
        qa, ka, va, qb, kb, vb, qc, kc, vc, ccol, crow = _front_call(xs, cw, lw, seq_len)
        oa = _dilated_call(qa, ka, va, cw["bias"], batch, seq_len)
        ob = _flash_call(qb, kb, vb, batch, seq_len, crow, ccol)
        oc = _flash_call(qc, kc, vc, batch, seq_len)
        x1, gates = _merge_call(xs, oa, ob, oc, lw)
        xs = _moe_call(x1, gates, cw, lw)
    return xs.reshape(batch, seq_len, d_model)
```

```python
import functools
import math

import numpy as np
import jax
import jax.numpy as jnp
from jax import lax
from jax.experimental import pallas as pl
from jax.experimental.pallas import tpu as pltpu

F32 = jnp.float32
BF16 = jnp.bfloat16

D_MODEL = 1024
N_HEADS = 8
HEAD_DIM = 64
N_PAIRS = N_HEADS // 2
PAIR_W = 2 * HEAD_DIM
BRANCH_W = N_HEADS * HEAD_DIM
DILATED_PAIRS = ((128, 1), (512, 4), (2048, 16))
Q_LORA = 256
KV_LORA = 128
NOPE = 64
ROPE = 32
ROPE_HALF = ROPE // 2
ROPE_THETA = 10000.0
N_BUCKETS = 32
MAX_DISTANCE = 2048
Q_BLOCK = 128
N_GROUPS = 4
EXPERTS_PER_GROUP = 4
N_EXPERTS = 16
D_FF = 256
EPS = 1e-6
LOG2E = math.log2(math.e)
LANES = 128
SUBLANES = 8

VMEM_LIMIT_BYTES = 56 * 1024 * 1024

FRONT_TM = 512
MERGE_TM = 1024
MOE_TM = 512
MOE_CHUNK = 128
MOE_ROWS = MOE_TM + (N_GROUPS - 1) * MOE_CHUNK
GROUP_LANE = N_EXPERTS
FLASH_T = 256

C_A = 0
C_B = 3 * BRANCH_W
C_CQ = 6 * BRANCH_W
C_CKV = C_CQ + Q_LORA
C_KRA = C_CKV + KV_LORA
C_KRB = C_KRA + LANES
C_HF = C_KRB + LANES
N_FRONT = C_HF + LANES
HF_ROWS = 16
GROUP_SLAB = 256


def _nt_dot(a, b):
    return lax.dot_general(a, b, (((1,), (1,)), ((), ())), preferred_element_type=F32)


def _dot(a, b):
    return jnp.dot(a, b, preferred_element_type=F32)


def _log_sigmoid(x):
    return jnp.minimum(x, 0.0) - jnp.log1p(jnp.exp(-jnp.abs(x)))


def _split_hi_lo(x):
    hi = x.astype(BF16)
    lo = (x - hi.astype(F32)).astype(BF16)
    return hi, lo


def _rope_a(v1, v2):
    pad = jnp.zeros(v1.shape[:-1] + (LANES - 4 * ROPE_HALF,), v1.dtype)
    return jnp.concatenate([v1, v2, v1, v2, pad], axis=-1)


def _front_kernel(x_ref, gmix_ref, wab_ref, ws_ref, wuq_ref, wukv_ref, g64_ref, g32_ref, tri_ref,
                  g512_ref, g128_ref, ncq_ref, nckv_ref, cos_ref, sin_ref,
                  qa_ref, ka_ref, va_ref, qb_ref, kb_ref, vb_ref, qc_ref, kc_ref, vc_ref,
                  ccol_ref, crow_ref, carry_col, *, tiles_per_seq):
    i = pl.program_id(0)

    @pl.when(i % tiles_per_seq == 0)
    def _():
        carry_col[...] = jnp.zeros_like(carry_col)

    x = x_ref[...]
    ms = jnp.mean(x * x, axis=-1, keepdims=True)
    xn = (x * lax.rsqrt(ms + EPS) * gmix_ref[...]).astype(BF16)

    def group_sums(h, gmat):
        sq = (h * h).astype(BF16)
        slab = gmat.shape[0]
        return jnp.concatenate([_dot(sq[:, i:i + slab], gmat) for i in range(0, h.shape[1], slab)], axis=-1)

    def group_norm(h, gmat, n, gain):
        return h * lax.rsqrt(group_sums(h, gmat) * (1.0 / n) + EPS) * gain

    def store_pairs(ref, val, lo=0, w=PAIR_W):
        for p in range(N_PAIRS):
            ref[p, :, lo:lo + w] = val[:, p * PAIR_W:(p + 1) * PAIR_W].astype(BF16)

    g64 = g64_ref[...]
    for base, q_ref, k_ref, v_ref, row in ((C_A, qa_ref, ka_ref, va_ref, 0), (C_B, qb_ref, kb_ref, vb_ref, 2)):
        hq = _dot(xn, wab_ref[:, base:base + BRANCH_W])
        store_pairs(q_ref, group_norm(hq, g64, HEAD_DIM, g512_ref[row:row + 1, :]))
        hk = _dot(xn, wab_ref[:, base + BRANCH_W:base + 2 * BRANCH_W])
        store_pairs(k_ref, group_norm(hk, g64, HEAD_DIM, g512_ref[row + 1:row + 2, :]))
        hv = _dot(xn, wab_ref[:, base + 2 * BRANCH_W:base + 3 * BRANCH_W])
        store_pairs(v_ref, hv)

    hs = _dot(xn, ws_ref[...])
    cos, sin = cos_ref[...], sin_ref[...]
    ca = _rope_a(cos, cos)
    sb = _rope_a(-sin, sin)

    hcq = hs[:, 0:Q_LORA]
    cq = (hcq * lax.rsqrt(jnp.mean(hcq * hcq, axis=-1, keepdims=True) + EPS) * ncq_ref[...]).astype(BF16)
    qc = _dot(cq, wuq_ref[...])
    qn = group_norm(qc[:, 0:BRANCH_W], g64, NOPE, g512_ref[4:5, :])
    store_pairs(qc_ref, qn, 0)
    qra = qc[:, BRANCH_W:2 * BRANCH_W]
    qrb = qc[:, 2 * BRANCH_W:3 * BRANCH_W]
    rs = lax.rsqrt(group_sums(qra, g32_ref[...]) * (1.0 / ROPE) + EPS)
    ga = g128_ref[0:1, :]
    gb = g128_ref[1:2, :]
    for p in range(N_PAIRS):
        sl = slice(p * PAIR_W, (p + 1) * PAIR_W)
        qr = (qra[:, sl] * ga * ca + qrb[:, sl] * gb * sb) * rs[:, sl]
        qc_ref[p, :, PAIR_W:2 * PAIR_W] = qr.astype(BF16)

    hckv = hs[:, Q_LORA:Q_LORA + KV_LORA]
    ckv = (hckv * lax.rsqrt(jnp.mean(hckv * hckv, axis=-1, keepdims=True) + EPS) * nckv_ref[...]).astype(BF16)
    kv = _dot(ckv, wukv_ref[...])
    store_pairs(kc_ref, group_norm(kv[:, 0:BRANCH_W], g64, NOPE, g512_ref[5:6, :]), 0)
    store_pairs(vc_ref, kv[:, BRANCH_W:2 * BRANCH_W])
    kra = hs[:, C_KRA - C_CQ:C_KRA - C_CQ + LANES]
    krb = hs[:, C_KRB - C_CQ:C_KRB - C_CQ + LANES]
    rsk = lax.rsqrt(_dot((kra * kra).astype(BF16), g32_ref[0:LANES, 0:LANES]) * (1.0 / ROPE) + EPS)
    kr = ((kra * g128_ref[2:3, :] * ca + krb * g128_ref[3:4, :] * sb) * rsk).astype(BF16)
    for p in range(N_PAIRS):
        kc_ref[p, :, PAIR_W:2 * PAIR_W] = kr

    tri = tri_ref[...]
    hf_col = hs[:, C_HF - C_CQ:C_HF - C_CQ + LANES] + g128_ref[4:5, :]
    hi, lo = _split_hi_lo(_log_sigmoid(hf_col))
    ccol = _dot(tri, hi) + _dot(tri, lo) + carry_col[0:1, :]
    ccol_ref[...] = ccol
    tm = ccol.shape[0]
    carry_col[...] = jnp.broadcast_to(ccol[tm - 1:tm, :], carry_col.shape)

    crow_ref[...] = jnp.transpose(ccol)[0:HF_ROWS, :]


def _front_call(x2d, cw, lw, seq_len):
    t = x2d.shape[0]
    tm = FRONT_TM
    const = lambda shape: pl.BlockSpec(shape, lambda i: (0,) * len(shape))
    pair_out = lambda w: pl.BlockSpec((N_PAIRS, tm, w), lambda i: (0, i, 0))
    pair_shape = lambda w: jax.ShapeDtypeStruct((N_PAIRS, t, w), BF16)
    in_specs = [
        pl.BlockSpec((tm, D_MODEL), lambda i: (i, 0)),
        const((1, D_MODEL)),
        const((D_MODEL, C_CQ)),
        const((D_MODEL, N_FRONT - C_CQ)),
        const((Q_LORA, 3 * BRANCH_W)),
        const((KV_LORA, 2 * BRANCH_W)),
        const((GROUP_SLAB, GROUP_SLAB)),
        const((GROUP_SLAB, GROUP_SLAB)),
        const((tm, tm)),
        const((SUBLANES, BRANCH_W)),
        const((SUBLANES, LANES)),
        const((1, Q_LORA)),
        const((1, KV_LORA)),
        pl.BlockSpec((tm, ROPE_HALF), lambda i: (i, 0)),
        pl.BlockSpec((tm, ROPE_HALF), lambda i: (i, 0)),
    ]
    out_specs = [pair_out(PAIR_W)] * 6 + [pair_out(2 * PAIR_W), pair_out(2 * PAIR_W), pair_out(PAIR_W),
                                           pl.BlockSpec((tm, LANES), lambda i: (i, 0)),
                                           pl.BlockSpec((HF_ROWS, tm), lambda i: (0, i))]
    out_shape = [pair_shape(PAIR_W)] * 6 + [pair_shape(2 * PAIR_W), pair_shape(2 * PAIR_W), pair_shape(PAIR_W),
                                            jax.ShapeDtypeStruct((t, LANES), F32),
                                            jax.ShapeDtypeStruct((HF_ROWS, t), F32)]
    return pl.pallas_call(
        functools.partial(_front_kernel, tiles_per_seq=seq_len // tm),
        grid=(t // tm,),
        in_specs=in_specs,
        out_specs=out_specs,
        out_shape=out_shape,
        scratch_shapes=[pltpu.VMEM((SUBLANES, LANES), F32)],
        compiler_params=pltpu.CompilerParams(dimension_semantics=("arbitrary",),
                                             vmem_limit_bytes=VMEM_LIMIT_BYTES),
        name="front",
    )(x2d, lw["gmix"], lw["wab"], lw["ws"], lw["wuq"], lw["wukv"], cw["g64"], cw["g32"], cw["tri"],
      lw["g512"], lw["g128"], lw["ncq"], lw["nckv"], cw["cos"], cw["sin"])


def _dilated_kernel(q_ref, k_ref, v_ref, bias_ref, o_ref, oacc, lacc, s_scr, nat, res4, g4, g16, vx, *, seq_len):
    lane = lax.broadcasted_iota(jnp.int32, (1, PAIR_W), 1)
    left = lane < HEAD_DIM
    d_mid, d_far = DILATED_PAIRS[1][1], DILATED_PAIRS[2][1]
    assert DILATED_PAIRS[0][1] == 1 and d_far == d_mid * d_mid
    n_mid, n_far = seq_len // d_mid, seq_len // d_far

    vx[:, :, PAIR_W:2 * PAIR_W] = jnp.ones((len(DILATED_PAIRS), seq_len, PAIR_W), BF16)
    vx[0, :, 0:PAIR_W] = v_ref[...]
    for ti, src in enumerate((q_ref, k_ref, v_ref)):
        nat[ti] = src[...].astype(F32)

    def put(ti, pattern, lo, hi, rows):
        if ti == 2:
            vx[pattern, lo:hi, 0:PAIR_W] = rows.astype(BF16)
        else:
            (g4, g16)[pattern - 1][ti, lo:hi, :] = rows.astype(BF16)

    for ti in range(3):
        for c in range(d_mid):
            rows = nat[ti, pl.ds(c, n_mid, stride=d_mid), :]
            res4[ti, c * n_mid:(c + 1) * n_mid, :] = rows
            put(ti, 1, c * n_mid, (c + 1) * n_mid, rows)
    for ti in range(3):
        for c in range(d_far):
            rows = res4[ti, pl.ds((c % d_mid) * n_mid + c // d_mid, n_far, stride=d_mid), :]
            put(ti, 2, c * n_far, (c + 1) * n_far, rows)

    readers = (
        (1, lambda ti, lo, hi: (q_ref, k_ref)[ti][lo:hi, :]),
        (d_mid, lambda ti, lo, hi: g4[ti, lo:hi, :]),
        (d_far, lambda ti, lo, hi: g16[ti, lo:hi, :]),
    )
    for di, (d, read) in enumerate(readers):
        n_sub = seq_len // d
        n_blocks = n_sub // Q_BLOCK
        blocks = [(c, nb) for c in range(d) for nb in range(n_blocks)]

        def window(c, nb):
            k_lo = max(nb - 1, 0) * Q_BLOCK
            return c * n_sub + k_lo, c * n_sub + (nb + 1) * Q_BLOCK

        for bi, (c, nb) in enumerate(blocks):
            q = read(0, c * n_sub + nb * Q_BLOCK, c * n_sub + (nb + 1) * Q_BLOCK)
            kwin = read(1, *window(c, nb))
            w = kwin.shape[0]
            for j in range(2):
                qh = jnp.where(left if j == 0 else jnp.logical_not(left), q, jnp.zeros_like(q))
                s_scr[j, bi * Q_BLOCK:(bi + 1) * Q_BLOCK, 0:w] = (
                    _nt_dot(qh, kwin) + bias_ref[di, j, :, 2 * Q_BLOCK - w:])
        for bi, (c, nb) in enumerate(blocks):
            lo, hi = window(c, nb)
            w = hi - lo
            outs, lses = [], []
            for j in range(2):
                s = s_scr[j, bi * Q_BLOCK:(bi + 1) * Q_BLOCK, 0:w]
                m = jnp.max(s, axis=-1, keepdims=True)
                p = jnp.exp2(s - m).astype(BF16)
                res = _dot(p, vx[di, lo:hi, :])
                den = res[:, PAIR_W:2 * PAIR_W]
                outs.append(res[:, 0:PAIR_W] / den)
                lses.append(m + jnp.log2(den))
            start = nb * Q_BLOCK * d + c
            rows = pl.ds(start, Q_BLOCK) if d == 1 else pl.ds(start, Q_BLOCK, stride=d)
            oacc[di, rows, :] = jnp.where(left, outs[0], outs[1])
            lacc[di, rows, :] = jnp.where(left, lses[0], lses[1])

    l0, l1, l2 = lacc[0], lacc[1], lacc[2]
    m = jnp.maximum(jnp.maximum(l0, l1), l2)
    w0, w1, w2 = jnp.exp2(l0 - m), jnp.exp2(l1 - m), jnp.exp2(l2 - m)
    wsum = w0 + w1 + w2
    o = oacc[0] * (w0 / wsum) + oacc[1] * (w1 / wsum) + oacc[2] * (w2 / wsum)
    o_ref[...] = o.astype(BF16)


def _dilated_call(q, k, v, bias, batch, seq_len):
    n_pat = len(DILATED_PAIRS)
    seq_block = pl.BlockSpec((None, None, seq_len, PAIR_W), lambda p, b: (p, b, 0, 0))
    view = lambda a: a.reshape(N_PAIRS, batch, seq_len, PAIR_W)
    seq_scratch = lambda n, dt: pltpu.VMEM((n, seq_len, PAIR_W), dt)
    out = pl.pallas_call(
        functools.partial(_dilated_kernel, seq_len=seq_len),
        grid=(N_PAIRS, batch),
        in_specs=[seq_block, seq_block, seq_block,
                  pl.BlockSpec((n_pat, 2, Q_BLOCK, 2 * Q_BLOCK), lambda p, b: (0, p, 0, 0))],
        out_specs=seq_block,
        out_shape=jax.ShapeDtypeStruct((N_PAIRS, batch, seq_len, PAIR_W), BF16),
        scratch_shapes=[seq_scratch(n_pat, F32), seq_scratch(n_pat, F32),
                        pltpu.VMEM((2, seq_len, 2 * Q_BLOCK), F32),
                        seq_scratch(3, F32), seq_scratch(3, F32), seq_scratch(2, BF16), seq_scratch(2, BF16),
                        pltpu.VMEM((n_pat, seq_len, 2 * PAIR_W), BF16)],
        compiler_params=pltpu.CompilerParams(dimension_semantics=("arbitrary", "arbitrary"),
                                             vmem_limit_bytes=VMEM_LIMIT_BYTES),
        name="dilated",
    )(view(q), view(k), view(v), bias)
    return out.reshape(N_PAIRS, batch * seq_len, PAIR_W)


def _flash_kernel(*refs, kw, decay, seq_len):
    if decay:
        q_ref, k_ref, v_ref, crow_ref, ccol_ref, o_ref, s_scr, p_scr, v_scr = refs
    else:
        q_ref, k_ref, v_ref, o_ref, s_scr, p_scr, v_scr = refs
    hp = pl.program_id(0)
    t = FLASH_T
    lane = lax.broadcasted_iota(jnp.int32, (1, kw), 1)
    in_h0 = (lane < HEAD_DIM) | ((lane >= PAIR_W) & (lane < PAIR_W + ROPE))
    in_h1 = ((lane >= HEAD_DIM) & (lane < PAIR_W)) | ((lane >= PAIR_W + ROPE) & (lane < PAIR_W + 2 * ROPE))
    row = lax.broadcasted_iota(jnp.int32, (t, t), 0)
    col = lax.broadcasted_iota(jnp.int32, (t, t), 1)
    causal = col <= row
    out_lane = lax.broadcasted_iota(jnp.int32, (1, PAIR_W), 1)
    v_scr[:, 0:PAIR_W] = v_ref[...]
    v_scr[:, PAIR_W:2 * PAIR_W] = jnp.ones((seq_len, PAIR_W), BF16)
    n_tiles = seq_len // t
    order = [n_tiles - 1 - i // 2 if i % 2 == 0 else i // 2 for i in range(n_tiles)]
    for step, qi in enumerate(order):
        rows = slice(qi * t, (qi + 1) * t)
        q = q_ref[rows, :]
        width = (qi + 1) * t
        buf = step % 2
        q2 = jnp.concatenate([jnp.where(sel, q, jnp.zeros_like(q)) for sel in (in_h0, in_h1)], axis=0)
        mx = [None, None]
        for c in range(qi + 1):
            cols = slice(c * t, (c + 1) * t)
            s2 = _nt_dot(q2, k_ref[cols, :])
            for j in range(2):
                s = s2[j * t:(j + 1) * t, :]
                if decay:
                    s = s - crow_ref[pl.ds(2 * hp + j, 1), cols] * LOG2E
                if c == qi:
                    s = jnp.where(causal, s, -jnp.inf)
                s_scr[buf, j * t:(j + 1) * t, cols] = s
                half = jnp.maximum(s[:, 0:t // 2], s[:, t // 2:t])
                mx[j] = half if mx[j] is None else jnp.maximum(mx[j], half)
        for j in range(2):
            shift = jnp.max(mx[j], axis=-1, keepdims=True)
            if decay:
                head = (2 * hp + j).astype(F32)
                cc = ccol_ref[rows, :]
                lane_c = lax.broadcasted_iota(jnp.int32, cc.shape, 1).astype(F32)
                cq = jnp.sum(jnp.where(lane_c == head, cc, 0.0), axis=-1, keepdims=True) * LOG2E
                shift = (shift + cq) - cq
            for c in range(qi + 1):
                cols = slice(c * t, (c + 1) * t)
                hrows = slice(j * t, (j + 1) * t)
                p_scr[buf, hrows, cols] = jnp.exp2(s_scr[buf, hrows, cols] - shift).astype(BF16)
        res = _dot(p_scr[buf, :, 0:width], v_scr[0:width, :])
        out = res[:, 0:PAIR_W] / res[:, PAIR_W:2 * PAIR_W]
        o_ref[rows, :] = jnp.where(out_lane < HEAD_DIM, out[0:t, :], out[t:2 * t, :]).astype(BF16)


def _flash_call(q, k, v, batch, seq_len, crow=None, ccol=None):
    kw = q.shape[-1]
    decay = crow is not None
    view = lambda a: a.reshape(N_PAIRS, batch, seq_len, a.shape[-1])
    args = [view(q), view(k), view(v)]
    seq_block = lambda w: pl.BlockSpec((None, None, seq_len, w), lambda p, b: (p, b, 0, 0))
    in_specs = [seq_block(kw), seq_block(kw), seq_block(PAIR_W)]
    if decay:
        args += [crow, ccol]
        in_specs += [pl.BlockSpec((HF_ROWS, seq_len), lambda p, b: (0, b)),
                     pl.BlockSpec((seq_len, LANES), lambda p, b: (b, 0))]
    out = pl.pallas_call(
        functools.partial(_flash_kernel, kw=kw, decay=decay, seq_len=seq_len),
        grid=(N_PAIRS, batch),
        in_specs=in_specs,
        out_specs=seq_block(PAIR_W),
        out_shape=jax.ShapeDtypeStruct((N_PAIRS, batch, seq_len, PAIR_W), BF16),
        scratch_shapes=[pltpu.VMEM((2, 2 * FLASH_T, seq_len), F32), pltpu.VMEM((2, 2 * FLASH_T, seq_len), BF16),
                        pltpu.VMEM((seq_len, 2 * PAIR_W), BF16)],
        compiler_params=pltpu.CompilerParams(dimension_semantics=("arbitrary",) * 2,
                                             vmem_limit_bytes=VMEM_LIMIT_BYTES),
        name="flash_fox" if decay else "flash_mla",
    )(*args)
    return out.reshape(N_PAIRS, batch * seq_len, PAIR_W)


def _merge_kernel(x_ref, oa_ref, ob_ref, oc_ref, gmix_ref, wg_ref, wb_ref, wo_ref, gffn_ref,
                  wr_ref, br_ref, x1_ref, gates_ref):
    x = x_ref[...]
    ms = jnp.mean(x * x, axis=-1, keepdims=True)
    xn = (x * lax.rsqrt(ms + EPS) * gmix_ref[...]).astype(BF16)
    mixed = None
    for g, o_ref in enumerate((oa_ref, ob_ref, oc_ref)):
        o = jnp.concatenate([o_ref[p] for p in range(N_PAIRS)], axis=-1)
        gate = jax.nn.sigmoid(_dot(xn, wg_ref[:, g * D_MODEL:(g + 1) * D_MODEL]))
        term = gate * _dot(o, wb_ref[g])
        mixed = term if mixed is None else mixed + term
    x1 = x + _dot(mixed.astype(BF16), wo_ref[...])
    x1_ref[...] = x1

    ms1 = jnp.mean(x1 * x1, axis=-1, keepdims=True)
    xf = x1 * lax.rsqrt(ms1 + EPS) * gffn_ref[...]
    hi, lo = _split_hi_lo(xf)
    tm = xf.shape[0]
    parts = _dot(jnp.concatenate([hi, lo], axis=0), wr_ref[...])
    lg = (parts[0:tm, 0:LANES] + parts[0:tm, LANES:2 * LANES]
          + parts[tm:2 * tm, 0:LANES] + parts[tm:2 * tm, LANES:2 * LANES]) + br_ref[...]
    lane = lax.broadcasted_iota(jnp.int32, lg.shape, 1)
    lane_f = lane.astype(F32)
    neg = -jnp.inf
    far = float(LANES)
    gl = jnp.where((lane >= N_EXPERTS) & (lane < N_EXPERTS + N_GROUPS), lg, neg)
    gmax = jnp.max(gl, axis=-1, keepdims=True)
    pg_top = 1.0 / jnp.sum(jnp.exp(gl - gmax), axis=-1, keepdims=True)
    gidx = jnp.min(jnp.where(gl == gmax, lane_f, far), axis=-1, keepdims=True) - float(N_EXPERTS)
    in_group = (lane < N_EXPERTS) & ((lane // EXPERTS_PER_GROUP).astype(F32) == gidx)
    ev = jnp.where(in_group, lg, neg)
    v1 = jnp.max(ev, axis=-1, keepdims=True)
    i1 = jnp.min(jnp.where(ev == v1, lane_f, far), axis=-1, keepdims=True)
    ev2 = jnp.where(lane_f == i1, neg, ev)
    v2 = jnp.max(ev2, axis=-1, keepdims=True)
    i2 = jnp.min(jnp.where(ev2 == v2, lane_f, far), axis=-1, keepdims=True)
    e2 = jnp.exp(v2 - v1)
    den = 1.0 + e2
    w1 = (1.0 / den) * pg_top
    w2 = (e2 / den) * pg_top
    gates_ref[...] = jnp.where(lane_f == i1, w1, jnp.where(lane_f == i2, w2,
                                                           jnp.where(lane == GROUP_LANE, gidx, 0.0)))


def _merge_call(x2d, oa, ob, oc, lw):
    t = x2d.shape[0]
    tm = MERGE_TM
    const = lambda shape: pl.BlockSpec(shape, lambda i: (0,) * len(shape), pipeline_mode=pl.Buffered(1))
    row_tile = lambda w: pl.BlockSpec((tm, w), lambda i: (i, 0))
    pair_in = pl.BlockSpec((N_PAIRS, tm, PAIR_W), lambda i: (0, i, 0))
    return pl.pallas_call(
        _merge_kernel,
        grid=(t // tm,),
        in_specs=[row_tile(D_MODEL), pair_in, pair_in, pair_in, const((1, D_MODEL)),
                  const((D_MODEL, 3 * D_MODEL)), const((3, BRANCH_W, D_MODEL)), const((D_MODEL, D_MODEL)),
                  const((1, D_MODEL)), const((D_MODEL, 2 * LANES)), const((1, LANES))],
        out_specs=[row_tile(D_MODEL), row_tile(LANES)],
        out_shape=[jax.ShapeDtypeStruct((t, D_MODEL), F32), jax.ShapeDtypeStruct((t, LANES), F32)],
        compiler_params=pltpu.CompilerParams(dimension_semantics=("arbitrary",),
                                             vmem_limit_bytes=VMEM_LIMIT_BYTES),
        name="merge",
    )(x2d, oa, ob, oc, lw["gmix"], lw["wg"], lw["wb"], lw["wo"], lw["gffn"], lw["wr"], lw["br"])


def _moe_kernel(x_ref, gates_ref, gffn_ref, lstrict_ref, ustrict_ref, wg_ref, wu_ref, wd_ref, o_ref,
                xs_ref, gs_ref, ys_ref):
    x = x_ref[...]
    tm = x.shape[0]
    ms = jnp.mean(x * x, axis=-1, keepdims=True)
    xn = (x * lax.rsqrt(ms + EPS) * gffn_ref[...]).astype(BF16)

    gates = gates_ref[...]
    lane = lax.broadcasted_iota(jnp.int32, gates.shape, 1)
    gid = jnp.sum(jnp.where(lane == GROUP_LANE, gates, 0.0), axis=-1, keepdims=True)
    onehot = jnp.where((lane.astype(F32) == gid) & (lane < N_GROUPS), 1.0, 0.0)
    before = _dot(lstrict_ref[...], onehot.astype(BF16))
    counts = before[tm - 1:tm, :] + onehot[tm - 1:tm, :]
    n_chunks = jnp.floor((counts + (MOE_CHUNK - 1.0)) * (1.0 / MOE_CHUNK))
    seg_start = _dot(jnp.broadcast_to(n_chunks, (SUBLANES, LANES)).astype(BF16), ustrict_ref[...])[0:1, :] * MOE_CHUNK
    rank = jnp.sum(onehot * (before + seg_start), axis=-1, keepdims=True)
    rank_row = jnp.transpose(jnp.broadcast_to(rank, (tm, LANES)))[0:1, :]
    slot_col = lax.broadcasted_iota(jnp.int32, (MOE_ROWS, 1), 0).astype(F32)
    slot_row = lax.broadcasted_iota(jnp.int32, (1, MOE_ROWS), 1).astype(F32)
    perm = jnp.where(slot_col == rank_row, 1.0, 0.0).astype(BF16)
    perm_t = jnp.where(rank == slot_row, 1.0, 0.0).astype(BF16)

    xs_ref[...] = _dot(perm, xn).astype(BF16)
    g_hi, g_lo = _split_hi_lo(gates)
    g_sorted = _dot(perm, jnp.concatenate([g_hi, g_lo], axis=-1))
    gs_ref[...] = g_sorted[:, 0:LANES] + g_sorted[:, LANES:2 * LANES]
    ys_ref[...] = jnp.zeros_like(ys_ref)

    nc = [n_chunks[0, g].astype(jnp.int32) for g in range(N_GROUPS)]
    ends = [nc[0], nc[0] + nc[1], nc[0] + nc[1] + nc[2]]
    total = ends[2] + nc[3]

    def chunk(k, carry):
        g = sum((k >= e).astype(jnp.int32) for e in ends)
        r0 = pl.multiple_of(k * MOE_CHUNK, MOE_CHUNK)
        xc = xs_ref[pl.ds(r0, MOE_CHUNK), :]
        gc = gs_ref[pl.ds(r0, MOE_CHUNK), :]
        lane_c = lax.broadcasted_iota(jnp.int32, gc.shape, 1)
        parts = []
        for e in range(EXPERTS_PER_GROUP):
            h_gate = _dot(xc, wg_ref[g * EXPERTS_PER_GROUP + e])
            h_up = _dot(xc, wu_ref[g * EXPERTS_PER_GROUP + e])
            gate = jnp.sum(jnp.where(lane_c == g * EXPERTS_PER_GROUP + e, gc, 0.0), axis=-1, keepdims=True)
            parts.append((jax.nn.silu(h_gate) * h_up * gate).astype(BF16))
        y = sum(_dot(parts[e], wd_ref[g * EXPERTS_PER_GROUP + e]) for e in range(EXPERTS_PER_GROUP))
        ys_ref[pl.ds(r0, MOE_CHUNK), :] = y.astype(BF16)
        return carry

    lax.fori_loop(0, total, chunk, 0)
    o_ref[...] = x + _dot(perm_t, ys_ref[...])


def _moe_call(x1, gates, cw, lw):
    t = x1.shape[0]
    tm = MOE_TM
    const = lambda shape: pl.BlockSpec(shape, lambda i: (0,) * len(shape))
    resident = lambda shape: pl.BlockSpec(shape, lambda i: (0,) * len(shape), pipeline_mode=pl.Buffered(1))
    return pl.pallas_call(
        _moe_kernel,
        grid=(t // tm,),
        in_specs=[pl.BlockSpec((tm, D_MODEL), lambda i: (i, 0)),
                  pl.BlockSpec((tm, LANES), lambda i: (i, 0)),
                  const((1, D_MODEL)), const((tm, tm)), const((LANES, LANES)),
                  resident((N_EXPERTS, D_MODEL, D_FF)), resident((N_EXPERTS, D_MODEL, D_FF)),
                  resident((N_EXPERTS, D_FF, D_MODEL))],
        out_specs=pl.BlockSpec((tm, D_MODEL), lambda i: (i, 0)),
        out_shape=jax.ShapeDtypeStruct((t, D_MODEL), F32),
        scratch_shapes=[pltpu.VMEM((MOE_ROWS, D_MODEL), BF16), pltpu.VMEM((MOE_ROWS, LANES), F32),
                        pltpu.VMEM((MOE_ROWS, D_MODEL), BF16)],
        compiler_params=pltpu.CompilerParams(dimension_semantics=("arbitrary",),
                                             vmem_limit_bytes=VMEM_LIMIT_BYTES),
        name="moe",
    )(x1, gates, lw["gffn"], cw["lstrict"], cw["ustrict"], lw["weg"], lw["weu"], lw["wed"])


def _t5_bucket(dist):
    max_exact = N_BUCKETS // 2
    log_ratio = np.log(np.maximum(dist, max_exact) / max_exact) / np.log(MAX_DISTANCE / max_exact)
    large = np.minimum(max_exact + (log_ratio * (N_BUCKETS - max_exact)).astype(np.int32), N_BUCKETS - 1)
    return np.where(dist < max_exact, dist, large).astype(np.int32)


def _dilated_bias(rel_bias):
    tables = []
    span = 3 * Q_BLOCK
    for window, dil in DILATED_PAIRS:
        assert window // dil == Q_BLOCK
        per_dist = rel_bias[_t5_bucket(np.arange(Q_BLOCK + 1) * dil)].astype(F32).T * LOG2E
        diag = jnp.concatenate([jnp.full((N_HEADS, Q_BLOCK), -jnp.inf, F32), per_dist[:, ::-1],
                                jnp.full((N_HEADS, span - 2 * Q_BLOCK), -jnp.inf, F32)], axis=1)
        skew = jnp.tile(diag, (1, Q_BLOCK))[:, :Q_BLOCK * span].reshape(N_HEADS, Q_BLOCK, span)
        tables.append(skew[:, :, Q_BLOCK:])
    return jnp.stack(tables, axis=0)


def _const_weights(positions, rel_bias):
    b, s = positions.shape
    inv_freq = ROPE_THETA ** (-jnp.arange(ROPE_HALF, dtype=F32) / ROPE_HALF)
    ang = positions.astype(F32).reshape(b * s, 1) * inv_freq
    cos, sin = jnp.cos(ang), jnp.sin(ang)
    idx = np.arange(GROUP_SLAB)
    return {
        "cos": cos, "sin": sin,
        "g64": jnp.asarray(idx[:, None] // HEAD_DIM == idx[None, :] // HEAD_DIM, BF16),
        "g32": jnp.asarray(idx[:, None] // ROPE == idx[None, :] // ROPE, BF16),
        "tri": jnp.asarray(np.arange(FRONT_TM)[None, :] <= np.arange(FRONT_TM)[:, None], BF16),
        "lstrict": jnp.asarray(np.arange(MOE_TM)[None, :] < np.arange(MOE_TM)[:, None], BF16),
        "ustrict": jnp.asarray(np.arange(LANES)[:, None] < np.arange(LANES)[None, :], BF16),
        "bias": _dilated_bias(rel_bias),
    }


def _layer_weights(l, p):
    w_in = p["w_in"][l]
    o_hf = C_CQ
    o_cq = o_hf + N_HEADS
    o_ckv = o_cq + Q_LORA
    o_kr = o_ckv + KV_LORA
    o_g = o_kr + ROPE
    kr1 = w_in[:, o_kr:o_kr + ROPE_HALF]
    kr2 = w_in[:, o_kr + ROPE_HALF:o_kr + ROPE]
    hf_w = w_in[:, o_hf:o_cq]
    w_small = jnp.concatenate([
        w_in[:, o_cq:o_ckv], w_in[:, o_ckv:o_kr], _rope_a(kr1, kr2), _rope_a(kr2, kr1),
        jnp.pad(hf_w, ((0, 0), (0, LANES - N_HEADS))),
    ], axis=1).astype(BF16)

    wuq = p["w_uq"][l].reshape(Q_LORA, N_HEADS, NOPE + ROPE)
    q_nope = wuq[:, :, :NOPE].reshape(Q_LORA, N_HEADS * NOPE)
    q1 = wuq[:, :, NOPE:NOPE + ROPE_HALF]
    q2 = wuq[:, :, NOPE + ROPE_HALF:]

    def pair_rope(v1, v2):
        x = jnp.concatenate([v1, v2], axis=-1).reshape(Q_LORA, N_PAIRS, 2 * ROPE)
        return jnp.pad(x, ((0, 0), (0, 0), (0, LANES - 2 * ROPE))).reshape(Q_LORA, N_PAIRS * LANES)

    wuq_p = jnp.concatenate([q_nope, pair_rope(q1, q2), pair_rope(q2, q1)], axis=1).astype(BF16)
    wukv = p["w_ukv"][l].reshape(KV_LORA, N_HEADS, NOPE + HEAD_DIM)
    wukv_p = jnp.concatenate([wukv[:, :, :NOPE].reshape(KV_LORA, BRANCH_W),
                              wukv[:, :, NOPE:].reshape(KV_LORA, BRANCH_W)], axis=1).astype(BF16)

    tile8 = lambda g: jnp.tile(g, N_HEADS)
    sc_ab = LOG2E / math.sqrt(HEAD_DIM)
    sc_c = LOG2E / math.sqrt(NOPE + ROPE)
    gq_c, gk_c = p["gq_c"][l], p["gk_c"][l]
    zeros512 = jnp.zeros((BRANCH_W,), F32)
    g512 = jnp.stack([tile8(p["gq_a"][l]) * sc_ab, tile8(p["gk_a"][l]),
                      tile8(p["gq_b"][l]) * sc_ab, tile8(p["gk_b"][l]),
                      tile8(gq_c[:NOPE]) * sc_c, tile8(gk_c[:NOPE]), zeros512, zeros512])
    gq1, gq2 = gq_c[NOPE:NOPE + ROPE_HALF] * sc_c, gq_c[NOPE + ROPE_HALF:] * sc_c
    gk1, gk2 = gk_c[NOPE:NOPE + ROPE_HALF], gk_c[NOPE + ROPE_HALF:]
    zeros128 = jnp.zeros((LANES,), F32)
    bf = p["b_forget"][l].astype(F32)
    g128 = jnp.stack([_rope_a(gq1, gq2), _rope_a(gq2, gq1), _rope_a(gk1, gk2), _rope_a(gk2, gk1),
                      jnp.pad(bf, (0, LANES - N_HEADS)), zeros128, zeros128, zeros128])

    wr = jnp.concatenate([p["w_router_expert"][l], p["w_router_group"][l]], axis=1)
    wr = jnp.pad(wr, ((0, 0), (0, LANES - N_EXPERTS - N_GROUPS))).astype(F32)
    wrh = wr.astype(BF16)
    wrl = (wr - wrh.astype(F32)).astype(BF16)
    br = jnp.pad(jnp.concatenate([p["b_router_expert"][l], p["b_router_group"][l]]).astype(F32),
                 (0, LANES - N_EXPERTS - N_GROUPS))[None, :]
    return {
        "gmix": p["norm_mix"][l][None, :], "wab": w_in[:, 0:C_CQ].astype(BF16), "ws": w_small, "wuq": wuq_p, "wukv": wukv_p,
        "g512": g512, "g128": g128, "ncq": p["norm_cq"][l][None, :], "nckv": p["norm_ckv"][l][None, :],
        "wg": w_in[:, o_g:].astype(BF16), "wb": p["w_branch"][l].astype(BF16), "wo": p["w_out"][l].astype(BF16),
        "gffn": p["norm_ffn"][l][None, :], "wr": jnp.concatenate([wrh, wrl], axis=1), "br": br,
        "weg": p["w_expert_gate"][l].astype(BF16), "weu": p["w_expert_up"][l].astype(BF16),
        "wed": p["w_expert_down"][l].astype(BF16),
    }


def kernel(x, positions, rel_bias, norm_mix, w_in, b_forget, gq_a, gk_a, gq_b, gk_b, gq_c, gk_c, norm_cq, norm_ckv, w_uq, w_ukv, w_branch, w_out, norm_ffn, w_router_group, b_router_group, w_router_expert, b_router_expert, w_expert_gate, w_expert_up, w_expert_down):
    batch, seq_len, d_model = x.shape
    assert d_model == D_MODEL and seq_len % (Q_BLOCK * DILATED_PAIRS[-1][1]) == 0
    assert (batch * seq_len) % max(FRONT_TM, MERGE_TM, MOE_TM) == 0 and seq_len % FRONT_TM == 0
    p = dict(norm_mix=norm_mix, w_in=w_in, b_forget=b_forget, gq_a=gq_a, gk_a=gk_a, gq_b=gq_b, gk_b=gk_b,
             gq_c=gq_c, gk_c=gk_c, norm_cq=norm_cq, norm_ckv=norm_ckv, w_uq=w_uq, w_ukv=w_ukv,
             w_branch=w_branch, w_out=w_out, norm_ffn=norm_ffn, w_router_group=w_router_group,
             b_router_group=b_router_group, w_router_expert=w_router_expert, b_router_expert=b_router_expert,
             w_expert_gate=w_expert_gate, w_expert_up=w_expert_up, w_expert_down=w_expert_down)
    cw = _const_weights(positions, rel_bias)
    xs = x.reshape(batch * seq_len, d_model)
    for l in range(norm_mix.shape[0]):
        lw = _layer_weights(l, p)
        qa, ka, va, qb, kb, vb, qc, kc, vc, ccol, crow = _front_call(xs, cw, lw, seq_len)
        oa = _dilated_call(qa, ka, va, cw["bias"], batch, seq_len)
        ob = _flash_call(qb, kb, vb, batch, seq_len, crow, ccol)
        oc = _flash_call(qc, kc, vc, batch, seq_len)
        x1, gates = _merge_call(xs, oa, ob, oc, lw)
        xs = _moe_call(x1, gates, cw, lw)
    return xs.reshape(batch, seq_len, d_model)
```

```python
import functools
import math

import numpy as np
import jax
import jax.numpy as jnp
from jax import lax
from jax.experimental import pallas as pl
from jax.experimental.pallas import tpu as pltpu

F32 = jnp.float32
BF16 = jnp.bfloat16

D_MODEL = 1024
N_HEADS = 8
HEAD_DIM = 64
N_PAIRS = N_HEADS // 2
PAIR_W = 2 * HEAD_DIM
BRANCH_W = N_HEADS * HEAD_DIM
DILATED_PAIRS = ((128, 1), (512, 4), (2048, 16))
Q_LORA = 256
KV_LORA = 128
NOPE = 64
ROPE = 32
ROPE_HALF = ROPE // 2
ROPE_THETA = 10000.0
N_BUCKETS = 32
MAX_DISTANCE = 2048
Q_BLOCK = 128
N_GROUPS = 4
EXPERTS_PER_GROUP = 4
N_EXPERTS = 16
D_FF = 256
EPS = 1e-6
LOG2E = math.log2(math.e)
LANES = 128
SUBLANES = 8

VMEM_LIMIT_BYTES = 56 * 1024 * 1024

FRONT_TM = 512
MERGE_TM = 1024
MOE_TM = 512
MOE_CHUNK = 144
MOE_CHUNKS_MAX = -(-MOE_TM // MOE_CHUNK)
MOE_SLOTS = MOE_CHUNKS_MAX + N_GROUPS - 1
MOE_ROWS = -(-MOE_SLOTS * MOE_CHUNK // LANES) * LANES
BF16_ROWS = 16
GROUP_LANE = N_EXPERTS
FLASH_T = 256

C_A = 0
C_B = 3 * BRANCH_W
C_CQ = 6 * BRANCH_W
C_CKV = C_CQ + Q_LORA
C_KRA = C_CKV + KV_LORA
C_KRB = C_KRA + LANES
C_HF = C_KRB + LANES
N_FRONT = C_HF + LANES
HF_ROWS = 16
GROUP_SLAB = 256


def _nt_dot(a, b):
    return lax.dot_general(a, b, (((1,), (1,)), ((), ())), preferred_element_type=F32)


def _dot(a, b):
    return jnp.dot(a, b, preferred_element_type=F32)


def _log_sigmoid(x):
    return jnp.minimum(x, 0.0) - jnp.log1p(jnp.exp(-jnp.abs(x)))


def _split_hi_lo(x):
    hi = x.astype(BF16)
    lo = (x - hi.astype(F32)).astype(BF16)
    return hi, lo


def _rope_a(v1, v2):
    pad = jnp.zeros(v1.shape[:-1] + (LANES - 4 * ROPE_HALF,), v1.dtype)
    return jnp.concatenate([v1, v2, v1, v2, pad], axis=-1)


def _front_kernel(x_ref, gmix_ref, wab_ref, ws_ref, wuq_ref, wukv_ref, g64_ref, g32_ref, tri_ref,
                  g512_ref, g128_ref, ncq_ref, nckv_ref, cos_ref, sin_ref,
                  qa_ref, ka_ref, va_ref, qb_ref, kb_ref, vb_ref, qc_ref, kc_ref, vc_ref,
                  ccol_ref, crow_ref, carry_col, *, tiles_per_seq):
    i = pl.program_id(0)

    @pl.when(i % tiles_per_seq == 0)
    def _():
        carry_col[...] = jnp.zeros_like(carry_col)

    x = x_ref[...]
    ms = jnp.mean(x * x, axis=-1, keepdims=True)
    xn = (x * lax.rsqrt(ms + EPS) * gmix_ref[...]).astype(BF16)

    def group_sums(h, gmat):
        sq = (h * h).astype(BF16)
        slab = gmat.shape[0]
        return jnp.concatenate([_dot(sq[:, i:i + slab], gmat) for i in range(0, h.shape[1], slab)], axis=-1)

    def group_norm(h, gmat, n, gain):
        return h * lax.rsqrt(group_sums(h, gmat) * (1.0 / n) + EPS) * gain

    def store_pairs(ref, val, lo=0, w=PAIR_W):
        for p in range(N_PAIRS):
            ref[p, :, lo:lo + w] = val[:, p * PAIR_W:(p + 1) * PAIR_W].astype(BF16)

    g64 = g64_ref[...]
    for base, q_ref, k_ref, v_ref, row in ((C_A, qa_ref, ka_ref, va_ref, 0), (C_B, qb_ref, kb_ref, vb_ref, 2)):
        hq = _dot(xn, wab_ref[:, base:base + BRANCH_W])
        store_pairs(q_ref, group_norm(hq, g64, HEAD_DIM, g512_ref[row:row + 1, :]))
        hk = _dot(xn, wab_ref[:, base + BRANCH_W:base + 2 * BRANCH_W])
        store_pairs(k_ref, group_norm(hk, g64, HEAD_DIM, g512_ref[row + 1:row + 2, :]))
        hv = _dot(xn, wab_ref[:, base + 2 * BRANCH_W:base + 3 * BRANCH_W])
        store_pairs(v_ref, hv)

    hs = _dot(xn, ws_ref[...])
    cos, sin = cos_ref[...], sin_ref[...]
    ca = _rope_a(cos, cos)
    sb = _rope_a(-sin, sin)

    hcq = hs[:, 0:Q_LORA]
    cq = (hcq * lax.rsqrt(jnp.mean(hcq * hcq, axis=-1, keepdims=True) + EPS) * ncq_ref[...]).astype(BF16)
    qc = _dot(cq, wuq_ref[...])
    qn = group_norm(qc[:, 0:BRANCH_W], g64, NOPE, g512_ref[4:5, :])
    store_pairs(qc_ref, qn, 0)
    qra = qc[:, BRANCH_W:2 * BRANCH_W]
    qrb = qc[:, 2 * BRANCH_W:3 * BRANCH_W]
    rs = lax.rsqrt(group_sums(qra, g32_ref[...]) * (1.0 / ROPE) + EPS)
    ga = g128_ref[0:1, :]
    gb = g128_ref[1:2, :]
    for p in range(N_PAIRS):
        sl = slice(p * PAIR_W, (p + 1) * PAIR_W)
        qr = (qra[:, sl] * ga * ca + qrb[:, sl] * gb * sb) * rs[:, sl]
        qc_ref[p, :, PAIR_W:2 * PAIR_W] = qr.astype(BF16)

    hckv = hs[:, Q_LORA:Q_LORA + KV_LORA]
    ckv = (hckv * lax.rsqrt(jnp.mean(hckv * hckv, axis=-1, keepdims=True) + EPS) * nckv_ref[...]).astype(BF16)
    kv = _dot(ckv, wukv_ref[...])
    store_pairs(kc_ref, group_norm(kv[:, 0:BRANCH_W], g64, NOPE, g512_ref[5:6, :]), 0)
    store_pairs(vc_ref, kv[:, BRANCH_W:2 * BRANCH_W])
    kra = hs[:, C_KRA - C_CQ:C_KRA - C_CQ + LANES]
    krb = hs[:, C_KRB - C_CQ:C_KRB - C_CQ + LANES]
    rsk = lax.rsqrt(_dot((kra * kra).astype(BF16), g32_ref[0:LANES, 0:LANES]) * (1.0 / ROPE) + EPS)
    kr = ((kra * g128_ref[2:3, :] * ca + krb * g128_ref[3:4, :] * sb) * rsk).astype(BF16)
    for p in range(N_PAIRS):
        kc_ref[p, :, PAIR_W:2 * PAIR_W] = kr

    tri = tri_ref[...]
    hf_col = hs[:, C_HF - C_CQ:C_HF - C_CQ + LANES] + g128_ref[4:5, :]
    hi, lo = _split_hi_lo(_log_sigmoid(hf_col))
    ccol = _dot(tri, hi) + _dot(tri, lo) + carry_col[0:1, :]
    ccol_ref[...] = ccol
    tm = ccol.shape[0]
    carry_col[...] = jnp.broadcast_to(ccol[tm - 1:tm, :], carry_col.shape)

    crow_ref[...] = jnp.transpose(ccol)[0:HF_ROWS, :]


def _front_call(x2d, cw, lw, seq_len):
    t = x2d.shape[0]
    tm = FRONT_TM
    const = lambda shape: pl.BlockSpec(shape, lambda i: (0,) * len(shape))
    pair_out = lambda w: pl.BlockSpec((N_PAIRS, tm, w), lambda i: (0, i, 0))
    pair_shape = lambda w: jax.ShapeDtypeStruct((N_PAIRS, t, w), BF16)
    in_specs = [
        pl.BlockSpec((tm, D_MODEL), lambda i: (i, 0)),
        const((1, D_MODEL)),
        const((D_MODEL, C_CQ)),
        const((D_MODEL, N_FRONT - C_CQ)),
        const((Q_LORA, 3 * BRANCH_W)),
        const((KV_LORA, 2 * BRANCH_W)),
        const((GROUP_SLAB, GROUP_SLAB)),
        const((GROUP_SLAB, GROUP_SLAB)),
        const((tm, tm)),
        const((SUBLANES, BRANCH_W)),
        const((SUBLANES, LANES)),
        const((1, Q_LORA)),
        const((1, KV_LORA)),
        pl.BlockSpec((tm, ROPE_HALF), lambda i: (i, 0)),
        pl.BlockSpec((tm, ROPE_HALF), lambda i: (i, 0)),
    ]
    out_specs = [pair_out(PAIR_W)] * 6 + [pair_out(2 * PAIR_W), pair_out(2 * PAIR_W), pair_out(PAIR_W),
                                           pl.BlockSpec((tm, LANES), lambda i: (i, 0)),
                                           pl.BlockSpec((HF_ROWS, tm), lambda i: (0, i))]
    out_shape = [pair_shape(PAIR_W)] * 6 + [pair_shape(2 * PAIR_W), pair_shape(2 * PAIR_W), pair_shape(PAIR_W),
                                            jax.ShapeDtypeStruct((t, LANES), F32),
                                            jax.ShapeDtypeStruct((HF_ROWS, t), F32)]
    return pl.pallas_call(
        functools.partial(_front_kernel, tiles_per_seq=seq_len // tm),
        grid=(t // tm,),
        in_specs=in_specs,
        out_specs=out_specs,
        out_shape=out_shape,
        scratch_shapes=[pltpu.VMEM((SUBLANES, LANES), F32)],
        compiler_params=pltpu.CompilerParams(dimension_semantics=("arbitrary",),
                                             vmem_limit_bytes=VMEM_LIMIT_BYTES),
        name="front",
    )(x2d, lw["gmix"], lw["wab"], lw["ws"], lw["wuq"], lw["wukv"], cw["g64"], cw["g32"], cw["tri"],
      lw["g512"], lw["g128"], lw["ncq"], lw["nckv"], cw["cos"], cw["sin"])


def _dilated_kernel(q_ref, k_ref, v_ref, bias_ref, o_ref, oacc, lacc, s_scr, nat, res4, g4, g16, vx, *, seq_len):
    lane = lax.broadcasted_iota(jnp.int32, (1, PAIR_W), 1)
    left = lane < HEAD_DIM
    d_mid, d_far = DILATED_PAIRS[1][1], DILATED_PAIRS[2][1]
    assert DILATED_PAIRS[0][1] == 1 and d_far == d_mid * d_mid
    n_mid, n_far = seq_len // d_mid, seq_len // d_far

    vx[:, :, PAIR_W:2 * PAIR_W] = jnp.ones((len(DILATED_PAIRS), seq_len, PAIR_W), BF16)
    vx[0, :, 0:PAIR_W] = v_ref[...]
    for ti, src in enumerate((q_ref, k_ref, v_ref)):
        nat[ti] = src[...].astype(F32)

    def put(ti, pattern, lo, hi, rows):
        if ti == 2:
            vx[pattern, lo:hi, 0:PAIR_W] = rows.astype(BF16)
        else:
            (g4, g16)[pattern - 1][ti, lo:hi, :] = rows.astype(BF16)

    for ti in range(3):
        for c in range(d_mid):
            rows = nat[ti, pl.ds(c, n_mid, stride=d_mid), :]
            res4[ti, c * n_mid:(c + 1) * n_mid, :] = rows
            put(ti, 1, c * n_mid, (c + 1) * n_mid, rows)
    for ti in range(3):
        for c in range(d_far):
            rows = res4[ti, pl.ds((c % d_mid) * n_mid + c // d_mid, n_far, stride=d_mid), :]
            put(ti, 2, c * n_far, (c + 1) * n_far, rows)

    readers = (
        (1, lambda ti, lo, hi: (q_ref, k_ref)[ti][lo:hi, :]),
        (d_mid, lambda ti, lo, hi: g4[ti, lo:hi, :]),
        (d_far, lambda ti, lo, hi: g16[ti, lo:hi, :]),
    )
    for di, (d, read) in enumerate(readers):
        n_sub = seq_len // d
        n_blocks = n_sub // Q_BLOCK
        blocks = [(c, nb) for c in range(d) for nb in range(n_blocks)]

        def window(c, nb):
            k_lo = max(nb - 1, 0) * Q_BLOCK
            return c * n_sub + k_lo, c * n_sub + (nb + 1) * Q_BLOCK

        for bi, (c, nb) in enumerate(blocks):
            q = read(0, c * n_sub + nb * Q_BLOCK, c * n_sub + (nb + 1) * Q_BLOCK)
            kwin = read(1, *window(c, nb))
            w = kwin.shape[0]
            for j in range(2):
                qh = jnp.where(left if j == 0 else jnp.logical_not(left), q, jnp.zeros_like(q))
                s_scr[j, bi * Q_BLOCK:(bi + 1) * Q_BLOCK, 0:w] = (
                    _nt_dot(qh, kwin) + bias_ref[di, j, :, 2 * Q_BLOCK - w:])
        for bi, (c, nb) in enumerate(blocks):
            lo, hi = window(c, nb)
            w = hi - lo
            outs, lses = [], []
            for j in range(2):
                s = s_scr[j, bi * Q_BLOCK:(bi + 1) * Q_BLOCK, 0:w]
                m = jnp.max(s, axis=-1, keepdims=True)
                p = jnp.exp2(s - m).astype(BF16)
                res = _dot(p, vx[di, lo:hi, :])
                den = res[:, PAIR_W:2 * PAIR_W]
                outs.append(res[:, 0:PAIR_W] / den)
                lses.append(m + jnp.log2(den))
            start = nb * Q_BLOCK * d + c
            rows = pl.ds(start, Q_BLOCK) if d == 1 else pl.ds(start, Q_BLOCK, stride=d)
            oacc[di, rows, :] = jnp.where(left, outs[0], outs[1])
            lacc[di, rows, :] = jnp.where(left, lses[0], lses[1])

    l0, l1, l2 = lacc[0], lacc[1], lacc[2]
    m = jnp.maximum(jnp.maximum(l0, l1), l2)
    w0, w1, w2 = jnp.exp2(l0 - m), jnp.exp2(l1 - m), jnp.exp2(l2 - m)
    wsum = w0 + w1 + w2
    o = oacc[0] * (w0 / wsum) + oacc[1] * (w1 / wsum) + oacc[2] * (w2 / wsum)
    o_ref[...] = o.astype(BF16)


def _dilated_call(q, k, v, bias, batch, seq_len):
    n_pat = len(DILATED_PAIRS)
    seq_block = pl.BlockSpec((None, None, seq_len, PAIR_W), lambda p, b: (p, b, 0, 0))
    view = lambda a: a.reshape(N_PAIRS, batch, seq_len, PAIR_W)
    seq_scratch = lambda n, dt: pltpu.VMEM((n, seq_len, PAIR_W), dt)
    out = pl.pallas_call(
        functools.partial(_dilated_kernel, seq_len=seq_len),
        grid=(N_PAIRS, batch),
        in_specs=[seq_block, seq_block, seq_block,
                  pl.BlockSpec((n_pat, 2, Q_BLOCK, 2 * Q_BLOCK), lambda p, b: (0, p, 0, 0))],
        out_specs=seq_block,
        out_shape=jax.ShapeDtypeStruct((N_PAIRS, batch, seq_len, PAIR_W), BF16),
        scratch_shapes=[seq_scratch(n_pat, F32), seq_scratch(n_pat, F32),
                        pltpu.VMEM((2, seq_len, 2 * Q_BLOCK), F32),
                        seq_scratch(3, F32), seq_scratch(3, F32), seq_scratch(2, BF16), seq_scratch(2, BF16),
                        pltpu.VMEM((n_pat, seq_len, 2 * PAIR_W), BF16)],
        compiler_params=pltpu.CompilerParams(dimension_semantics=("arbitrary", "arbitrary"),
                                             vmem_limit_bytes=VMEM_LIMIT_BYTES),
        name="dilated",
    )(view(q), view(k), view(v), bias)
    return out.reshape(N_PAIRS, batch * seq_len, PAIR_W)


def _flash_kernel(*refs, kw, decay, seq_len):
    if decay:
        q_ref, k_ref, v_ref, crow_ref, ccol_ref, o_ref, s_scr, p_scr, v_scr = refs
    else:
        q_ref, k_ref, v_ref, o_ref, s_scr, p_scr, v_scr = refs
    hp = pl.program_id(0)
    t = FLASH_T
    lane = lax.broadcasted_iota(jnp.int32, (1, kw), 1)
    in_h0 = (lane < HEAD_DIM) | ((lane >= PAIR_W) & (lane < PAIR_W + ROPE))
    in_h1 = ((lane >= HEAD_DIM) & (lane < PAIR_W)) | ((lane >= PAIR_W + ROPE) & (lane < PAIR_W + 2 * ROPE))
    row = lax.broadcasted_iota(jnp.int32, (t, t), 0)
    col = lax.broadcasted_iota(jnp.int32, (t, t), 1)
    causal = col <= row
    out_lane = lax.broadcasted_iota(jnp.int32, (1, PAIR_W), 1)
    v_scr[:, 0:PAIR_W] = v_ref[...]
    v_scr[:, PAIR_W:2 * PAIR_W] = jnp.ones((seq_len, PAIR_W), BF16)
    n_tiles = seq_len // t
    order = [n_tiles - 1 - i // 2 if i % 2 == 0 else i // 2 for i in range(n_tiles)]
    for step, qi in enumerate(order):
        rows = slice(qi * t, (qi + 1) * t)
        q = q_ref[rows, :]
        width = (qi + 1) * t
        buf = step % 2
        q2 = jnp.concatenate([jnp.where(sel, q, jnp.zeros_like(q)) for sel in (in_h0, in_h1)], axis=0)
        mx = [None, None]
        for c in range(qi + 1):
            cols = slice(c * t, (c + 1) * t)
            s2 = _nt_dot(q2, k_ref[cols, :])
            for j in range(2):
                s = s2[j * t:(j + 1) * t, :]
                if decay:
                    s = s - crow_ref[pl.ds(2 * hp + j, 1), cols] * LOG2E
                if c == qi:
                    s = jnp.where(causal, s, -jnp.inf)
                s_scr[buf, j * t:(j + 1) * t, cols] = s
                half = jnp.maximum(s[:, 0:t // 2], s[:, t // 2:t])
                mx[j] = half if mx[j] is None else jnp.maximum(mx[j], half)
        for j in range(2):
            shift = jnp.max(mx[j], axis=-1, keepdims=True)
            if decay:
                head = (2 * hp + j).astype(F32)
                cc = ccol_ref[rows, :]
                lane_c = lax.broadcasted_iota(jnp.int32, cc.shape, 1).astype(F32)
                cq = jnp.sum(jnp.where(lane_c == head, cc, 0.0), axis=-1, keepdims=True) * LOG2E
                shift = (shift + cq) - cq
            for c in range(qi + 1):
                cols = slice(c * t, (c + 1) * t)
                hrows = slice(j * t, (j + 1) * t)
                p_scr[buf, hrows, cols] = jnp.exp2(s_scr[buf, hrows, cols] - shift).astype(BF16)
        res = _dot(p_scr[buf, :, 0:width], v_scr[0:width, :])
        out = res[:, 0:PAIR_W] / res[:, PAIR_W:2 * PAIR_W]
        o_ref[rows, :] = jnp.where(out_lane < HEAD_DIM, out[0:t, :], out[t:2 * t, :]).astype(BF16)


def _flash_call(q, k, v, batch, seq_len, crow=None, ccol=None):
    kw = q.shape[-1]
    decay = crow is not None
    view = lambda a: a.reshape(N_PAIRS, batch, seq_len, a.shape[-1])
    args = [view(q), view(k), view(v)]
    seq_block = lambda w: pl.BlockSpec((None, None, seq_len, w), lambda p, b: (p, b, 0, 0))
    in_specs = [seq_block(kw), seq_block(kw), seq_block(PAIR_W)]
    if decay:
        args += [crow, ccol]
        in_specs += [pl.BlockSpec((HF_ROWS, seq_len), lambda p, b: (0, b)),
                     pl.BlockSpec((seq_len, LANES), lambda p, b: (b, 0))]
    out = pl.pallas_call(
        functools.partial(_flash_kernel, kw=kw, decay=decay, seq_len=seq_len),
        grid=(N_PAIRS, batch),
        in_specs=in_specs,
        out_specs=seq_block(PAIR_W),
        out_shape=jax.ShapeDtypeStruct((N_PAIRS, batch, seq_len, PAIR_W), BF16),
        scratch_shapes=[pltpu.VMEM((2, 2 * FLASH_T, seq_len), F32), pltpu.VMEM((2, 2 * FLASH_T, seq_len), BF16),
                        pltpu.VMEM((seq_len, 2 * PAIR_W), BF16)],
        compiler_params=pltpu.CompilerParams(dimension_semantics=("arbitrary",) * 2,
                                             vmem_limit_bytes=VMEM_LIMIT_BYTES),
        name="flash_fox" if decay else "flash_mla",
    )(*args)
    return out.reshape(N_PAIRS, batch * seq_len, PAIR_W)


def _merge_kernel(x_ref, oa_ref, ob_ref, oc_ref, gmix_ref, wg_ref, wb_ref, wo_ref, gffn_ref,
                  wr_ref, br_ref, x1_ref, gates_ref):
    x = x_ref[...]
    ms = jnp.mean(x * x, axis=-1, keepdims=True)
    xn = (x * lax.rsqrt(ms + EPS) * gmix_ref[...]).astype(BF16)
    mixed = None
    for g, o_ref in enumerate((oa_ref, ob_ref, oc_ref)):
        o = jnp.concatenate([o_ref[p] for p in range(N_PAIRS)], axis=-1)
        gate = jax.nn.sigmoid(_dot(xn, wg_ref[:, g * D_MODEL:(g + 1) * D_MODEL]))
        term = gate * _dot(o, wb_ref[g])
        mixed = term if mixed is None else mixed + term
    x1 = x + _dot(mixed.astype(BF16), wo_ref[...])
    x1_ref[...] = x1

    ms1 = jnp.mean(x1 * x1, axis=-1, keepdims=True)
    xf = x1 * lax.rsqrt(ms1 + EPS) * gffn_ref[...]
    hi, lo = _split_hi_lo(xf)
    tm = xf.shape[0]
    parts = _dot(jnp.concatenate([hi, lo], axis=0), wr_ref[...])
    lg = (parts[0:tm, 0:LANES] + parts[0:tm, LANES:2 * LANES]
          + parts[tm:2 * tm, 0:LANES] + parts[tm:2 * tm, LANES:2 * LANES]) + br_ref[...]
    lane = lax.broadcasted_iota(jnp.int32, lg.shape, 1)
    lane_f = lane.astype(F32)
    neg = -jnp.inf
    far = float(LANES)
    gl = jnp.where((lane >= N_EXPERTS) & (lane < N_EXPERTS + N_GROUPS), lg, neg)
    gmax = jnp.max(gl, axis=-1, keepdims=True)
    pg_top = 1.0 / jnp.sum(jnp.exp(gl - gmax), axis=-1, keepdims=True)
    gidx = jnp.min(jnp.where(gl == gmax, lane_f, far), axis=-1, keepdims=True) - float(N_EXPERTS)
    in_group = (lane < N_EXPERTS) & ((lane // EXPERTS_PER_GROUP).astype(F32) == gidx)
    ev = jnp.where(in_group, lg, neg)
    v1 = jnp.max(ev, axis=-1, keepdims=True)
    i1 = jnp.min(jnp.where(ev == v1, lane_f, far), axis=-1, keepdims=True)
    ev2 = jnp.where(lane_f == i1, neg, ev)
    v2 = jnp.max(ev2, axis=-1, keepdims=True)
    i2 = jnp.min(jnp.where(ev2 == v2, lane_f, far), axis=-1, keepdims=True)
    e2 = jnp.exp(v2 - v1)
    den = 1.0 + e2
    w1 = (1.0 / den) * pg_top
    w2 = (e2 / den) * pg_top
    gates_ref[...] = jnp.where(lane_f == i1, w1, jnp.where(lane_f == i2, w2,
                                                           jnp.where(lane == GROUP_LANE, gidx, 0.0)))


def _merge_call(x2d, oa, ob, oc, lw):
    t = x2d.shape[0]
    tm = MERGE_TM
    const = lambda shape: pl.BlockSpec(shape, lambda i: (0,) * len(shape), pipeline_mode=pl.Buffered(1))
    row_tile = lambda w: pl.BlockSpec((tm, w), lambda i: (i, 0))
    pair_in = pl.BlockSpec((N_PAIRS, tm, PAIR_W), lambda i: (0, i, 0))
    return pl.pallas_call(
        _merge_kernel,
        grid=(t // tm,),
        in_specs=[row_tile(D_MODEL), pair_in, pair_in, pair_in, const((1, D_MODEL)),
                  const((D_MODEL, 3 * D_MODEL)), const((3, BRANCH_W, D_MODEL)), const((D_MODEL, D_MODEL)),
                  const((1, D_MODEL)), const((D_MODEL, 2 * LANES)), const((1, LANES))],
        out_specs=[row_tile(D_MODEL), row_tile(LANES)],
        out_shape=[jax.ShapeDtypeStruct((t, D_MODEL), F32), jax.ShapeDtypeStruct((t, LANES), F32)],
        compiler_params=pltpu.CompilerParams(dimension_semantics=("arbitrary",),
                                             vmem_limit_bytes=VMEM_LIMIT_BYTES),
        name="merge",
    )(x2d, oa, ob, oc, lw["gmix"], lw["wg"], lw["wb"], lw["wo"], lw["gffn"], lw["wr"], lw["br"])


def _moe_kernel(x_ref, gates_ref, gffn_ref, lstrict_ref, ustrict_ref, wg_ref, wu_ref, wd_ref, o_ref,
                xs_ref, gs_ref, ys_ref):
    x = x_ref[...]
    tm = x.shape[0]
    ms = jnp.mean(x * x, axis=-1, keepdims=True)
    xn = (x * lax.rsqrt(ms + EPS) * gffn_ref[...]).astype(BF16)

    gates = gates_ref[...]
    lane = lax.broadcasted_iota(jnp.int32, gates.shape, 1)
    gid = jnp.sum(jnp.where(lane == GROUP_LANE, gates, 0.0), axis=-1, keepdims=True)
    onehot = jnp.where((lane.astype(F32) == gid) & (lane < N_GROUPS), 1.0, 0.0)
    before = _dot(lstrict_ref[...], onehot.astype(BF16))
    counts = before[tm - 1:tm, :] + onehot[tm - 1:tm, :]
    n_chunks = sum(jnp.where(counts > float(k * MOE_CHUNK), 1.0, 0.0) for k in range(MOE_CHUNKS_MAX))
    seg_start = _dot(jnp.broadcast_to(n_chunks, (SUBLANES, LANES)).astype(BF16), ustrict_ref[...])[0:1, :] * MOE_CHUNK
    rank = jnp.sum(onehot * (before + seg_start), axis=-1, keepdims=True)
    rank_row = jnp.transpose(jnp.broadcast_to(rank, (tm, LANES)))[0:1, :]
    slot_col = lax.broadcasted_iota(jnp.int32, (MOE_ROWS, 1), 0).astype(F32)
    slot_row = lax.broadcasted_iota(jnp.int32, (1, MOE_ROWS), 1).astype(F32)
    perm = jnp.where(slot_col == rank_row, 1.0, 0.0).astype(BF16)
    perm_t = jnp.where(rank == slot_row, 1.0, 0.0).astype(BF16)

    xs_ref[...] = _dot(perm, xn).astype(BF16)
    g_hi, g_lo = _split_hi_lo(gates)
    g_sorted = _dot(perm, jnp.concatenate([g_hi, g_lo], axis=-1))
    gs_ref[...] = g_sorted[:, 0:LANES] + g_sorted[:, LANES:2 * LANES]
    ys_ref[...] = jnp.zeros_like(ys_ref)

    nc = [n_chunks[0, g].astype(jnp.int32) for g in range(N_GROUPS)]
    ends = [nc[0], nc[0] + nc[1], nc[0] + nc[1] + nc[2]]
    total = ends[2] + nc[3]

    def chunk(k, carry):
        g = sum((k >= e).astype(jnp.int32) for e in ends)
        r0 = pl.multiple_of(k * MOE_CHUNK, BF16_ROWS)
        xc = xs_ref[pl.ds(r0, MOE_CHUNK), :]
        gc = gs_ref[pl.ds(r0, MOE_CHUNK), :]
        lane_c = lax.broadcasted_iota(jnp.int32, gc.shape, 1)
        parts = []
        for e in range(EXPERTS_PER_GROUP):
            h_gate = _dot(xc, wg_ref[g * EXPERTS_PER_GROUP + e])
            h_up = _dot(xc, wu_ref[g * EXPERTS_PER_GROUP + e])
            gate = jnp.sum(jnp.where(lane_c == g * EXPERTS_PER_GROUP + e, gc, 0.0), axis=-1, keepdims=True)
            parts.append((jax.nn.silu(h_gate) * h_up * gate).astype(BF16))
        y = sum(_dot(parts[e], wd_ref[g * EXPERTS_PER_GROUP + e]) for e in range(EXPERTS_PER_GROUP))
        ys_ref[pl.ds(r0, MOE_CHUNK), :] = y.astype(BF16)
        return carry

    lax.fori_loop(0, total, chunk, 0)
    o_ref[...] = x + _dot(perm_t, ys_ref[...])


def _moe_call(x1, gates, cw, lw):
    t = x1.shape[0]
    tm = MOE_TM
    const = lambda shape: pl.BlockSpec(shape, lambda i: (0,) * len(shape))
    resident = lambda shape: pl.BlockSpec(shape, lambda i: (0,) * len(shape), pipeline_mode=pl.Buffered(1))
    return pl.pallas_call(
        _moe_kernel,
        grid=(t // tm,),
        in_specs=[pl.BlockSpec((tm, D_MODEL), lambda i: (i, 0)),
                  pl.BlockSpec((tm, LANES), lambda i: (i, 0)),
                  const((1, D_MODEL)), const((tm, tm)), const((LANES, LANES)),
                  resident((N_EXPERTS, D_MODEL, D_FF)), resident((N_EXPERTS, D_MODEL, D_FF)),
                  resident((N_EXPERTS, D_FF, D_MODEL))],
        out_specs=pl.BlockSpec((tm, D_MODEL), lambda i: (i, 0)),
        out_shape=jax.ShapeDtypeStruct((t, D_MODEL), F32),
        scratch_shapes=[pltpu.VMEM((MOE_ROWS, D_MODEL), BF16), pltpu.VMEM((MOE_ROWS, LANES), F32),
                        pltpu.VMEM((MOE_ROWS, D_MODEL), BF16)],
        compiler_params=pltpu.CompilerParams(dimension_semantics=("arbitrary",),
                                             vmem_limit_bytes=VMEM_LIMIT_BYTES),
        name="moe",
    )(x1, gates, lw["gffn"], cw["lstrict"], cw["ustrict"], lw["weg"], lw["weu"], lw["wed"])


def _t5_bucket(dist):
    max_exact = N_BUCKETS // 2
    log_ratio = np.log(np.maximum(dist, max_exact) / max_exact) / np.log(MAX_DISTANCE / max_exact)
    large = np.minimum(max_exact + (log_ratio * (N_BUCKETS - max_exact)).astype(np.int32), N_BUCKETS - 1)
    return np.where(dist < max_exact, dist, large).astype(np.int32)


def _dilated_bias(rel_bias):
    tables = []
    span = 3 * Q_BLOCK
    for window, dil in DILATED_PAIRS:
        assert window // dil == Q_BLOCK
        per_dist = rel_bias[_t5_bucket(np.arange(Q_BLOCK + 1) * dil)].astype(F32).T * LOG2E
        diag = jnp.concatenate([jnp.full((N_HEADS, Q_BLOCK), -jnp.inf, F32), per_dist[:, ::-1],
                                jnp.full((N_HEADS, span - 2 * Q_BLOCK), -jnp.inf, F32)], axis=1)
        skew = jnp.tile(diag, (1, Q_BLOCK))[:, :Q_BLOCK * span].reshape(N_HEADS, Q_BLOCK, span)
        tables.append(skew[:, :, Q_BLOCK:])
    return jnp.stack(tables, axis=0)


def _const_weights(positions, rel_bias):
    b, s = positions.shape
    inv_freq = ROPE_THETA ** (-jnp.arange(ROPE_HALF, dtype=F32) / ROPE_HALF)
    ang = positions.astype(F32).reshape(b * s, 1) * inv_freq
    cos, sin = jnp.cos(ang), jnp.sin(ang)
    idx = np.arange(GROUP_SLAB)
    return {
        "cos": cos, "sin": sin,
        "g64": jnp.asarray(idx[:, None] // HEAD_DIM == idx[None, :] // HEAD_DIM, BF16),
        "g32": jnp.asarray(idx[:, None] // ROPE == idx[None, :] // ROPE, BF16),
        "tri": jnp.asarray(np.arange(FRONT_TM)[None, :] <= np.arange(FRONT_TM)[:, None], BF16),
        "lstrict": jnp.asarray(np.arange(MOE_TM)[None, :] < np.arange(MOE_TM)[:, None], BF16),
        "ustrict": jnp.asarray(np.arange(LANES)[:, None] < np.arange(LANES)[None, :], BF16),
        "bias": _dilated_bias(rel_bias),
    }


def _layer_weights(l, p):
    w_in = p["w_in"][l]
    o_hf = C_CQ
    o_cq = o_hf + N_HEADS
    o_ckv = o_cq + Q_LORA
    o_kr = o_ckv + KV_LORA
    o_g = o_kr + ROPE
    kr1 = w_in[:, o_kr:o_kr + ROPE_HALF]
    kr2 = w_in[:, o_kr + ROPE_HALF:o_kr + ROPE]
    hf_w = w_in[:, o_hf:o_cq]
    w_small = jnp.concatenate([
        w_in[:, o_cq:o_ckv], w_in[:, o_ckv:o_kr], _rope_a(kr1, kr2), _rope_a(kr2, kr1),
        jnp.pad(hf_w, ((0, 0), (0, LANES - N_HEADS))),
    ], axis=1).astype(BF16)

    wuq = p["w_uq"][l].reshape(Q_LORA, N_HEADS, NOPE + ROPE)
    q_nope = wuq[:, :, :NOPE].reshape(Q_LORA, N_HEADS * NOPE)
    q1 = wuq[:, :, NOPE:NOPE + ROPE_HALF]
    q2 = wuq[:, :, NOPE + ROPE_HALF:]

    def pair_rope(v1, v2):
        x = jnp.concatenate([v1, v2], axis=-1).reshape(Q_LORA, N_PAIRS, 2 * ROPE)
        return jnp.pad(x, ((0, 0), (0, 0), (0, LANES - 2 * ROPE))).reshape(Q_LORA, N_PAIRS * LANES)

    wuq_p = jnp.concatenate([q_nope, pair_rope(q1, q2), pair_rope(q2, q1)], axis=1).astype(BF16)
    wukv = p["w_ukv"][l].reshape(KV_LORA, N_HEADS, NOPE + HEAD_DIM)
    wukv_p = jnp.concatenate([wukv[:, :, :NOPE].reshape(KV_LORA, BRANCH_W),
                              wukv[:, :, NOPE:].reshape(KV_LORA, BRANCH_W)], axis=1).astype(BF16)

    tile8 = lambda g: jnp.tile(g, N_HEADS)
    sc_ab = LOG2E / math.sqrt(HEAD_DIM)
    sc_c = LOG2E / math.sqrt(NOPE + ROPE)
    gq_c, gk_c = p["gq_c"][l], p["gk_c"][l]
    zeros512 = jnp.zeros((BRANCH_W,), F32)
    g512 = jnp.stack([tile8(p["gq_a"][l]) * sc_ab, tile8(p["gk_a"][l]),
                      tile8(p["gq_b"][l]) * sc_ab, tile8(p["gk_b"][l]),
                      tile8(gq_c[:NOPE]) * sc_c, tile8(gk_c[:NOPE]), zeros512, zeros512])
    gq1, gq2 = gq_c[NOPE:NOPE + ROPE_HALF] * sc_c, gq_c[NOPE + ROPE_HALF:] * sc_c
    gk1, gk2 = gk_c[NOPE:NOPE + ROPE_HALF], gk_c[NOPE + ROPE_HALF:]
    zeros128 = jnp.zeros((LANES,), F32)
    bf = p["b_forget"][l].astype(F32)
    g128 = jnp.stack([_rope_a(gq1, gq2), _rope_a(gq2, gq1), _rope_a(gk1, gk2), _rope_a(gk2, gk1),
                      jnp.pad(bf, (0, LANES - N_HEADS)), zeros128, zeros128, zeros128])

    wr = jnp.concatenate([p["w_router_expert"][l], p["w_router_group"][l]], axis=1)
    wr = jnp.pad(wr, ((0, 0), (0, LANES - N_EXPERTS - N_GROUPS))).astype(F32)
    wrh = wr.astype(BF16)
    wrl = (wr - wrh.astype(F32)).astype(BF16)
    br = jnp.pad(jnp.concatenate([p["b_router_expert"][l], p["b_router_group"][l]]).astype(F32),
                 (0, LANES - N_EXPERTS - N_GROUPS))[None, :]
    return {
        "gmix": p["norm_mix"][l][None, :], "wab": w_in[:, 0:C_CQ].astype(BF16), "ws": w_small, "wuq": wuq_p, "wukv": wukv_p,
        "g512": g512, "g128": g128, "ncq": p["norm_cq"][l][None, :], "nckv": p["norm_ckv"][l][None, :],
        "wg": w_in[:, o_g:].astype(BF16), "wb": p["w_branch"][l].astype(BF16), "wo": p["w_out"][l].astype(BF16),
        "gffn": p["norm_ffn"][l][None, :], "wr": jnp.concatenate([wrh, wrl], axis=1), "br": br,
        "weg": p["w_expert_gate"][l].astype(BF16), "weu": p["w_expert_up"][l].astype(BF16),
        "wed": p["w_expert_down"][l].astype(BF16),
    }


def kernel(x, positions, rel_bias, norm_mix, w_in, b_forget, gq_a, gk_a, gq_b, gk_b, gq_c, gk_c, norm_cq, norm_ckv, w_uq, w_ukv, w_branch, w_out, norm_ffn, w_router_group, b_router_group, w_router_expert, b_router_expert, w_expert_gate, w_expert_up, w_expert_down):
    batch, seq_len, d_model = x.shape
    assert d_model == D_MODEL and seq_len % (Q_BLOCK * DILATED_PAIRS[-1][1]) == 0
    assert (batch * seq_len) % max(FRONT_TM, MERGE_TM, MOE_TM) == 0 and seq_len % FRONT_TM == 0
    p = dict(norm_mix=norm_mix, w_in=w_in, b_forget=b_forget, gq_a=gq_a, gk_a=gk_a, gq_b=gq_b, gk_b=gk_b,
             gq_c=gq_c, gk_c=gk_c, norm_cq=norm_cq, norm_ckv=norm_ckv, w_uq=w_uq, w_ukv=w_ukv,
             w_branch=w_branch, w_out=w_out, norm_ffn=norm_ffn, w_router_group=w_router_group,
             b_router_group=b_router_group, w_router_expert=w_router_expert, b_router_expert=b_router_expert,
             w_expert_gate=w_expert_gate, w_expert_up=w_expert_up, w_expert_down=w_expert_down)
    cw = _const_weights(positions, rel_bias)
    xs = x.reshape(batch * seq_len, d_model)
    for l in range(norm_mix.shape[0]):
        lw = _layer_weights(l, p)
        qa, ka, va, qb, kb, vb, qc, kc, vc, ccol, crow = _front_call(xs, cw, lw, seq_len)
        oa = _dilated_call(qa, ka, va, cw["bias"], batch, seq_len)
        ob = _flash_call(qb, kb, vb, batch, seq_len, crow, ccol)
        oc = _flash_call(qc, kc, vc, batch, seq_len)
        x1, gates = _merge_call(xs, oa, ob, oc, lw)
        xs = _moe_call(x1, gates, cw, lw)
    return xs.reshape(batch, seq_len, d_model)
```

```python
import functools
import math

import numpy as np
import jax
import jax.numpy as jnp
from jax import lax
from jax.experimental import pallas as pl
from jax.experimental.pallas import tpu as pltpu

F32 = jnp.float32
BF16 = jnp.bfloat16

D_MODEL = 1024
N_HEADS = 8
HEAD_DIM = 64
N_PAIRS = N_HEADS // 2
PAIR_W = 2 * HEAD_DIM
BRANCH_W = N_HEADS * HEAD_DIM
DILATED_PAIRS = ((128, 1), (512, 4), (2048, 16))
Q_LORA = 256
KV_LORA = 128
NOPE = 64
ROPE = 32
ROPE_HALF = ROPE // 2
ROPE_THETA = 10000.0
N_BUCKETS = 32
MAX_DISTANCE = 2048
Q_BLOCK = 128
N_GROUPS = 4
EXPERTS_PER_GROUP = 4
N_EXPERTS = 16
D_FF = 256
EPS = 1e-6
LOG2E = math.log2(math.e)
LANES = 128
SUBLANES = 8

VMEM_LIMIT_BYTES = 56 * 1024 * 1024

FRONT_TM = 512
MERGE_TM = 1024
MOE_TM = 512
MOE_CHUNK = 144
MOE_CHUNKS_MAX = -(-MOE_TM // MOE_CHUNK)
MOE_SLOTS = MOE_CHUNKS_MAX + N_GROUPS - 1
MOE_ROWS = -(-MOE_SLOTS * MOE_CHUNK // LANES) * LANES
BF16_ROWS = 16
GROUP_LANE = N_EXPERTS
FLASH_T = 256

C_A = 0
C_B = 3 * BRANCH_W
C_CQ = 6 * BRANCH_W
C_CKV = C_CQ + Q_LORA
C_KRA = C_CKV + KV_LORA
C_KRB = C_KRA + LANES
C_HF = C_KRB + LANES
N_FRONT = C_HF + LANES
HF_ROWS = 16
GROUP_SLAB = 256


def _nt_dot(a, b):
    return lax.dot_general(a, b, (((1,), (1,)), ((), ())), preferred_element_type=F32)


def _dot(a, b):
    return jnp.dot(a, b, preferred_element_type=F32)


def _log_sigmoid(x):
    return jnp.minimum(x, 0.0) - jnp.log1p(jnp.exp(-jnp.abs(x)))


def _split_hi_lo(x):
    hi = x.astype(BF16)
    lo = (x - hi.astype(F32)).astype(BF16)
    return hi, lo


def _rope_a(v1, v2):
    pad = jnp.zeros(v1.shape[:-1] + (LANES - 4 * ROPE_HALF,), v1.dtype)
    return jnp.concatenate([v1, v2, v1, v2, pad], axis=-1)


def _front_kernel(x_ref, gmix_ref, wab_ref, ws_ref, wuq_ref, wukv_ref, g64_ref, g32_ref, tri_ref,
                  g512_ref, g128_ref, ncq_ref, nckv_ref, cos_ref, sin_ref,
                  qa_ref, ka_ref, va_ref, qb_ref, kb_ref, vb_ref, qc_ref, kc_ref, vc_ref,
                  ccol_ref, crow_ref, carry_col, *, tiles_per_seq):
    i = pl.program_id(0)

    @pl.when(i % tiles_per_seq == 0)
    def _():
        carry_col[...] = jnp.zeros_like(carry_col)

    x = x_ref[...]
    ms = jnp.mean(x * x, axis=-1, keepdims=True)
    xn = (x * lax.rsqrt(ms + EPS) * gmix_ref[...]).astype(BF16)

    def group_sums(h, gmat):
        sq = (h * h).astype(BF16)
        slab = gmat.shape[0]
        return jnp.concatenate([_dot(sq[:, i:i + slab], gmat) for i in range(0, h.shape[1], slab)], axis=-1)

    def group_norm(h, gmat, n, gain):
        return h * lax.rsqrt(group_sums(h, gmat) * (1.0 / n) + EPS) * gain

    def store_pairs(ref, val, lo=0, w=PAIR_W):
        for p in range(N_PAIRS):
            ref[p, :, lo:lo + w] = val[:, p * PAIR_W:(p + 1) * PAIR_W].astype(BF16)

    g64 = g64_ref[...]
    for base, q_ref, k_ref, v_ref, row in ((C_A, qa_ref, ka_ref, va_ref, 0), (C_B, qb_ref, kb_ref, vb_ref, 2)):
        hq = _dot(xn, wab_ref[:, base:base + BRANCH_W])
        store_pairs(q_ref, group_norm(hq, g64, HEAD_DIM, g512_ref[row:row + 1, :]))
        hk = _dot(xn, wab_ref[:, base + BRANCH_W:base + 2 * BRANCH_W])
        store_pairs(k_ref, group_norm(hk, g64, HEAD_DIM, g512_ref[row + 1:row + 2, :]))
        hv = _dot(xn, wab_ref[:, base + 2 * BRANCH_W:base + 3 * BRANCH_W])
        store_pairs(v_ref, hv)

    hs = _dot(xn, ws_ref[...])
    cos, sin = cos_ref[...], sin_ref[...]
    ca = _rope_a(cos, cos)
    sb = _rope_a(-sin, sin)

    hcq = hs[:, 0:Q_LORA]
    cq = (hcq * lax.rsqrt(jnp.mean(hcq * hcq, axis=-1, keepdims=True) + EPS) * ncq_ref[...]).astype(BF16)
    qc = _dot(cq, wuq_ref[...])
    qn = group_norm(qc[:, 0:BRANCH_W], g64, NOPE, g512_ref[4:5, :])
    store_pairs(qc_ref, qn, 0)
    qra = qc[:, BRANCH_W:2 * BRANCH_W]
    qrb = qc[:, 2 * BRANCH_W:3 * BRANCH_W]
    rs = lax.rsqrt(group_sums(qra, g32_ref[...]) * (1.0 / ROPE) + EPS)
    ga = g128_ref[0:1, :]
    gb = g128_ref[1:2, :]
    for p in range(N_PAIRS):
        sl = slice(p * PAIR_W, (p + 1) * PAIR_W)
        qr = (qra[:, sl] * ga * ca + qrb[:, sl] * gb * sb) * rs[:, sl]
        qc_ref[p, :, PAIR_W:2 * PAIR_W] = qr.astype(BF16)

    hckv = hs[:, Q_LORA:Q_LORA + KV_LORA]
    ckv = (hckv * lax.rsqrt(jnp.mean(hckv * hckv, axis=-1, keepdims=True) + EPS) * nckv_ref[...]).astype(BF16)
    kv = _dot(ckv, wukv_ref[...])
    store_pairs(kc_ref, group_norm(kv[:, 0:BRANCH_W], g64, NOPE, g512_ref[5:6, :]), 0)
    store_pairs(vc_ref, kv[:, BRANCH_W:2 * BRANCH_W])
    kra = hs[:, C_KRA - C_CQ:C_KRA - C_CQ + LANES]
    krb = hs[:, C_KRB - C_CQ:C_KRB - C_CQ + LANES]
    rsk = lax.rsqrt(_dot((kra * kra).astype(BF16), g32_ref[0:LANES, 0:LANES]) * (1.0 / ROPE) + EPS)
    kr = ((kra * g128_ref[2:3, :] * ca + krb * g128_ref[3:4, :] * sb) * rsk).astype(BF16)
    for p in range(N_PAIRS):
        kc_ref[p, :, PAIR_W:2 * PAIR_W] = kr

    tri = tri_ref[...]
    hf_col = hs[:, C_HF - C_CQ:C_HF - C_CQ + LANES] + g128_ref[4:5, :]
    hi, lo = _split_hi_lo(_log_sigmoid(hf_col))
    ccol = _dot(tri, hi) + _dot(tri, lo) + carry_col[0:1, :]
    ccol_ref[...] = ccol
    tm = ccol.shape[0]
    carry_col[...] = jnp.broadcast_to(ccol[tm - 1:tm, :], carry_col.shape)

    crow_ref[...] = jnp.transpose(ccol)[0:HF_ROWS, :]


def _front_call(x2d, cw, lw, seq_len):
    t = x2d.shape[0]
    tm = FRONT_TM
    const = lambda shape: pl.BlockSpec(shape, lambda i: (0,) * len(shape))
    pair_out = lambda w: pl.BlockSpec((N_PAIRS, tm, w), lambda i: (0, i, 0))
    pair_shape = lambda w: jax.ShapeDtypeStruct((N_PAIRS, t, w), BF16)
    in_specs = [
        pl.BlockSpec((tm, D_MODEL), lambda i: (i, 0)),
        const((1, D_MODEL)),
        const((D_MODEL, C_CQ)),
        const((D_MODEL, N_FRONT - C_CQ)),
        const((Q_LORA, 3 * BRANCH_W)),
        const((KV_LORA, 2 * BRANCH_W)),
        const((GROUP_SLAB, GROUP_SLAB)),
        const((GROUP_SLAB, GROUP_SLAB)),
        const((tm, tm)),
        const((SUBLANES, BRANCH_W)),
        const((SUBLANES, LANES)),
        const((1, Q_LORA)),
        const((1, KV_LORA)),
        pl.BlockSpec((tm, ROPE_HALF), lambda i: (i, 0)),
        pl.BlockSpec((tm, ROPE_HALF), lambda i: (i, 0)),
    ]
    out_specs = [pair_out(PAIR_W)] * 6 + [pair_out(2 * PAIR_W), pair_out(2 * PAIR_W), pair_out(PAIR_W),
                                           pl.BlockSpec((tm, LANES), lambda i: (i, 0)),
                                           pl.BlockSpec((HF_ROWS, tm), lambda i: (0, i))]
    out_shape = [pair_shape(PAIR_W)] * 6 + [pair_shape(2 * PAIR_W), pair_shape(2 * PAIR_W), pair_shape(PAIR_W),
                                            jax.ShapeDtypeStruct((t, LANES), F32),
                                            jax.ShapeDtypeStruct((HF_ROWS, t), F32)]
    return pl.pallas_call(
        functools.partial(_front_kernel, tiles_per_seq=seq_len // tm),
        grid=(t // tm,),
        in_specs=in_specs,
        out_specs=out_specs,
        out_shape=out_shape,
        scratch_shapes=[pltpu.VMEM((SUBLANES, LANES), F32)],
        compiler_params=pltpu.CompilerParams(dimension_semantics=("arbitrary",),
                                             vmem_limit_bytes=VMEM_LIMIT_BYTES),
        name="front",
    )(x2d, lw["gmix"], lw["wab"], lw["ws"], lw["wuq"], lw["wukv"], cw["g64"], cw["g32"], cw["tri"],
      lw["g512"], lw["g128"], lw["ncq"], lw["nckv"], cw["cos"], cw["sin"])


def _dilated_kernel(q_ref, k_ref, v_ref, bias_ref, o_ref, oacc, lacc, s_scr, nat, res4, g4, g16, vx, *, seq_len):
    lane = lax.broadcasted_iota(jnp.int32, (1, PAIR_W), 1)
    left = lane < HEAD_DIM
    d_mid, d_far = DILATED_PAIRS[1][1], DILATED_PAIRS[2][1]
    assert DILATED_PAIRS[0][1] == 1 and d_far == d_mid * d_mid
    n_mid, n_far = seq_len // d_mid, seq_len // d_far

    vx[:, :, PAIR_W:2 * PAIR_W] = jnp.ones((len(DILATED_PAIRS), seq_len, PAIR_W), BF16)
    vx[0, :, 0:PAIR_W] = v_ref[...]
    for ti, src in enumerate((q_ref, k_ref, v_ref)):
        nat[ti] = src[...].astype(F32)

    def put(ti, pattern, lo, hi, rows):
        if ti == 2:
            vx[pattern, lo:hi, 0:PAIR_W] = rows.astype(BF16)
        else:
            (g4, g16)[pattern - 1][ti, lo:hi, :] = rows.astype(BF16)

    for ti in range(3):
        for c in range(d_mid):
            rows = nat[ti, pl.ds(c, n_mid, stride=d_mid), :]
            res4[ti, c * n_mid:(c + 1) * n_mid, :] = rows
            put(ti, 1, c * n_mid, (c + 1) * n_mid, rows)
    for ti in range(3):
        for c in range(d_far):
            rows = res4[ti, pl.ds((c % d_mid) * n_mid + c // d_mid, n_far, stride=d_mid), :]
            put(ti, 2, c * n_far, (c + 1) * n_far, rows)

    readers = (
        (1, lambda ti, lo, hi: (q_ref, k_ref)[ti][lo:hi, :]),
        (d_mid, lambda ti, lo, hi: g4[ti, lo:hi, :]),
        (d_far, lambda ti, lo, hi: g16[ti, lo:hi, :]),
    )
    for di, (d, read) in enumerate(readers):
        n_sub = seq_len // d
        n_blocks = n_sub // Q_BLOCK
        blocks = [(c, nb) for c in range(d) for nb in range(n_blocks)]

        def window(c, nb):
            k_lo = max(nb - 1, 0) * Q_BLOCK
            return c * n_sub + k_lo, c * n_sub + (nb + 1) * Q_BLOCK

        for bi, (c, nb) in enumerate(blocks):
            q = read(0, c * n_sub + nb * Q_BLOCK, c * n_sub + (nb + 1) * Q_BLOCK)
            kwin = read(1, *window(c, nb))
            w = kwin.shape[0]
            for j in range(2):
                qh = jnp.where(left if j == 0 else jnp.logical_not(left), q, jnp.zeros_like(q))
                s_scr[j, bi * Q_BLOCK:(bi + 1) * Q_BLOCK, 0:w] = (
                    _nt_dot(qh, kwin) + bias_ref[di, j, :, 2 * Q_BLOCK - w:])
        for bi, (c, nb) in enumerate(blocks):
            lo, hi = window(c, nb)
            w = hi - lo
            outs, lses = [], []
            for j in range(2):
                s = s_scr[j, bi * Q_BLOCK:(bi + 1) * Q_BLOCK, 0:w]
                m = jnp.max(s, axis=-1, keepdims=True)
                p = jnp.exp2(s - m).astype(BF16)
                res = _dot(p, vx[di, lo:hi, :])
                den = res[:, PAIR_W:2 * PAIR_W]
                outs.append(res[:, 0:PAIR_W] / den)
                lses.append(m + jnp.log2(den))
            start = nb * Q_BLOCK * d + c
            rows = pl.ds(start, Q_BLOCK) if d == 1 else pl.ds(start, Q_BLOCK, stride=d)
            oacc[di, rows, :] = jnp.where(left, outs[0], outs[1])
            lacc[di, rows, :] = jnp.where(left, lses[0], lses[1])

    l0, l1, l2 = lacc[0], lacc[1], lacc[2]
    m = jnp.maximum(jnp.maximum(l0, l1), l2)
    w0, w1, w2 = jnp.exp2(l0 - m), jnp.exp2(l1 - m), jnp.exp2(l2 - m)
    wsum = w0 + w1 + w2
    o = oacc[0] * (w0 / wsum) + oacc[1] * (w1 / wsum) + oacc[2] * (w2 / wsum)
    o_ref[...] = o.astype(BF16)


def _dilated_call(q, k, v, bias, batch, seq_len):
    n_pat = len(DILATED_PAIRS)
    seq_block = pl.BlockSpec((None, None, seq_len, PAIR_W), lambda p, b: (p, b, 0, 0))
    view = lambda a: a.reshape(N_PAIRS, batch, seq_len, PAIR_W)
    seq_scratch = lambda n, dt: pltpu.VMEM((n, seq_len, PAIR_W), dt)
    out = pl.pallas_call(
        functools.partial(_dilated_kernel, seq_len=seq_len),
        grid=(N_PAIRS, batch),
        in_specs=[seq_block, seq_block, seq_block,
                  pl.BlockSpec((n_pat, 2, Q_BLOCK, 2 * Q_BLOCK), lambda p, b: (0, p, 0, 0))],
        out_specs=seq_block,
        out_shape=jax.ShapeDtypeStruct((N_PAIRS, batch, seq_len, PAIR_W), BF16),
        scratch_shapes=[seq_scratch(n_pat, F32), seq_scratch(n_pat, F32),
                        pltpu.VMEM((2, seq_len, 2 * Q_BLOCK), F32),
                        seq_scratch(3, F32), seq_scratch(3, F32), seq_scratch(2, BF16), seq_scratch(2, BF16),
                        pltpu.VMEM((n_pat, seq_len, 2 * PAIR_W), BF16)],
        compiler_params=pltpu.CompilerParams(dimension_semantics=("arbitrary", "arbitrary"),
                                             vmem_limit_bytes=VMEM_LIMIT_BYTES),
        name="dilated",
    )(view(q), view(k), view(v), bias)
    return out.reshape(N_PAIRS, batch * seq_len, PAIR_W)


def _flash_kernel(*refs, kw, decay, seq_len):
    if decay:
        q_ref, k_ref, v_ref, crow_ref, ccol_ref, o_ref, s_scr, p_scr, v_scr = refs
    else:
        q_ref, k_ref, v_ref, o_ref, s_scr, p_scr, v_scr = refs
    hp = pl.program_id(0)
    t = FLASH_T
    lane = lax.broadcasted_iota(jnp.int32, (1, kw), 1)
    in_h0 = (lane < HEAD_DIM) | ((lane >= PAIR_W) & (lane < PAIR_W + ROPE))
    in_h1 = ((lane >= HEAD_DIM) & (lane < PAIR_W)) | ((lane >= PAIR_W + ROPE) & (lane < PAIR_W + 2 * ROPE))
    row = lax.broadcasted_iota(jnp.int32, (t, t), 0)
    col = lax.broadcasted_iota(jnp.int32, (t, t), 1)
    causal = col <= row
    out_lane = lax.broadcasted_iota(jnp.int32, (1, PAIR_W), 1)
    v_scr[:, 0:PAIR_W] = v_ref[...]
    v_scr[:, PAIR_W:2 * PAIR_W] = jnp.ones((seq_len, PAIR_W), BF16)
    n_tiles = seq_len // t
    order = list(range(n_tiles))[::-1]
    for step, qi in enumerate(order):
        rows = slice(qi * t, (qi + 1) * t)
        q = q_ref[rows, :]
        width = (qi + 1) * t
        buf = step % 2
        q2 = jnp.concatenate([jnp.where(sel, q, jnp.zeros_like(q)) for sel in (in_h0, in_h1)], axis=0)
        mx = [None, None]
        for c in range(qi + 1):
            cols = slice(c * t, (c + 1) * t)
            s2 = _nt_dot(q2, k_ref[cols, :])
            for j in range(2):
                s = s2[j * t:(j + 1) * t, :]
                if decay:
                    s = s - crow_ref[pl.ds(2 * hp + j, 1), cols] * LOG2E
                if c == qi:
                    s = jnp.where(causal, s, -jnp.inf)
                s_scr[buf, j * t:(j + 1) * t, cols] = s
                half = jnp.maximum(s[:, 0:t // 2], s[:, t // 2:t])
                mx[j] = half if mx[j] is None else jnp.maximum(mx[j], half)
        for j in range(2):
            shift = jnp.max(mx[j], axis=-1, keepdims=True)
            if decay:
                head = (2 * hp + j).astype(F32)
                cc = ccol_ref[rows, :]
                lane_c = lax.broadcasted_iota(jnp.int32, cc.shape, 1).astype(F32)
                cq = jnp.sum(jnp.where(lane_c == head, cc, 0.0), axis=-1, keepdims=True) * LOG2E
                shift = (shift + cq) - cq
            for c in range(qi + 1):
                cols = slice(c * t, (c + 1) * t)
                hrows = slice(j * t, (j + 1) * t)
                p_scr[buf, hrows, cols] = jnp.exp2(s_scr[buf, hrows, cols] - shift).astype(BF16)
        res = _dot(p_scr[buf, :, 0:width], v_scr[0:width, :])
        out = res[:, 0:PAIR_W] / res[:, PAIR_W:2 * PAIR_W]
        o_ref[rows, :] = jnp.where(out_lane < HEAD_DIM, out[0:t, :], out[t:2 * t, :]).astype(BF16)


def _flash_call(q, k, v, batch, seq_len, crow=None, ccol=None):
    kw = q.shape[-1]
    decay = crow is not None
    view = lambda a: a.reshape(N_PAIRS, batch, seq_len, a.shape[-1])
    args = [view(q), view(k), view(v)]
    seq_block = lambda w: pl.BlockSpec((None, None, seq_len, w), lambda p, b: (p, b, 0, 0))
    in_specs = [seq_block(kw), seq_block(kw), seq_block(PAIR_W)]
    if decay:
        args += [crow, ccol]
        in_specs += [pl.BlockSpec((HF_ROWS, seq_len), lambda p, b: (0, b)),
                     pl.BlockSpec((seq_len, LANES), lambda p, b: (b, 0))]
    out = pl.pallas_call(
        functools.partial(_flash_kernel, kw=kw, decay=decay, seq_len=seq_len),
        grid=(N_PAIRS, batch),
        in_specs=in_specs,
        out_specs=seq_block(PAIR_W),
        out_shape=jax.ShapeDtypeStruct((N_PAIRS, batch, seq_len, PAIR_W), BF16),
        scratch_shapes=[pltpu.VMEM((2, 2 * FLASH_T, seq_len), F32), pltpu.VMEM((2, 2 * FLASH_T, seq_len), BF16),
                        pltpu.VMEM((seq_len, 2 * PAIR_W), BF16)],
        compiler_params=pltpu.CompilerParams(dimension_semantics=("arbitrary",) * 2,
                                             vmem_limit_bytes=VMEM_LIMIT_BYTES),
        name="flash_fox" if decay else "flash_mla",
    )(*args)
    return out.reshape(N_PAIRS, batch * seq_len, PAIR_W)


def _merge_kernel(x_ref, oa_ref, ob_ref, oc_ref, gmix_ref, wg_ref, wb_ref, wo_ref, gffn_ref,
                  wr_ref, br_ref, x1_ref, gates_ref):
    x = x_ref[...]
    ms = jnp.mean(x * x, axis=-1, keepdims=True)
    xn = (x * lax.rsqrt(ms + EPS) * gmix_ref[...]).astype(BF16)
    mixed = None
    for g, o_ref in enumerate((oa_ref, ob_ref, oc_ref)):
        o = jnp.concatenate([o_ref[p] for p in range(N_PAIRS)], axis=-1)
        gate = jax.nn.sigmoid(_dot(xn, wg_ref[:, g * D_MODEL:(g + 1) * D_MODEL]))
        term = gate * _dot(o, wb_ref[g])
        mixed = term if mixed is None else mixed + term
    x1 = x + _dot(mixed.astype(BF16), wo_ref[...])
    x1_ref[...] = x1

    ms1 = jnp.mean(x1 * x1, axis=-1, keepdims=True)
    xf = x1 * lax.rsqrt(ms1 + EPS) * gffn_ref[...]
    hi, lo = _split_hi_lo(xf)
    tm = xf.shape[0]
    parts = _dot(jnp.concatenate([hi, lo], axis=0), wr_ref[...])
    lg = (parts[0:tm, 0:LANES] + parts[0:tm, LANES:2 * LANES]
          + parts[tm:2 * tm, 0:LANES] + parts[tm:2 * tm, LANES:2 * LANES]) + br_ref[...]
    lane = lax.broadcasted_iota(jnp.int32, lg.shape, 1)
    lane_f = lane.astype(F32)
    neg = -jnp.inf
    far = float(LANES)
    gl = jnp.where((lane >= N_EXPERTS) & (lane < N_EXPERTS + N_GROUPS), lg, neg)
    gmax = jnp.max(gl, axis=-1, keepdims=True)
    pg_top = 1.0 / jnp.sum(jnp.exp(gl - gmax), axis=-1, keepdims=True)
    gidx = jnp.min(jnp.where(gl == gmax, lane_f, far), axis=-1, keepdims=True) - float(N_EXPERTS)
    in_group = (lane < N_EXPERTS) & ((lane // EXPERTS_PER_GROUP).astype(F32) == gidx)
    ev = jnp.where(in_group, lg, neg)
    v1 = jnp.max(ev, axis=-1, keepdims=True)
    i1 = jnp.min(jnp.where(ev == v1, lane_f, far), axis=-1, keepdims=True)
    ev2 = jnp.where(lane_f == i1, neg, ev)
    v2 = jnp.max(ev2, axis=-1, keepdims=True)
    i2 = jnp.min(jnp.where(ev2 == v2, lane_f, far), axis=-1, keepdims=True)
    e2 = jnp.exp(v2 - v1)
    den = 1.0 + e2
    w1 = (1.0 / den) * pg_top
    w2 = (e2 / den) * pg_top
    gates_ref[...] = jnp.where(lane_f == i1, w1, jnp.where(lane_f == i2, w2,
                                                           jnp.where(lane == GROUP_LANE, gidx, 0.0)))


def _merge_call(x2d, oa, ob, oc, lw):
    t = x2d.shape[0]
    tm = MERGE_TM
    const = lambda shape: pl.BlockSpec(shape, lambda i: (0,) * len(shape), pipeline_mode=pl.Buffered(1))
    row_tile = lambda w: pl.BlockSpec((tm, w), lambda i: (i, 0))
    pair_in = pl.BlockSpec((N_PAIRS, tm, PAIR_W), lambda i: (0, i, 0))
    return pl.pallas_call(
        _merge_kernel,
        grid=(t // tm,),
        in_specs=[row_tile(D_MODEL), pair_in, pair_in, pair_in, const((1, D_MODEL)),
                  const((D_MODEL, 3 * D_MODEL)), const((3, BRANCH_W, D_MODEL)), const((D_MODEL, D_MODEL)),
                  const((1, D_MODEL)), const((D_MODEL, 2 * LANES)), const((1, LANES))],
        out_specs=[row_tile(D_MODEL), row_tile(LANES)],
        out_shape=[jax.ShapeDtypeStruct((t, D_MODEL), F32), jax.ShapeDtypeStruct((t, LANES), F32)],
        compiler_params=pltpu.CompilerParams(dimension_semantics=("arbitrary",),
                                             vmem_limit_bytes=VMEM_LIMIT_BYTES),
        name="merge",
    )(x2d, oa, ob, oc, lw["gmix"], lw["wg"], lw["wb"], lw["wo"], lw["gffn"], lw["wr"], lw["br"])


def _moe_kernel(x_ref, gates_ref, gffn_ref, lstrict_ref, ustrict_ref, wg_ref, wu_ref, wd_ref, o_ref,
                xs_ref, gs_ref, ys_ref):
    x = x_ref[...]
    tm = x.shape[0]
    ms = jnp.mean(x * x, axis=-1, keepdims=True)
    xn = (x * lax.rsqrt(ms + EPS) * gffn_ref[...]).astype(BF16)

    gates = gates_ref[...]
    lane = lax.broadcasted_iota(jnp.int32, gates.shape, 1)
    gid = jnp.sum(jnp.where(lane == GROUP_LANE, gates, 0.0), axis=-1, keepdims=True)
    onehot = jnp.where((lane.astype(F32) == gid) & (lane < N_GROUPS), 1.0, 0.0)
    before = _dot(lstrict_ref[...], onehot.astype(BF16))
    counts = before[tm - 1:tm, :] + onehot[tm - 1:tm, :]
    n_chunks = sum(jnp.where(counts > float(k * MOE_CHUNK), 1.0, 0.0) for k in range(MOE_CHUNKS_MAX))
    seg_start = _dot(jnp.broadcast_to(n_chunks, (SUBLANES, LANES)).astype(BF16), ustrict_ref[...])[0:1, :] * MOE_CHUNK
    rank = jnp.sum(onehot * (before + seg_start), axis=-1, keepdims=True)
    rank_row = jnp.transpose(jnp.broadcast_to(rank, (tm, LANES)))[0:1, :]
    slot_col = lax.broadcasted_iota(jnp.int32, (MOE_ROWS, 1), 0).astype(F32)
    slot_row = lax.broadcasted_iota(jnp.int32, (1, MOE_ROWS), 1).astype(F32)
    perm = jnp.where(slot_col == rank_row, 1.0, 0.0).astype(BF16)
    perm_t = jnp.where(rank == slot_row, 1.0, 0.0).astype(BF16)

    xs_ref[...] = _dot(perm, xn).astype(BF16)
    g_hi, g_lo = _split_hi_lo(gates)
    g_sorted = _dot(perm, jnp.concatenate([g_hi, g_lo], axis=-1))
    gs_ref[...] = g_sorted[:, 0:LANES] + g_sorted[:, LANES:2 * LANES]
    ys_ref[...] = jnp.zeros_like(ys_ref)

    nc = [n_chunks[0, g].astype(jnp.int32) for g in range(N_GROUPS)]
    ends = [nc[0], nc[0] + nc[1], nc[0] + nc[1] + nc[2]]
    total = ends[2] + nc[3]

    def chunk(k, carry):
        g = sum((k >= e).astype(jnp.int32) for e in ends)
        r0 = pl.multiple_of(k * MOE_CHUNK, BF16_ROWS)
        xc = xs_ref[pl.ds(r0, MOE_CHUNK), :]
        gc = gs_ref[pl.ds(r0, MOE_CHUNK), :]
        lane_c = lax.broadcasted_iota(jnp.int32, gc.shape, 1)
        parts = []
        for e in range(EXPERTS_PER_GROUP):
            h_gate = _dot(xc, wg_ref[g * EXPERTS_PER_GROUP + e])
            h_up = _dot(xc, wu_ref[g * EXPERTS_PER_GROUP + e])
            gate = jnp.sum(jnp.where(lane_c == g * EXPERTS_PER_GROUP + e, gc, 0.0), axis=-1, keepdims=True)
            parts.append((jax.nn.silu(h_gate) * h_up * gate).astype(BF16))
        y = sum(_dot(parts[e], wd_ref[g * EXPERTS_PER_GROUP + e]) for e in range(EXPERTS_PER_GROUP))
        ys_ref[pl.ds(r0, MOE_CHUNK), :] = y.astype(BF16)
        return carry

    lax.fori_loop(0, total, chunk, 0)
    o_ref[...] = x + _dot(perm_t, ys_ref[...])


def _moe_call(x1, gates, cw, lw):
    t = x1.shape[0]
    tm = MOE_TM
    const = lambda shape: pl.BlockSpec(shape, lambda i: (0,) * len(shape))
    resident = lambda shape: pl.BlockSpec(shape, lambda i: (0,) * len(shape), pipeline_mode=pl.Buffered(1))
    return pl.pallas_call(
        _moe_kernel,
        grid=(t // tm,),
        in_specs=[pl.BlockSpec((tm, D_MODEL), lambda i: (i, 0)),
                  pl.BlockSpec((tm, LANES), lambda i: (i, 0)),
                  const((1, D_MODEL)), const((tm, tm)), const((LANES, LANES)),
                  resident((N_EXPERTS, D_MODEL, D_FF)), resident((N_EXPERTS, D_MODEL, D_FF)),
                  resident((N_EXPERTS, D_FF, D_MODEL))],
        out_specs=pl.BlockSpec((tm, D_MODEL), lambda i: (i, 0)),
        out_shape=jax.ShapeDtypeStruct((t, D_MODEL), F32),
        scratch_shapes=[pltpu.VMEM((MOE_ROWS, D_MODEL), BF16), pltpu.VMEM((MOE_ROWS, LANES), F32),
                        pltpu.VMEM((MOE_ROWS, D_MODEL), BF16)],
        compiler_params=pltpu.CompilerParams(dimension_semantics=("arbitrary",),
                                             vmem_limit_bytes=VMEM_LIMIT_BYTES),
        name="moe",
    )(x1, gates, lw["gffn"], cw["lstrict"], cw["ustrict"], lw["weg"], lw["weu"], lw["wed"])


def _t5_bucket(dist):
    max_exact = N_BUCKETS // 2
    log_ratio = np.log(np.maximum(dist, max_exact) / max_exact) / np.log(MAX_DISTANCE / max_exact)
    large = np.minimum(max_exact + (log_ratio * (N_BUCKETS - max_exact)).astype(np.int32), N_BUCKETS - 1)
    return np.where(dist < max_exact, dist, large).astype(np.int32)


def _dilated_bias(rel_bias):
    tables = []
    span = 3 * Q_BLOCK
    for window, dil in DILATED_PAIRS:
        assert window // dil == Q_BLOCK
        per_dist = rel_bias[_t5_bucket(np.arange(Q_BLOCK + 1) * dil)].astype(F32).T * LOG2E
        diag = jnp.concatenate([jnp.full((N_HEADS, Q_BLOCK), -jnp.inf, F32), per_dist[:, ::-1],
                                jnp.full((N_HEADS, span - 2 * Q_BLOCK), -jnp.inf, F32)], axis=1)
        skew = jnp.tile(diag, (1, Q_BLOCK))[:, :Q_BLOCK * span].reshape(N_HEADS, Q_BLOCK, span)
        tables.append(skew[:, :, Q_BLOCK:])
    return jnp.stack(tables, axis=0)


def _const_weights(positions, rel_bias):
    b, s = positions.shape
    inv_freq = ROPE_THETA ** (-jnp.arange(ROPE_HALF, dtype=F32) / ROPE_HALF)
    ang = positions.astype(F32).reshape(b * s, 1) * inv_freq
    cos, sin = jnp.cos(ang), jnp.sin(ang)
    idx = np.arange(GROUP_SLAB)
    return {
        "cos": cos, "sin": sin,
        "g64": jnp.asarray(idx[:, None] // HEAD_DIM == idx[None, :] // HEAD_DIM, BF16),
        "g32": jnp.asarray(idx[:, None] // ROPE == idx[None, :] // ROPE, BF16),
        "tri": jnp.asarray(np.arange(FRONT_TM)[None, :] <= np.arange(FRONT_TM)[:, None], BF16),
        "lstrict": jnp.asarray(np.arange(MOE_TM)[None, :] < np.arange(MOE_TM)[:, None], BF16),
        "ustrict": jnp.asarray(np.arange(LANES)[:, None] < np.arange(LANES)[None, :], BF16),
        "bias": _dilated_bias(rel_bias),
    }


def _layer_weights(l, p):
    w_in = p["w_in"][l]
    o_hf = C_CQ
    o_cq = o_hf + N_HEADS
    o_ckv = o_cq + Q_LORA
    o_kr = o_ckv + KV_LORA
    o_g = o_kr + ROPE
    kr1 = w_in[:, o_kr:o_kr + ROPE_HALF]
    kr2 = w_in[:, o_kr + ROPE_HALF:o_kr + ROPE]
    hf_w = w_in[:, o_hf:o_cq]
    w_small = jnp.concatenate([
        w_in[:, o_cq:o_ckv], w_in[:, o_ckv:o_kr], _rope_a(kr1, kr2), _rope_a(kr2, kr1),
        jnp.pad(hf_w, ((0, 0), (0, LANES - N_HEADS))),
    ], axis=1).astype(BF16)

    wuq = p["w_uq"][l].reshape(Q_LORA, N_HEADS, NOPE + ROPE)
    q_nope = wuq[:, :, :NOPE].reshape(Q_LORA, N_HEADS * NOPE)
    q1 = wuq[:, :, NOPE:NOPE + ROPE_HALF]
    q2 = wuq[:, :, NOPE + ROPE_HALF:]

    def pair_rope(v1, v2):
        x = jnp.concatenate([v1, v2], axis=-1).reshape(Q_LORA, N_PAIRS, 2 * ROPE)
        return jnp.pad(x, ((0, 0), (0, 0), (0, LANES - 2 * ROPE))).reshape(Q_LORA, N_PAIRS * LANES)

    wuq_p = jnp.concatenate([q_nope, pair_rope(q1, q2), pair_rope(q2, q1)], axis=1).astype(BF16)
    wukv = p["w_ukv"][l].reshape(KV_LORA, N_HEADS, NOPE + HEAD_DIM)
    wukv_p = jnp.concatenate([wukv[:, :, :NOPE].reshape(KV_LORA, BRANCH_W),
                              wukv[:, :, NOPE:].reshape(KV_LORA, BRANCH_W)], axis=1).astype(BF16)

    tile8 = lambda g: jnp.tile(g, N_HEADS)
    sc_ab = LOG2E / math.sqrt(HEAD_DIM)
    sc_c = LOG2E / math.sqrt(NOPE + ROPE)
    gq_c, gk_c = p["gq_c"][l], p["gk_c"][l]
    zeros512 = jnp.zeros((BRANCH_W,), F32)
    g512 = jnp.stack([tile8(p["gq_a"][l]) * sc_ab, tile8(p["gk_a"][l]),
                      tile8(p["gq_b"][l]) * sc_ab, tile8(p["gk_b"][l]),
                      tile8(gq_c[:NOPE]) * sc_c, tile8(gk_c[:NOPE]), zeros512, zeros512])
    gq1, gq2 = gq_c[NOPE:NOPE + ROPE_HALF] * sc_c, gq_c[NOPE + ROPE_HALF:] * sc_c
    gk1, gk2 = gk_c[NOPE:NOPE + ROPE_HALF], gk_c[NOPE + ROPE_HALF:]
    zeros128 = jnp.zeros((LANES,), F32)
    bf = p["b_forget"][l].astype(F32)
    g128 = jnp.stack([_rope_a(gq1, gq2), _rope_a(gq2, gq1), _rope_a(gk1, gk2), _rope_a(gk2, gk1),
                      jnp.pad(bf, (0, LANES - N_HEADS)), zeros128, zeros128, zeros128])

    wr = jnp.concatenate([p["w_router_expert"][l], p["w_router_group"][l]], axis=1)
    wr = jnp.pad(wr, ((0, 0), (0, LANES - N_EXPERTS - N_GROUPS))).astype(F32)
    wrh = wr.astype(BF16)
    wrl = (wr - wrh.astype(F32)).astype(BF16)
    br = jnp.pad(jnp.concatenate([p["b_router_expert"][l], p["b_router_group"][l]]).astype(F32),
                 (0, LANES - N_EXPERTS - N_GROUPS))[None, :]
    return {
        "gmix": p["norm_mix"][l][None, :], "wab": w_in[:, 0:C_CQ].astype(BF16), "ws": w_small, "wuq": wuq_p, "wukv": wukv_p,
        "g512": g512, "g128": g128, "ncq": p["norm_cq"][l][None, :], "nckv": p["norm_ckv"][l][None, :],
        "wg": w_in[:, o_g:].astype(BF16), "wb": p["w_branch"][l].astype(BF16), "wo": p["w_out"][l].astype(BF16),
        "gffn": p["norm_ffn"][l][None, :], "wr": jnp.concatenate([wrh, wrl], axis=1), "br": br,
        "weg": p["w_expert_gate"][l].astype(BF16), "weu": p["w_expert_up"][l].astype(BF16),
        "wed": p["w_expert_down"][l].astype(BF16),
    }


def kernel(x, positions, rel_bias, norm_mix, w_in, b_forget, gq_a, gk_a, gq_b, gk_b, gq_c, gk_c, norm_cq, norm_ckv, w_uq, w_ukv, w_branch, w_out, norm_ffn, w_router_group, b_router_group, w_router_expert, b_router_expert, w_expert_gate, w_expert_up, w_expert_down):
    batch, seq_len, d_model = x.shape
    assert d_model == D_MODEL and seq_len % (Q_BLOCK * DILATED_PAIRS[-1][1]) == 0
    assert (batch * seq_len) % max(FRONT_TM, MERGE_TM, MOE_TM) == 0 and seq_len % FRONT_TM == 0
    p = dict(norm_mix=norm_mix, w_in=w_in, b_forget=b_forget, gq_a=gq_a, gk_a=gk_a, gq_b=gq_b, gk_b=gk_b,
             gq_c=gq_c, gk_c=gk_c, norm_cq=norm_cq, norm_ckv=norm_ckv, w_uq=w_uq, w_ukv=w_ukv,
             w_branch=w_branch, w_out=w_out, norm_ffn=norm_ffn, w_router_group=w_router_group,
             b_router_group=b_router_group, w_router_expert=w_router_expert, b_router_expert=b_router_expert,
             w_expert_gate=w_expert_gate, w_expert_up=w_expert_up, w_expert_down=w_expert_down)
    cw = _const_weights(positions, rel_bias)
    xs = x.reshape(batch * seq_len, d_model)
    for l in range(norm_mix.shape[0]):
        lw = _layer_weights(l, p)
        qa, ka, va, qb, kb, vb, qc, kc, vc, ccol, crow = _front_call(xs, cw, lw, seq_len)
        oa = _dilated_call(qa, ka, va, cw["bias"], batch, seq_len)
        ob = _flash_call(qb, kb, vb, batch, seq_len, crow, ccol)
        oc = _flash_call(qc, kc, vc, batch, seq_len)
        x1, gates = _merge_call(xs, oa, ob, oc, lw)
        xs = _moe_call(x1, gates, cw, lw)
    return xs.reshape(batch, seq_len, d_model)
```
